```python
import jax, jax.numpy as jnp
from jax import lax
import numpy as np

D_MODEL = 2048
BATCH = 16
SEQ = 2048
DEPTH = 1
DEC_BATCH = 32
DEC_SEQ = 32
PAST_LEN = 4096

CHUNK = 64
D_MIX = D_MODEL
D_ATTN = D_MIX // 2
D_LRU = D_MIX - D_ATTN
HEAD_DIM = 128
N_HEADS = D_ATTN // HEAD_DIM
N_LRU_BLOCKS = 8
LRU_BLOCK = D_LRU // N_LRU_BLOCKS
CONV_W = 4
LRU_C = 8.0
Q_BLOCK = 128
SPLITS = (D_ATTN, 2 * D_ATTN, 3 * D_ATTN, 3 * D_ATTN + N_HEADS,
          4 * D_ATTN + N_HEADS, 4 * D_ATTN + N_HEADS + D_LRU)
D_IN = 4 * D_ATTN + N_HEADS + 2 * D_LRU
ALPHA = (2.0 * DEPTH) ** 0.25
BETA = (8.0 * DEPTH) ** -0.25
LN_EPS = 1e-5
RMS_EPS = 1e-6

kernel_name = 'hymba_fox_rglru_streaming_step'

F32 = jnp.float32


def project(x, w_in, b_f):
    B, T, _ = x.shape
    z = jnp.einsum('btd,de->bte', x, w_in)
    o = SPLITS
    q = z[..., :o[0]].reshape(B, T, N_HEADS, HEAD_DIM)
    k = z[..., o[0]:o[1]].reshape(B, T, N_HEADS, HEAD_DIM)
    v = z[..., o[1]:o[2]].reshape(B, T, N_HEADS, HEAD_DIM)
    logf = jax.nn.log_sigmoid((z[..., o[2]:o[3]] + b_f).astype(F32))
    g_a = z[..., o[3]:o[4]]
    x_l = z[..., o[4]:o[5]]
    g_l = z[..., o[5]:]
    return q, k, v, logf, g_a, x_l, g_l


def fox_prompt(q, k, v, logf):
    B, S = q.shape[:2]
    scale = HEAD_DIM ** -0.5
    c = jnp.cumsum(logf, axis=1)
    cT = jnp.transpose(c, (0, 2, 1))
    nb = S // Q_BLOCK
    qb = jnp.transpose(q.reshape(B, nb, Q_BLOCK, N_HEADS, HEAD_DIM), (1, 0, 2, 3, 4))
    cb = jnp.transpose(c.reshape(B, nb, Q_BLOCK, N_HEADS), (1, 0, 3, 2))
    kpos = jnp.arange(S)

    def one_block(args):
        i, qi, ci = args
        s = jnp.einsum('bqhd,bkhd->bhqk', qi, k).astype(F32) * scale
        s = s + (ci[..., :, None] - cT[..., None, :])
        qpos = i * Q_BLOCK + jnp.arange(Q_BLOCK)
        mask = kpos[None, :] <= qpos[:, None]
        s = jnp.where(mask, s, -jnp.inf)
        p = jax.nn.softmax(s, axis=-1)
        return jnp.einsum('bhqk,bkhd->bqhd', p.astype(v.dtype), v)

    out = lax.map(one_block, (jnp.arange(nb), qb, cb))
    return jnp.transpose(out, (1, 0, 2, 3, 4)).reshape(B, S, N_HEADS, HEAD_DIM)


def fox_sample(q, k, v, logf, cache_k, cache_v, cache_logf):
    P = cache_k.shape[1]
    T = q.shape[1]
    scale = HEAD_DIM ** -0.5
    c_past = jnp.cumsum(cache_logf.astype(F32), axis=1)
    c_new = c_past[:, -1:, :] + jnp.cumsum(logf, axis=1)
    c_all = jnp.concatenate([c_past, c_new], axis=1)
    k_all = jnp.concatenate([cache_k, k.astype(cache_k.dtype)], axis=1)
    v_all = jnp.concatenate([cache_v, v.astype(cache_v.dtype)], axis=1)
    s = jnp.einsum('bqhd,bkhd->bhqk', q, k_all).astype(F32) * scale
    s = s + (jnp.transpose(c_new, (0, 2, 1))[..., :, None] - jnp.transpose(c_all, (0, 2, 1))[..., None, :])
    kpos = jnp.arange(P + T)
    qpos = P + jnp.arange(T)
    s = jnp.where(kpos[None, :] <= qpos[:, None], s, -jnp.inf)
    p = jax.nn.softmax(s, axis=-1)
    return jnp.einsum('bhqk,bkhd->bqhd', p.astype(v_all.dtype), v_all)


def rglru_branch(x_l, conv_hist, h0, reset_first, conv_w, conv_b, w_r, b_r, w_i, b_i, lru_lambda):
    B, T, _ = x_l.shape
    xpad = jnp.concatenate([conv_hist.astype(x_l.dtype), x_l], axis=1)
    xc = conv_b + xpad[:, 0:T] * conv_w[0]
    for j in range(1, CONV_W):
        xc = xc + xpad[:, j:j + T] * conv_w[j]
    new_hist = xpad[:, T:]
    xb = xc.reshape(B, T, N_LRU_BLOCKS, LRU_BLOCK)
    r = jax.nn.sigmoid((jnp.einsum('btnj,njk->btnk', xb, w_r).reshape(B, T, D_LRU) + b_r).astype(F32))
    i = jax.nn.sigmoid((jnp.einsum('btnj,njk->btnk', xb, w_i).reshape(B, T, D_LRU) + b_i).astype(F32))
    log_a = -LRU_C * r * jax.nn.softplus(-lru_lambda.astype(F32))
    a = jnp.exp(log_a)
    mult = jnp.sqrt(-jnp.expm1(2.0 * log_a))
    if reset_first:
        mult = jnp.where((jnp.arange(T) == 0)[None, :, None], 1.0, mult)
    u = mult * i * xc.astype(F32)

    def step(h, au):
        a_t, u_t = au
        h = a_t * h + u_t
        return h, h

    h_T, hs = lax.scan(step, h0.astype(F32), (jnp.swapaxes(a, 0, 1), jnp.swapaxes(u, 0, 1)))
    return jnp.swapaxes(hs, 0, 1).astype(x_l.dtype), h_T, new_hist


def group_rms(y, g, n_groups):
    B, T, D = y.shape
    yg = y.astype(F32).reshape(B, T, n_groups, D // n_groups)
    yg = yg * lax.rsqrt(jnp.mean(jnp.square(yg), axis=-1, keepdims=True) + RMS_EPS)
    return yg.reshape(B, T, D) * g.astype(F32)


def mix_and_norm(x, attn, lru, g_a, g_l, g_attn, g_lru, w_out, ln_g, ln_b):
    B, T, _ = x.shape
    ya = group_rms(attn.reshape(B, T, D_ATTN), g_attn, N_HEADS) * jax.nn.silu(g_a.astype(F32))
    yl = group_rms(lru, g_lru, N_LRU_BLOCKS) * jax.nn.silu(g_l.astype(F32))
    ycat = jnp.concatenate([ya, yl], axis=-1).astype(x.dtype)
    out = jnp.einsum('bte,ed->btd', ycat, w_out)
    h = ALPHA * x.astype(F32) + out.astype(F32)
    mu = jnp.mean(h, axis=-1, keepdims=True)
    var = jnp.mean(jnp.square(h - mu), axis=-1, keepdims=True)
    return ((h - mu) * lax.rsqrt(var + LN_EPS) * ln_g + ln_b).astype(x.dtype)


def setup_inputs(seed: int = 0) -> dict:
    key = jax.random.key(seed)
    ks = jax.random.split(key, 22)
    nrm = jax.random.normal
    x_prompt = nrm(ks[0], (BATCH, SEQ, D_MODEL), F32)
    x_sample = nrm(ks[1], (DEC_BATCH, DEC_SEQ, D_MODEL), F32)
    cache_k = nrm(ks[2], (DEPTH, DEC_BATCH, PAST_LEN, N_HEADS, HEAD_DIM), F32)
    cache_v = nrm(ks[3], (DEPTH, DEC_BATCH, PAST_LEN, N_HEADS, HEAD_DIM), F32)
    head_bias = jnp.linspace(2.0, 7.0, N_HEADS, dtype=F32)
    cache_logf = jax.nn.log_sigmoid(nrm(ks[4], (DEPTH, DEC_BATCH, PAST_LEN, N_HEADS), F32) + head_bias)
    state_h = 0.5 * nrm(ks[5], (DEPTH, DEC_BATCH, D_LRU), F32)
    state_conv = nrm(ks[6], (DEPTH, DEC_BATCH, CONV_W - 1, D_LRU), F32)
    w_in = nrm(ks[7], (DEPTH, D_MODEL, D_IN), F32) * D_MODEL ** -0.5
    b_f = head_bias + 0.1 * nrm(ks[8], (DEPTH, N_HEADS), F32)
    conv_w = nrm(ks[9], (DEPTH, CONV_W, D_LRU), F32) * CONV_W ** -0.5
    conv_b = 0.01 * nrm(ks[10], (DEPTH, D_LRU), F32)
    w_r = nrm(ks[11], (DEPTH, N_LRU_BLOCKS, LRU_BLOCK, LRU_BLOCK), F32) * LRU_BLOCK ** -0.5
    b_r = 0.01 * nrm(ks[12], (DEPTH, D_LRU), F32)
    w_i = nrm(ks[13], (DEPTH, N_LRU_BLOCKS, LRU_BLOCK, LRU_BLOCK), F32) * LRU_BLOCK ** -0.5
    b_i = 0.01 * nrm(ks[14], (DEPTH, D_LRU), F32)
    a_target = jax.random.uniform(ks[15], (DEPTH, D_LRU), F32, minval=0.9, maxval=0.999)
    p = a_target ** (1.0 / LRU_C)
    lru_lambda = jnp.log(p) - jnp.log1p(-p)
    g_attn = 1.0 + 0.02 * nrm(ks[16], (DEPTH, D_ATTN), F32)
    g_lru = 1.0 + 0.02 * nrm(ks[17], (DEPTH, D_LRU), F32)
    w_out = nrm(ks[18], (DEPTH, D_MIX, D_MODEL), F32) * (D_MIX ** -0.5) * BETA
    ln_g = 1.0 + 0.02 * nrm(ks[19], (DEPTH, D_MODEL), F32)
    ln_b = 0.01 * nrm(ks[20], (DEPTH, D_MODEL), F32)
    return {'x_prompt': x_prompt, 'x_sample': x_sample, 'cache_k': cache_k, 'cache_v': cache_v,
            'cache_logf': cache_logf, 'state_h': state_h, 'state_conv': state_conv,
            'w_in': w_in, 'b_f': b_f, 'conv_w': conv_w, 'conv_b': conv_b, 'w_r': w_r, 'b_r': b_r,
            'w_i': w_i, 'b_i': b_i, 'lru_lambda': lru_lambda, 'g_attn': g_attn, 'g_lru': g_lru,
            'w_out': w_out, 'ln_g': ln_g, 'ln_b': ln_b}


def reference(x_prompt, x_sample, cache_k, cache_v, cache_logf, state_h, state_conv,
              w_in, b_f, conv_w, conv_b, w_r, b_r, w_i, b_i, lru_lambda,
              g_attn, g_lru, w_out, ln_g, ln_b):
    xp, xs = x_prompt, x_sample
    kp_l, vp_l, fp_l, hp_l, cp_l = [], [], [], [], []
    ks_l, vs_l, fs_l, hs_l, cs_l = [], [], [], [], []
    for l in range(DEPTH):
        lru_w = (conv_w[l], conv_b[l], w_r[l], b_r[l], w_i[l], b_i[l], lru_lambda[l])
        q, k, v, logf, g_a, x_l, g_l = project(xp, w_in[l], b_f[l])
        attn = fox_prompt(q, k, v, logf)
        hist0 = jnp.zeros((xp.shape[0], CONV_W - 1, D_LRU), xp.dtype)
        h0 = jnp.zeros((xp.shape[0], D_LRU), F32)
        lru, h_T, hist_T = rglru_branch(x_l, hist0, h0, True, *lru_w)
        kp_l.append(k); vp_l.append(v); fp_l.append(logf); hp_l.append(h_T); cp_l.append(hist_T)
        xp = mix_and_norm(xp, attn, lru, g_a, g_l, g_attn[l], g_lru[l], w_out[l], ln_g[l], ln_b[l])
        q, k, v, logf, g_a, x_l, g_l = project(xs, w_in[l], b_f[l])
        attn = fox_sample(q, k, v, logf, cache_k[l], cache_v[l], cache_logf[l])
        lru, h_T, hist_T = rglru_branch(x_l, state_conv[l], state_h[l], False, *lru_w)
        ks_l.append(k); vs_l.append(v); fs_l.append(logf); hs_l.append(h_T); cs_l.append(hist_T)
        xs = mix_and_norm(xs, attn, lru, g_a, g_l, g_attn[l], g_lru[l], w_out[l], ln_g[l], ln_b[l])
    k_prompt = jnp.stack(kp_l, 0)
    v_prompt = jnp.stack(vp_l, 0)
    logf_prompt = jnp.stack(fp_l, 0)
    h_prompt = jnp.stack(hp_l, 0)
    conv_prompt = jnp.stack(cp_l, 0)
    k_sample = jnp.stack(ks_l, 0)
    v_sample = jnp.stack(vs_l, 0)
    logf_sample = jnp.stack(fs_l, 0)
    h_sample = jnp.stack(hs_l, 0)
    conv_sample = jnp.stack(cs_l, 0)
    return (xp, xs, k_prompt, v_prompt, logf_prompt, h_prompt, conv_prompt,
            k_sample, v_sample, logf_sample, h_sample, conv_sample)
```

```python
import functools

import jax
import jax.numpy as jnp
from jax import lax
from jax.experimental import pallas as pl
from jax.experimental.pallas import tpu as pltpu

F32 = jnp.float32
BF16 = jnp.bfloat16

HEAD_DIM = 128
LRU_BLOCK = 128
CONV_W = 4
LRU_C = 8.0
LN_EPS = 1e-5
RMS_EPS = 1e-6
SCALE = HEAD_DIM ** -0.5

LANES = 128
SUBLANES = 8
V7X_VMEM_LIMIT = 56 * 2 ** 20


def _dot(a, b):
    return jnp.dot(a, b, preferred_element_type=F32)


def _dot_nt(a, b):
    return lax.dot_general(a, b, (((1,), (1,)), ((), ())), preferred_element_type=F32)


def _softplus(y):
    return jnp.maximum(y, 0.0) + jnp.log1p(jnp.exp(-jnp.abs(y)))


def _log_sigmoid(y):
    return -_softplus(-y)


def _sigmoid(y):
    return 1.0 / (1.0 + jnp.exp(-y))


def _silu(y):
    return y * _sigmoid(y)


def _group_rms(y):
    return y * lax.rsqrt(jnp.mean(y * y, axis=-1, keepdims=True) + RMS_EPS)


def _params(*semantics):
    return pltpu.CompilerParams(dimension_semantics=semantics, vmem_limit_bytes=V7X_VMEM_LIMIT)


def _resident(shape):
    return pl.BlockSpec(shape, lambda *_: (0,) * len(shape), pipeline_mode=pl.Buffered(1))


def _in_proj_kernel(x_ref, wq_ref, wk_ref, wv_ref, wf_ref, wga_ref, wxl_ref, wgl_ref, bf_ref,
                    q_ref, k_ref, v_ref, kb_ref, vb_ref, logf_ref, ga_ref, xl_ref, gl_ref):
    n_heads = logf_ref.shape[-1]
    xb = x_ref[...].astype(BF16)
    q_ref[...] = (_dot(xb, wq_ref[...]) * SCALE).astype(BF16)
    k = _dot(xb, wk_ref[...])
    k_ref[...] = k
    kb_ref[...] = k.astype(BF16)
    v = _dot(xb, wv_ref[...])
    v_ref[...] = v
    vb_ref[...] = v.astype(BF16)
    zf = _dot(xb, wf_ref[...]) + bf_ref[...]
    logf_ref[...] = _log_sigmoid(zf)[:, :n_heads]
    ga_ref[...] = _dot(xb, wga_ref[...])
    xl_ref[...] = _dot(xb, wxl_ref[...])
    gl_ref[...] = _dot(xb, wgl_ref[...])


def _in_proj(x2d, wts, tm):
    M, D = x2d.shape
    wq, wk, wv, wf, wga, wxl, wgl, bf, n_heads = wts
    d_attn, d_lru = wq.shape[1], wxl.shape[1]
    row = lambda n: pl.BlockSpec((tm, n), lambda i: (i, 0))
    out_shape = (
        jax.ShapeDtypeStruct((M, d_attn), BF16),
        jax.ShapeDtypeStruct((M, d_attn), F32),
        jax.ShapeDtypeStruct((M, d_attn), F32),
        jax.ShapeDtypeStruct((M, d_attn), BF16),
        jax.ShapeDtypeStruct((M, d_attn), BF16),
        jax.ShapeDtypeStruct((M, n_heads), F32),
        jax.ShapeDtypeStruct((M, d_attn), F32),
        jax.ShapeDtypeStruct((M, d_lru), F32),
        jax.ShapeDtypeStruct((M, d_lru), F32),
    )
    return pl.pallas_call(
        _in_proj_kernel,
        grid=(M // tm,),
        in_specs=[row(D)] + [_resident(w.shape) for w in (wq, wk, wv, wf, wga, wxl, wgl, bf)],
        out_specs=tuple(row(s.shape[1]) for s in out_shape),
        out_shape=out_shape,
        compiler_params=_params("parallel"),
        name="in_proj",
    )(x2d, wq, wk, wv, wf, wga, wxl, wgl, bf)


def _cumsum_lanes(x):
    n = x.shape[-1]
    lane = lax.broadcasted_iota(jnp.int32, x.shape, x.ndim - 1)
    step = 1
    while step < n:
        x = x + jnp.where(lane >= step, pltpu.roll(x, step, x.ndim - 1), 0.0)
        step *= 2
    return x


def _cumsum_kernel(f_ref, c_ref):
    c_ref[0] = _cumsum_lanes(f_ref[0])


def _cumsum_carry_kernel(past_ref, new_ref, cpast_ref, cnew_ref):
    cp = _cumsum_lanes(past_ref[0])
    cpast_ref[0] = cp
    cnew_ref[0] = _cumsum_lanes(new_ref[0]) + cp[:, cp.shape[1] - 1:]


def _cumsum(fT):
    B, H, S = fT.shape
    spec = pl.BlockSpec((1, H, S), lambda b: (b, 0, 0))
    return pl.pallas_call(
        _cumsum_kernel, grid=(B,), in_specs=[spec], out_specs=spec,
        out_shape=jax.ShapeDtypeStruct(fT.shape, F32),
        compiler_params=_params("parallel"), name="cumsum",
    )(fT)


def _cumsum_carry(pastT, newT):
    B, H, P = pastT.shape
    N = newT.shape[2]
    pspec = pl.BlockSpec((1, H, P), lambda b: (b, 0, 0))
    nspec = pl.BlockSpec((1, H, N), lambda b: (b, 0, 0))
    return pl.pallas_call(
        _cumsum_carry_kernel, grid=(B,), in_specs=[pspec, nspec], out_specs=(pspec, nspec),
        out_shape=(jax.ShapeDtypeStruct(pastT.shape, F32), jax.ShapeDtypeStruct(newT.shape, F32)),
        compiler_params=_params("parallel"), name="cumsum_carry",
    )(pastT, newT)


def _head_epilogue(acc, l, gain, gate):
    return (_group_rms(acc / l) * gain * _silu(gate)).astype(BF16)


def _fox_prompt_kernel(q_ref, kb_ref, vb_ref, cT_ref, cc_ref, ga_ref, gattn_ref, ya_ref, *, tq):
    n_heads = cT_ref.shape[1]
    qi = pl.program_id(1)
    q0 = pl.multiple_of(qi * tq, tq)
    row = lax.broadcasted_iota(jnp.int32, (tq, tq), 0)
    col = lax.broadcasted_iota(jnp.int32, (tq, tq), 1)
    causal = col <= row
    for h in range(n_heads):
        hs = slice(h * HEAD_DIM, (h + 1) * HEAD_DIM)
        qh = q_ref[0, :, hs]
        ci = cc_ref[0, :, h:h + 1]

        def scores(k0, qh=qh, ci=ci, h=h, hs=hs):
            kj = kb_ref[0, pl.ds(k0, tq), hs]
            cj = cT_ref[0, h:h + 1, pl.ds(k0, tq)]
            return _dot_nt(qh, kj) + (ci - cj)

        s = jnp.where(causal, scores(q0), -jnp.inf)
        m = jnp.max(s, axis=-1, keepdims=True)
        p = jnp.exp(s - m)
        l = jnp.sum(p, axis=-1, keepdims=True)
        acc = _dot(p.astype(BF16), vb_ref[0, pl.ds(q0, tq), hs])

        def body(j, carry, scores=scores, hs=hs):
            m, l, acc = carry
            k0 = pl.multiple_of(j * tq, tq)
            s = scores(k0)
            m_new = jnp.maximum(m, jnp.max(s, axis=-1, keepdims=True))
            alpha = jnp.exp(m - m_new)
            p = jnp.exp(s - m_new)
            l = alpha * l + jnp.sum(p, axis=-1, keepdims=True)
            acc = alpha * acc + _dot(p.astype(BF16), vb_ref[0, pl.ds(k0, tq), hs])
            return m_new, l, acc

        m, l, acc = lax.fori_loop(0, qi, body, (m, l, acc))
        ya_ref[0, :, hs] = _head_epilogue(acc, l, gattn_ref[:, hs], ga_ref[0, :, hs])


def _fox_prompt(q, kb, vb, cT, cc, ga, gattn, tq):
    B, S, DA = q.shape
    H = cT.shape[1]
    blk = pl.BlockSpec((1, tq, DA), lambda b, i: (b, i, 0))
    seq = pl.BlockSpec((1, S, DA), lambda b, i: (b, 0, 0))
    return pl.pallas_call(
        functools.partial(_fox_prompt_kernel, tq=tq),
        grid=(B, S // tq),
        in_specs=[blk, seq, seq,
                  pl.BlockSpec((1, H, S), lambda b, i: (b, 0, 0)),
                  pl.BlockSpec((1, tq, H), lambda b, i: (b, i, 0)),
                  blk, _resident(gattn.shape)],
        out_specs=blk,
        out_shape=jax.ShapeDtypeStruct((B, S, DA), BF16),
        compiler_params=_params("parallel", "arbitrary"),
        name="fox_prompt",
    )(q, kb, vb, cT, cc, ga, gattn)


def _fox_sample_kernel(q_ref, ck_ref, cv_ref, cpT_ref, kn_ref, vn_ref, cnT_ref, cnc_ref, ga_ref,
                       gattn_ref, ya_ref, m_s, l_s, acc_s):
    n_heads = cpT_ref.shape[1]
    T = q_ref.shape[1]
    j = pl.program_id(1)

    @pl.when(j == 0)
    def _():
        m_s[...] = jnp.full(m_s.shape, -jnp.inf, F32)
        l_s[...] = jnp.zeros(l_s.shape, F32)
        acc_s[...] = jnp.zeros(acc_s.shape, F32)

    def update(h, s, vals):
        hs = slice(h * HEAD_DIM, (h + 1) * HEAD_DIM)
        m_old = m_s[:, h:h + 1]
        m_new = jnp.maximum(m_old, jnp.max(s, axis=-1, keepdims=True))
        alpha = jnp.exp(m_old - m_new)
        p = jnp.exp(s - m_new)
        l_s[:, h:h + 1] = alpha * l_s[:, h:h + 1] + jnp.sum(p, axis=-1, keepdims=True)
        acc_s[:, hs] = alpha * acc_s[:, hs] + _dot(p.astype(BF16), vals)
        m_s[:, h:h + 1] = m_new

    for h in range(n_heads):
        hs = slice(h * HEAD_DIM, (h + 1) * HEAD_DIM)
        s = _dot_nt(q_ref[0, :, hs], ck_ref[0, :, hs].astype(BF16))
        s = s + (cnc_ref[0, :, h:h + 1] - cpT_ref[0, h:h + 1, :])
        update(h, s, cv_ref[0, :, hs].astype(BF16))

    @pl.when(j == pl.num_programs(1) - 1)
    def _():
        row = lax.broadcasted_iota(jnp.int32, (T, T), 0)
        col = lax.broadcasted_iota(jnp.int32, (T, T), 1)
        for h in range(n_heads):
            hs = slice(h * HEAD_DIM, (h + 1) * HEAD_DIM)
            s = _dot_nt(q_ref[0, :, hs], kn_ref[0, :, hs])
            s = s + (cnc_ref[0, :, h:h + 1] - cnT_ref[0, h:h + 1, :T])
            update(h, jnp.where(col <= row, s, -jnp.inf), vn_ref[0, :, hs])
            ya_ref[0, :, hs] = _head_epilogue(acc_s[:, hs], l_s[:, h:h + 1], gattn_ref[:, hs],
                                              ga_ref[0, :, hs])


def _fox_sample(q, ck, cv, cpT, kn, vn, cnT, cnc, ga, gattn, tk):
    B, T, DA = q.shape
    P = ck.shape[1]
    H = cpT.shape[1]
    new = pl.BlockSpec((1, T, DA), lambda b, j: (b, 0, 0))
    past = pl.BlockSpec((1, tk, DA), lambda b, j: (b, j, 0))
    return pl.pallas_call(
        _fox_sample_kernel,
        grid=(B, P // tk),
        in_specs=[new, past, past,
                  pl.BlockSpec((1, H, tk), lambda b, j: (b, 0, j)),
                  new, new,
                  pl.BlockSpec((1, H, cnT.shape[2]), lambda b, j: (b, 0, 0)),
                  pl.BlockSpec((1, T, H), lambda b, j: (b, 0, 0)),
                  new, _resident(gattn.shape)],
        out_specs=new,
        out_shape=jax.ShapeDtypeStruct((B, T, DA), BF16),
        scratch_shapes=[pltpu.VMEM((T, LANES), F32), pltpu.VMEM((T, LANES), F32),
                        pltpu.VMEM((T, DA), F32)],
        compiler_params=_params("parallel", "arbitrary"),
        name="fox_sample",
    )(q, ck, cv, cpT, kn, vn, cnT, cnc, ga, gattn)


def _rglru_kernel(xl_ref, gl_ref, hist_ref, h0_ref, cw_ref, cb_ref, wr_ref, br_ref, wi_ref, bi_ref,
                  lam_ref, glru_ref, yl_ref, hT_ref, hist_out_ref, xpad_s, a_s, u_s, hcar_s,
                  *, tt, reset_first):
    n_blocks = wr_ref.shape[0]
    t = pl.program_id(1)
    pad = SUBLANES

    @pl.when(t == 0)
    def _():
        xpad_s[0:pad, :] = jnp.zeros((pad, xpad_s.shape[1]), F32)
        xpad_s[pad - (CONV_W - 1):pad, :] = hist_ref[0]
        hcar_s[...] = h0_ref[0]

    @pl.when(t > 0)
    def _():
        xpad_s[0:pad, :] = xpad_s[tt:tt + pad, :]

    xpad_s[pad:pad + tt, :] = xl_ref[0]

    xc = cb_ref[...] + xpad_s[pad - (CONV_W - 1):pad - (CONV_W - 1) + tt, :] * cw_ref[0:1, :]
    for j in range(1, CONV_W):
        off = pad - (CONV_W - 1) + j
        xc = xc + xpad_s[off:off + tt, :] * cw_ref[j:j + 1, :]
    hist_out_ref[0] = xpad_s[pad + tt - (CONV_W - 1):pad + tt, :]

    xcb = xc.astype(BF16)
    first = (lax.broadcasted_iota(jnp.int32, (tt, LRU_BLOCK), 0) + t * tt) == 0
    for n in range(n_blocks):
        ns = slice(n * LRU_BLOCK, (n + 1) * LRU_BLOCK)
        r = _sigmoid(_dot(xcb[:, ns], wr_ref[n]) + br_ref[:, ns])
        i = _sigmoid(_dot(xcb[:, ns], wi_ref[n]) + bi_ref[:, ns])
        log_a = -LRU_C * r * _softplus(-lam_ref[:, ns])
        a = jnp.exp(log_a)
        mult = jnp.sqrt(-jnp.tanh(log_a) * (1.0 + a * a))
        if reset_first:
            mult = jnp.where(first, 1.0, mult)
        a_s[:, ns] = a
        u_s[:, ns] = mult * i * xc[:, ns]

    sub = lax.broadcasted_iota(jnp.int32, (SUBLANES, a_s.shape[1]), 0)

    def tile(k, h_prev):
        r0 = pl.multiple_of(k * SUBLANES, SUBLANES)
        a = a_s[pl.ds(r0, SUBLANES), :]
        u = u_s[pl.ds(r0, SUBLANES), :]
        for step in (1, 2, 4):
            keep = sub >= step
            u = u + a * jnp.where(keep, pltpu.roll(u, step, 0), 0.0)
            a = a * jnp.where(keep, pltpu.roll(a, step, 0), 1.0)
        h = u + a * h_prev
        u_s[pl.ds(r0, SUBLANES), :] = h
        return h[SUBLANES - 1:SUBLANES, :]

    h_last = lax.fori_loop(0, tt // SUBLANES, tile, hcar_s[...])
    hcar_s[...] = h_last
    hT_ref[0] = h_last

    for n in range(n_blocks):
        ns = slice(n * LRU_BLOCK, (n + 1) * LRU_BLOCK)
        y = _group_rms(u_s[:, ns]) * glru_ref[:, ns] * _silu(gl_ref[0, :, ns])
        yl_ref[0, :, ns] = y.astype(BF16)


def _rglru(xl, gl, hist, h0, lw, tt, reset_first):
    B, T, DL = xl.shape
    cw, cb, wr, br, wi, bi, lam, glru = lw
    blk = pl.BlockSpec((1, tt, DL), lambda b, t: (b, t, 0))
    per_b = lambda n: pl.BlockSpec((1, n, DL), lambda b, t: (b, 0, 0))
    return pl.pallas_call(
        functools.partial(_rglru_kernel, tt=tt, reset_first=reset_first),
        grid=(B, T // tt),
        in_specs=[blk, blk, per_b(CONV_W - 1), per_b(1)]
                 + [_resident(w.shape) for w in (cw, cb, wr, br, wi, bi, lam, glru)],
        out_specs=(blk, per_b(1), per_b(CONV_W - 1)),
        out_shape=(jax.ShapeDtypeStruct((B, T, DL), BF16),
                   jax.ShapeDtypeStruct((B, 1, DL), F32),
                   jax.ShapeDtypeStruct((B, CONV_W - 1, DL), F32)),
        scratch_shapes=[pltpu.VMEM((SUBLANES + tt, DL), F32), pltpu.VMEM((tt, DL), F32),
                        pltpu.VMEM((tt, DL), F32), pltpu.VMEM((1, DL), F32)],
        compiler_params=_params("parallel", "arbitrary"),
        name="rglru",
    )(xl, gl, hist, h0, cw, cb, wr, br, wi, bi, lam, glru)


def _out_proj_kernel(x_ref, ya_ref, yl_ref, wa_ref, wl_ref, g_ref, b_ref, o_ref, *, alpha):
    out = _dot(ya_ref[...], wa_ref[...]) + _dot(yl_ref[...], wl_ref[...])
    h = alpha * x_ref[...] + out
    mu = jnp.mean(h, axis=-1, keepdims=True)
    d = h - mu
    var = jnp.mean(d * d, axis=-1, keepdims=True)
    o_ref[...] = d * lax.rsqrt(var + LN_EPS) * g_ref[...] + b_ref[...]


def _out_proj(x2d, ya, yl, wa, wl, g, b, alpha, tm):
    M, D = x2d.shape
    row = lambda n: pl.BlockSpec((tm, n), lambda i: (i, 0))
    return pl.pallas_call(
        functools.partial(_out_proj_kernel, alpha=alpha),
        grid=(M // tm,),
        in_specs=[row(D), row(ya.shape[1]), row(yl.shape[1])]
                 + [_resident(w.shape) for w in (wa, wl, g, b)],
        out_specs=row(D),
        out_shape=jax.ShapeDtypeStruct((M, D), F32),
        compiler_params=_params("parallel"),
        name="out_proj",
    )(x2d, ya, yl, wa, wl, g, b)


def _tile(n, pref):
    return pref if n % pref == 0 else n


def kernel(x_prompt, x_sample, cache_k, cache_v, cache_logf, state_h, state_conv, w_in, b_f, conv_w,
           conv_b, w_r, b_r, w_i, b_i, lru_lambda, g_attn, g_lru, w_out, ln_g, ln_b):
    depth, d_model, _ = w_in.shape
    n_heads = b_f.shape[1]
    d_attn = n_heads * HEAD_DIM
    d_lru = lru_lambda.shape[1]
    alpha = (2.0 * depth) ** 0.25
    B, S, _ = x_prompt.shape
    DB, T, _ = x_sample.shape
    P = cache_k.shape[2]

    xp = x_prompt.reshape(B * S, d_model)
    xs = x_sample.reshape(DB * T, d_model)
    outs_p, outs_s = [], []
    for l in range(depth):
        w = w_in[l]
        o = (d_attn, 2 * d_attn, 3 * d_attn, 3 * d_attn + n_heads, 4 * d_attn + n_heads,
             4 * d_attn + n_heads + d_lru)
        wf = jnp.pad(w[:, o[2]:o[3]], ((0, 0), (0, LANES - n_heads)))
        bf = jnp.pad(b_f[l][None, :], ((0, 0), (0, LANES - n_heads)))
        wts = (w[:, :o[0]].astype(BF16), w[:, o[0]:o[1]].astype(BF16), w[:, o[1]:o[2]].astype(BF16),
               wf.astype(BF16), w[:, o[3]:o[4]].astype(BF16), w[:, o[4]:o[5]].astype(BF16),
               w[:, o[5]:].astype(BF16), bf, n_heads)
        lw = (conv_w[l], conv_b[l][None], w_r[l].astype(BF16), b_r[l][None], w_i[l].astype(BF16),
              b_i[l][None], lru_lambda[l][None], g_lru[l][None])
        gattn = g_attn[l][None]
        wa = w_out[l][:d_attn].astype(BF16)
        wl = w_out[l][d_attn:].astype(BF16)
        lng, lnb = ln_g[l][None], ln_b[l][None]

        q, k, v, kb, vb, logf, ga, xl, gl = _in_proj(xp, wts, _tile(B * S, 256))
        r3 = lambda a: a.reshape(B, S, a.shape[-1])
        cT = _cumsum(jnp.transpose(r3(logf), (0, 2, 1)))
        cc = jnp.transpose(cT, (0, 2, 1))
        ya = _fox_prompt(r3(q), r3(kb), r3(vb), cT, cc, r3(ga), gattn, _tile(S, 256))
        yl, h_T, hist_T = _rglru(r3(xl), r3(gl), jnp.zeros((B, CONV_W - 1, d_lru), F32),
                                 jnp.zeros((B, 1, d_lru), F32), lw, _tile(S, 256), True)
        outs_p.append((k.reshape(B, S, n_heads, HEAD_DIM), v.reshape(B, S, n_heads, HEAD_DIM),
                       r3(logf), h_T.reshape(B, d_lru), hist_T))
        xp = _out_proj(xp, ya.reshape(B * S, d_attn), yl.reshape(B * S, d_lru), wa, wl, lng, lnb,
                       alpha, _tile(B * S, 256))

        q, k, v, kb, vb, logf, ga, xl, gl = _in_proj(xs, wts, _tile(DB * T, 256))
        r3 = lambda a: a.reshape(DB, T, a.shape[-1])
        newT = jnp.pad(jnp.transpose(r3(logf), (0, 2, 1)), ((0, 0), (0, 0), (0, LANES - T)))
        cpT, cnT = _cumsum_carry(jnp.transpose(cache_logf[l].astype(F32), (0, 2, 1)), newT)
        cnc = jnp.transpose(cnT[:, :, :T], (0, 2, 1))
        ya = _fox_sample(r3(q), cache_k[l].reshape(DB, P, d_attn), cache_v[l].reshape(DB, P, d_attn),
                         cpT, r3(kb), r3(vb), cnT, cnc, r3(ga), gattn, _tile(P, 1024))
        yl, h_T, hist_T = _rglru(r3(xl), r3(gl), state_conv[l], state_h[l][:, None, :], lw,
                                 _tile(T, 256), False)
        outs_s.append((k.reshape(DB, T, n_heads, HEAD_DIM), v.reshape(DB, T, n_heads, HEAD_DIM),
                       r3(logf), h_T.reshape(DB, d_lru), hist_T))
        xs = _out_proj(xs, ya.reshape(DB * T, d_attn), yl.reshape(DB * T, d_lru), wa, wl, lng, lnb,
                       alpha, _tile(DB * T, 256))

    stack = lambda outs, i: jnp.stack([o[i] for o in outs], 0)
    return (xp.reshape(B, S, d_model), xs.reshape(DB, T, d_model),
            stack(outs_p, 0), stack(outs_p, 1), stack(outs_p, 2), stack(outs_p, 3), stack(outs_p, 4),
            stack(outs_s, 0), stack(outs_s, 1), stack(outs_s, 2), stack(outs_s, 3), stack(outs_s, 4))
```

```python
import functools

import jax
import jax.numpy as jnp
from jax import lax
from jax.experimental import pallas as pl
from jax.experimental.pallas import tpu as pltpu

F32 = jnp.float32
BF16 = jnp.bfloat16

HEAD_DIM = 128
LRU_BLOCK = 128
CONV_W = 4
LRU_C = 8.0
LN_EPS = 1e-5
RMS_EPS = 1e-6
LOG2E = 1.4426950408889634
Q_SCALE = HEAD_DIM ** -0.5 * LOG2E

LANES = 128
SUBLANES = 8
V7X_VMEM_LIMIT = 56 * 2 ** 20


def _dot(a, b):
    return jnp.dot(a, b, preferred_element_type=F32)


def _dot_nt(a, b):
    return lax.dot_general(a, b, (((1,), (1,)), ((), ())), preferred_element_type=F32)


def _softplus(y):
    return jnp.maximum(y, 0.0) + jnp.log1p(jnp.exp(-jnp.abs(y)))


def _log_sigmoid(y):
    return -_softplus(-y)


def _sigmoid(y):
    return 1.0 / (1.0 + jnp.exp(-y))


def _silu(y):
    return y * _sigmoid(y)


def _group_rms(y):
    return y * lax.rsqrt(jnp.mean(y * y, axis=-1, keepdims=True) + RMS_EPS)


def _params(*semantics):
    return pltpu.CompilerParams(dimension_semantics=semantics, vmem_limit_bytes=V7X_VMEM_LIMIT)


def _resident(shape):
    return pl.BlockSpec(shape, lambda *_: (0,) * len(shape), pipeline_mode=pl.Buffered(1))


def _in_proj_kernel(x_ref, wq_ref, wk_ref, wv_ref, wf_ref, wga_ref, wxl_ref, wgl_ref, bf_ref,
                    q_ref, k_ref, v_ref, kb_ref, vb_ref, logf_ref, ga_ref, xl_ref, gl_ref):
    n_heads = logf_ref.shape[-1]
    xb = x_ref[...].astype(BF16)
    q_ref[...] = (_dot(xb, wq_ref[...]) * Q_SCALE).astype(BF16)
    k = _dot(xb, wk_ref[...])
    k_ref[...] = k
    kb_ref[...] = k.astype(BF16)
    v = _dot(xb, wv_ref[...])
    v_ref[...] = v
    vb_ref[...] = v.astype(BF16)
    zf = _dot(xb, wf_ref[...]) + bf_ref[...]
    logf_ref[...] = _log_sigmoid(zf)[:, :n_heads]
    ga_ref[...] = _dot(xb, wga_ref[...])
    xl_ref[...] = _dot(xb, wxl_ref[...])
    gl_ref[...] = _dot(xb, wgl_ref[...])


def _in_proj(x2d, wts, tm):
    M, D = x2d.shape
    wq, wk, wv, wf, wga, wxl, wgl, bf, n_heads = wts
    d_attn, d_lru = wq.shape[1], wxl.shape[1]
    row = lambda n: pl.BlockSpec((tm, n), lambda i: (i, 0))
    out_shape = (
        jax.ShapeDtypeStruct((M, d_attn), BF16),
        jax.ShapeDtypeStruct((M, d_attn), F32),
        jax.ShapeDtypeStruct((M, d_attn), F32),
        jax.ShapeDtypeStruct((M, d_attn), BF16),
        jax.ShapeDtypeStruct((M, d_attn), BF16),
        jax.ShapeDtypeStruct((M, n_heads), F32),
        jax.ShapeDtypeStruct((M, d_attn), F32),
        jax.ShapeDtypeStruct((M, d_lru), F32),
        jax.ShapeDtypeStruct((M, d_lru), F32),
    )
    return pl.pallas_call(
        _in_proj_kernel,
        grid=(M // tm,),
        in_specs=[row(D)] + [_resident(w.shape) for w in (wq, wk, wv, wf, wga, wxl, wgl, bf)],
        out_specs=tuple(row(s.shape[1]) for s in out_shape),
        out_shape=out_shape,
        compiler_params=_params("parallel"),
        name="in_proj",
    )(x2d, wq, wk, wv, wf, wga, wxl, wgl, bf)


def _cumsum_lanes(x):
    n = x.shape[-1]
    lane = lax.broadcasted_iota(jnp.int32, x.shape, x.ndim - 1)
    step = 1
    while step < n:
        x = x + jnp.where(lane >= step, pltpu.roll(x, step, x.ndim - 1), 0.0)
        step *= 2
    return x


def _cumsum_kernel(f_ref, c_ref):
    c_ref[0] = _cumsum_lanes(f_ref[0]) * LOG2E


def _cumsum_carry_kernel(past_ref, new_ref, cpast_ref, cnew_ref):
    cp = _cumsum_lanes(past_ref[0])
    cpast_ref[0] = cp * LOG2E
    cnew_ref[0] = (_cumsum_lanes(new_ref[0]) + cp[:, cp.shape[1] - 1:]) * LOG2E


def _cumsum(fT):
    B, H, S = fT.shape
    spec = pl.BlockSpec((1, H, S), lambda b: (b, 0, 0))
    return pl.pallas_call(
        _cumsum_kernel, grid=(B,), in_specs=[spec], out_specs=spec,
        out_shape=jax.ShapeDtypeStruct(fT.shape, F32),
        compiler_params=_params("parallel"), name="cumsum",
    )(fT)


def _cumsum_carry(pastT, newT):
    B, H, P = pastT.shape
    N = newT.shape[2]
    pspec = pl.BlockSpec((1, H, P), lambda b: (b, 0, 0))
    nspec = pl.BlockSpec((1, H, N), lambda b: (b, 0, 0))
    return pl.pallas_call(
        _cumsum_carry_kernel, grid=(B,), in_specs=[pspec, nspec], out_specs=(pspec, nspec),
        out_shape=(jax.ShapeDtypeStruct(pastT.shape, F32), jax.ShapeDtypeStruct(newT.shape, F32)),
        compiler_params=_params("parallel"), name="cumsum_carry",
    )(pastT, newT)


def _head_epilogue(acc, l, gain, gate):
    return (_group_rms(acc / l) * gain * _silu(gate)).astype(BF16)


def _fox_prompt_kernel(q_ref, kb_ref, vb_ref, cT_ref, ga_ref, gattn_ref, ya_ref,
                       vp_s, m_s, acc_s, *, tq):
    n_heads = cT_ref.shape[1]
    qi = pl.program_id(1)
    q0 = pl.multiple_of(qi * tq, tq)

    @pl.when(qi == 0)
    def _():
        ones = jnp.ones((vb_ref.shape[1], HEAD_DIM), BF16)
        for h in range(n_heads):
            vp_s[h, :, :HEAD_DIM] = vb_ref[0, :, h * HEAD_DIM:(h + 1) * HEAD_DIM]
            vp_s[h, :, HEAD_DIM:] = ones

    def scores(h, k0):
        hs = slice(h * HEAD_DIM, (h + 1) * HEAD_DIM)
        s = _dot_nt(q_ref[0, :, hs], kb_ref[0, pl.ds(k0, tq), hs])
        return s - cT_ref[0, h:h + 1, pl.ds(k0, tq)]

    def row_max(s):
        m = s[:, :LANES]
        for c in range(1, s.shape[1] // LANES):
            m = jnp.maximum(m, s[:, c * LANES:(c + 1) * LANES])
        return jnp.max(m, axis=-1, keepdims=True)

    def weights(s, m):
        parts = [jnp.exp2(s[:, c * LANES:(c + 1) * LANES] - m) for c in range(s.shape[1] // LANES)]
        return jnp.concatenate(parts, axis=1).astype(BF16)

    row = lax.broadcasted_iota(jnp.int32, (tq, tq), 0)
    col = lax.broadcasted_iota(jnp.int32, (tq, tq), 1)
    for h in range(n_heads):
        s = jnp.where(col <= row, scores(h, q0), -jnp.inf)
        m = jnp.broadcast_to(row_max(s), (tq, LANES))
        m_s[h] = m
        acc_s[h] = _dot(weights(s, m), vp_s[h, pl.ds(q0, tq), :])

    def body(j, carry):
        k0 = pl.multiple_of(j * tq, tq)
        for h in range(n_heads):
            s = scores(h, k0)
            m_old = m_s[h]
            m_new = jnp.maximum(m_old, row_max(s))
            alpha = jnp.exp2(m_old - m_new)
            m_s[h] = m_new
            pv = _dot(weights(s, m_new), vp_s[h, pl.ds(k0, tq), :])
            acc_s[h, :, :HEAD_DIM] = alpha * acc_s[h, :, :HEAD_DIM] + pv[:, :HEAD_DIM]
            acc_s[h, :, HEAD_DIM:] = alpha * acc_s[h, :, HEAD_DIM:] + pv[:, HEAD_DIM:]
        return carry

    lax.fori_loop(0, qi, body, 0)

    for h in range(n_heads):
        hs = slice(h * HEAD_DIM, (h + 1) * HEAD_DIM)
        ya_ref[0, :, hs] = _head_epilogue(acc_s[h, :, :HEAD_DIM], acc_s[h, :, HEAD_DIM:],
                                          gattn_ref[:, hs], ga_ref[0, :, hs])


def _fox_prompt(q, kb, vb, cT, ga, gattn, tq):
    B, S, DA = q.shape
    H = cT.shape[1]
    blk = pl.BlockSpec((1, tq, DA), lambda b, i: (b, i, 0))
    seq = pl.BlockSpec((1, S, DA), lambda b, i: (b, 0, 0))
    return pl.pallas_call(
        functools.partial(_fox_prompt_kernel, tq=tq),
        grid=(B, S // tq),
        in_specs=[blk, seq, seq, pl.BlockSpec((1, H, S), lambda b, i: (b, 0, 0)), blk,
                  _resident(gattn.shape)],
        out_specs=blk,
        out_shape=jax.ShapeDtypeStruct((B, S, DA), BF16),
        scratch_shapes=[pltpu.VMEM((H, S, 2 * HEAD_DIM), BF16),
                        pltpu.VMEM((H, tq, LANES), F32),
                        pltpu.VMEM((H, tq, 2 * HEAD_DIM), F32)],
        compiler_params=_params("parallel", "arbitrary"),
        name="fox_prompt",
    )(q, kb, vb, cT, ga, gattn)


def _fox_sample_kernel(q_ref, ck_ref, cv_ref, cpT_ref, kn_ref, vn_ref, cnT_ref, cnc_ref, ga_ref,
                       gattn_ref, ya_ref, m_s, l_s, acc_s):
    n_heads = cpT_ref.shape[1]
    T = q_ref.shape[1]
    tk = cpT_ref.shape[2]
    j = pl.program_id(1)

    @pl.when(j == 0)
    def _():
        m_s[...] = jnp.full(m_s.shape, -jnp.inf, F32)
        l_s[...] = jnp.zeros(l_s.shape, F32)
        acc_s[...] = jnp.zeros(acc_s.shape, F32)

    def update(h, s, vals):
        m_old = m_s[h]
        m_new = jnp.maximum(m_old, jnp.max(s, axis=-1, keepdims=True))
        alpha = jnp.exp2(m_old - m_new)
        p = jnp.exp2(s - m_new)
        m_s[h] = m_new
        l_s[h] = alpha * l_s[h] + jnp.sum(p, axis=-1, keepdims=True)
        acc_s[h] = alpha * acc_s[h] + _dot(p.astype(BF16), vals)

    for h in range(n_heads):
        hs = slice(h * HEAD_DIM, (h + 1) * HEAD_DIM)
        kh = ck_ref[0, pl.ds(h, tk, stride=n_heads), :].astype(BF16)
        vh = cv_ref[0, pl.ds(h, tk, stride=n_heads), :].astype(BF16)
        s = _dot_nt(q_ref[0, :, hs], kh) + (cnc_ref[0, :, h:h + 1] - cpT_ref[0, h:h + 1, :])
        update(h, s, vh)

    @pl.when(j == pl.num_programs(1) - 1)
    def _():
        row = lax.broadcasted_iota(jnp.int32, (T, T), 0)
        col = lax.broadcasted_iota(jnp.int32, (T, T), 1)
        for h in range(n_heads):
            hs = slice(h * HEAD_DIM, (h + 1) * HEAD_DIM)
            s = _dot_nt(q_ref[0, :, hs], kn_ref[0, :, hs])
            s = s + (cnc_ref[0, :, h:h + 1] - cnT_ref[0, h:h + 1, :T])
            update(h, jnp.where(col <= row, s, -jnp.inf), vn_ref[0, :, hs])
            ya_ref[0, :, hs] = _head_epilogue(acc_s[h], l_s[h], gattn_ref[:, hs], ga_ref[0, :, hs])


def _fox_sample(q, ck, cv, cpT, kn, vn, cnT, cnc, ga, gattn, tk):
    B, T, DA = q.shape
    H = cpT.shape[1]
    P = ck.shape[1] // H
    new = pl.BlockSpec((1, T, DA), lambda b, j: (b, 0, 0))
    past = pl.BlockSpec((1, tk * H, HEAD_DIM), lambda b, j: (b, j, 0))
    return pl.pallas_call(
        _fox_sample_kernel,
        grid=(B, P // tk),
        in_specs=[new, past, past,
                  pl.BlockSpec((1, H, tk), lambda b, j: (b, 0, j)),
                  new, new,
                  pl.BlockSpec((1, H, cnT.shape[2]), lambda b, j: (b, 0, 0)),
                  pl.BlockSpec((1, T, H), lambda b, j: (b, 0, 0)),
                  new, _resident(gattn.shape)],
        out_specs=new,
        out_shape=jax.ShapeDtypeStruct((B, T, DA), BF16),
        scratch_shapes=[pltpu.VMEM((H, T, 1), F32), pltpu.VMEM((H, T, 1), F32),
                        pltpu.VMEM((H, T, HEAD_DIM), F32)],
        compiler_params=_params("parallel", "arbitrary"),
        name="fox_sample",
    )(q, ck, cv, cpT, kn, vn, cnT, cnc, ga, gattn)


def _rglru_kernel(xl_ref, gl_ref, hist_ref, h0_ref, cw_ref, cb_ref, wr_ref, br_ref, wi_ref, bi_ref,
                  lam_ref, glru_ref, yl_ref, hT_ref, hist_out_ref, xpad_s, a_s, u_s, hcar_s,
                  *, tt, reset_first):
    n_blocks = wr_ref.shape[0]
    t = pl.program_id(1)
    pad = SUBLANES

    @pl.when(t == 0)
    def _():
        xpad_s[0:pad, :] = jnp.zeros((pad, xpad_s.shape[1]), F32)
        xpad_s[pad - (CONV_W - 1):pad, :] = hist_ref[0]
        hcar_s[...] = h0_ref[0]

    @pl.when(t > 0)
    def _():
        xpad_s[0:pad, :] = xpad_s[tt:tt + pad, :]

    xpad_s[pad:pad + tt, :] = xl_ref[0]

    xc = cb_ref[...] + xpad_s[pad - (CONV_W - 1):pad - (CONV_W - 1) + tt, :] * cw_ref[0:1, :]
    for j in range(1, CONV_W):
        off = pad - (CONV_W - 1) + j
        xc = xc + xpad_s[off:off + tt, :] * cw_ref[j:j + 1, :]
    hist_out_ref[0] = xpad_s[pad + tt - (CONV_W - 1):pad + tt, :]

    xcb = xc.astype(BF16)
    first = (lax.broadcasted_iota(jnp.int32, (tt, LRU_BLOCK), 0) + t * tt) == 0
    for n in range(n_blocks):
        ns = slice(n * LRU_BLOCK, (n + 1) * LRU_BLOCK)
        r = _sigmoid(_dot(xcb[:, ns], wr_ref[n]) + br_ref[:, ns])
        i = _sigmoid(_dot(xcb[:, ns], wi_ref[n]) + bi_ref[:, ns])
        log_a = -LRU_C * r * _softplus(-lam_ref[:, ns])
        a = jnp.exp(log_a)
        mult = jnp.sqrt(-jnp.tanh(log_a) * (1.0 + a * a))
        if reset_first:
            mult = jnp.where(first, 1.0, mult)
        a_s[:, ns] = a
        u_s[:, ns] = mult * i * xc[:, ns]

    sub = lax.broadcasted_iota(jnp.int32, (SUBLANES, a_s.shape[1]), 0)

    def tile(k, h_prev):
        r0 = pl.multiple_of(k * SUBLANES, SUBLANES)
        a = a_s[pl.ds(r0, SUBLANES), :]
        u = u_s[pl.ds(r0, SUBLANES), :]
        for step in (1, 2, 4):
            keep = sub >= step
            u = u + a * jnp.where(keep, pltpu.roll(u, step, 0), 0.0)
            a = a * jnp.where(keep, pltpu.roll(a, step, 0), 1.0)
        h = u + a * h_prev
        u_s[pl.ds(r0, SUBLANES), :] = h
        return h[SUBLANES - 1:SUBLANES, :]

    h_last = lax.fori_loop(0, tt // SUBLANES, tile, hcar_s[...])
    hcar_s[...] = h_last
    hT_ref[0] = h_last

    for n in range(n_blocks):
        ns = slice(n * LRU_BLOCK, (n + 1) * LRU_BLOCK)
        y = _group_rms(u_s[:, ns]) * glru_ref[:, ns] * _silu(gl_ref[0, :, ns])
        yl_ref[0, :, ns] = y.astype(BF16)


def _rglru(xl, gl, hist, h0, lw, tt, reset_first):
    B, T, DL = xl.shape
    cw, cb, wr, br, wi, bi, lam, glru = lw
    blk = pl.BlockSpec((1, tt, DL), lambda b, t: (b, t, 0))
    per_b = lambda n: pl.BlockSpec((1, n, DL), lambda b, t: (b, 0, 0))
    return pl.pallas_call(
        functools.partial(_rglru_kernel, tt=tt, reset_first=reset_first),
        grid=(B, T // tt),
        in_specs=[blk, blk, per_b(CONV_W - 1), per_b(1)]
                 + [_resident(w.shape) for w in (cw, cb, wr, br, wi, bi, lam, glru)],
        out_specs=(blk, per_b(1), per_b(CONV_W - 1)),
        out_shape=(jax.ShapeDtypeStruct((B, T, DL), BF16),
                   jax.ShapeDtypeStruct((B, 1, DL), F32),
                   jax.ShapeDtypeStruct((B, CONV_W - 1, DL), F32)),
        scratch_shapes=[pltpu.VMEM((SUBLANES + tt, DL), F32), pltpu.VMEM((tt, DL), F32),
                        pltpu.VMEM((tt, DL), F32), pltpu.VMEM((1, DL), F32)],
        compiler_params=_params("parallel", "arbitrary"),
        name="rglru",
    )(xl, gl, hist, h0, cw, cb, wr, br, wi, bi, lam, glru)


def _out_proj_kernel(x_ref, ya_ref, yl_ref, wa_ref, wl_ref, g_ref, b_ref, o_ref, *, alpha):
    out = _dot(ya_ref[...], wa_ref[...]) + _dot(yl_ref[...], wl_ref[...])
    h = alpha * x_ref[...] + out
    mu = jnp.mean(h, axis=-1, keepdims=True)
    d = h - mu
    var = jnp.mean(d * d, axis=-1, keepdims=True)
    o_ref[...] = d * lax.rsqrt(var + LN_EPS) * g_ref[...] + b_ref[...]


def _out_proj(x2d, ya, yl, wa, wl, g, b, alpha, tm):
    M, D = x2d.shape
    row = lambda n: pl.BlockSpec((tm, n), lambda i: (i, 0))
    return pl.pallas_call(
        functools.partial(_out_proj_kernel, alpha=alpha),
        grid=(M // tm,),
        in_specs=[row(D), row(ya.shape[1]), row(yl.shape[1])]
                 + [_resident(w.shape) for w in (wa, wl, g, b)],
        out_specs=row(D),
        out_shape=jax.ShapeDtypeStruct((M, D), F32),
        compiler_params=_params("parallel"),
        name="out_proj",
    )(x2d, ya, yl, wa, wl, g, b)


def _tile(n, pref):
    return pref if n % pref == 0 else n


def kernel(x_prompt, x_sample, cache_k, cache_v, cache_logf, state_h, state_conv, w_in, b_f, conv_w,
           conv_b, w_r, b_r, w_i, b_i, lru_lambda, g_attn, g_lru, w_out, ln_g, ln_b):
    depth, d_model, _ = w_in.shape
    n_heads = b_f.shape[1]
    d_attn = n_heads * HEAD_DIM
    d_lru = lru_lambda.shape[1]
    alpha = (2.0 * depth) ** 0.25
    B, S, _ = x_prompt.shape
    DB, T, _ = x_sample.shape
    P = cache_k.shape[2]

    xp = x_prompt.reshape(B * S, d_model)
    xs = x_sample.reshape(DB * T, d_model)
    outs_p, outs_s = [], []
    for l in range(depth):
        w = w_in[l]
        o = (d_attn, 2 * d_attn, 3 * d_attn, 3 * d_attn + n_heads, 4 * d_attn + n_heads,
             4 * d_attn + n_heads + d_lru)
        wf = jnp.pad(w[:, o[2]:o[3]], ((0, 0), (0, LANES - n_heads)))
        bf = jnp.pad(b_f[l][None, :], ((0, 0), (0, LANES - n_heads)))
        wts = (w[:, :o[0]].astype(BF16), w[:, o[0]:o[1]].astype(BF16), w[:, o[1]:o[2]].astype(BF16),
               wf.astype(BF16), w[:, o[3]:o[4]].astype(BF16), w[:, o[4]:o[5]].astype(BF16),
               w[:, o[5]:].astype(BF16), bf, n_heads)
        lw = (conv_w[l], conv_b[l][None], w_r[l].astype(BF16), b_r[l][None], w_i[l].astype(BF16),
              b_i[l][None], lru_lambda[l][None], g_lru[l][None])
        gattn = g_attn[l][None]
        wa = w_out[l][:d_attn].astype(BF16)
        wl = w_out[l][d_attn:].astype(BF16)
        lng, lnb = ln_g[l][None], ln_b[l][None]

        q, k, v, kb, vb, logf, ga, xl, gl = _in_proj(xp, wts, _tile(B * S, 256))
        r3 = lambda a: a.reshape(B, S, a.shape[-1])
        cT = _cumsum(jnp.transpose(r3(logf), (0, 2, 1)))
        ya = _fox_prompt(r3(q), r3(kb), r3(vb), cT, r3(ga), gattn, _tile(S, 512))
        yl, h_T, hist_T = _rglru(r3(xl), r3(gl), jnp.zeros((B, CONV_W - 1, d_lru), F32),
                                 jnp.zeros((B, 1, d_lru), F32), lw, _tile(S, 256), True)
        outs_p.append((k.reshape(B, S, n_heads, HEAD_DIM), v.reshape(B, S, n_heads, HEAD_DIM),
                       r3(logf), h_T.reshape(B, d_lru), hist_T))
        xp = _out_proj(xp, ya.reshape(B * S, d_attn), yl.reshape(B * S, d_lru), wa, wl, lng, lnb,
                       alpha, _tile(B * S, 256))

        q, k, v, kb, vb, logf, ga, xl, gl = _in_proj(xs, wts, _tile(DB * T, 256))
        r3 = lambda a: a.reshape(DB, T, a.shape[-1])
        newT = jnp.pad(jnp.transpose(r3(logf), (0, 2, 1)), ((0, 0), (0, 0), (0, LANES - T)))
        cpT, cnT = _cumsum_carry(jnp.transpose(cache_logf[l].astype(F32), (0, 2, 1)), newT)
        cnc = jnp.transpose(cnT[:, :, :T], (0, 2, 1))
        ya = _fox_sample(r3(q), cache_k[l].reshape(DB, P * n_heads, HEAD_DIM),
                         cache_v[l].reshape(DB, P * n_heads, HEAD_DIM),
                         cpT, r3(kb), r3(vb), cnT, cnc, r3(ga), gattn, _tile(P, 1024))
        yl, h_T, hist_T = _rglru(r3(xl), r3(gl), state_conv[l], state_h[l][:, None, :], lw,
                                 _tile(T, 256), False)
        outs_s.append((k.reshape(DB, T, n_heads, HEAD_DIM), v.reshape(DB, T, n_heads, HEAD_DIM),
                       r3(logf), h_T.reshape(DB, d_lru), hist_T))
        xs = _out_proj(xs, ya.reshape(DB * T, d_attn), yl.reshape(DB * T, d_lru), wa, wl, lng, lnb,
                       alpha, _tile(DB * T, 256))

    stack = lambda outs, i: jnp.stack([o[i] for o in outs], 0)
    return (xp.reshape(B, S, d_model), xs.reshape(DB, T, d_model),
            stack(outs_p, 0), stack(outs_p, 1), stack(outs_p, 2), stack(outs_p, 3), stack(outs_p, 4),
            stack(outs_s, 0), stack(outs_s, 1), stack(outs_s, 2), stack(outs_s, 3), stack(outs_s, 4))
```

```python
import functools

import jax
import jax.numpy as jnp
from jax import lax
from jax.experimental import pallas as pl
from jax.experimental.pallas import tpu as pltpu

F32 = jnp.float32
BF16 = jnp.bfloat16

HEAD_DIM = 128
LRU_BLOCK = 128
CONV_W = 4
LRU_C = 8.0
LN_EPS = 1e-5
RMS_EPS = 1e-6
LOG2E = 1.4426950408889634
Q_SCALE = HEAD_DIM ** -0.5 * LOG2E

LANES = 128
SUBLANES = 8
V7X_VMEM_LIMIT = 56 * 2 ** 20
OUT_PROJ_ROWS = 128


def _dot(a, b):
    return jnp.dot(a, b, preferred_element_type=F32)


def _dot_nt(a, b):
    return lax.dot_general(a, b, (((1,), (1,)), ((), ())), preferred_element_type=F32)


def _softplus(y):
    return jnp.maximum(y, 0.0) + jnp.log1p(jnp.exp(-jnp.abs(y)))


def _log_sigmoid(y):
    return -_softplus(-y)


def _sigmoid(y):
    return 1.0 / (1.0 + jnp.exp(-y))


def _silu(y):
    return y * _sigmoid(y)


def _group_rms(y):
    return y * lax.rsqrt(jnp.mean(y * y, axis=-1, keepdims=True) + RMS_EPS)


def _params(*semantics):
    return pltpu.CompilerParams(dimension_semantics=semantics, vmem_limit_bytes=V7X_VMEM_LIMIT)


def _resident(shape):
    return pl.BlockSpec(shape, lambda *_: (0,) * len(shape), pipeline_mode=pl.Buffered(1))


def _in_proj_kernel(x_ref, wq_ref, wk_ref, wv_ref, wf_ref, wga_ref, wxl_ref, wgl_ref, bf_ref,
                    q_ref, k_ref, v_ref, kb_ref, vb_ref, logf_ref, ga_ref, xl_ref, gl_ref):
    n_heads = logf_ref.shape[-1]
    xb = x_ref[...].astype(BF16)
    q_ref[...] = (_dot(xb, wq_ref[...]) * Q_SCALE).astype(BF16)
    k = _dot(xb, wk_ref[...])
    k_ref[...] = k
    kb_ref[...] = k.astype(BF16)
    v = _dot(xb, wv_ref[...])
    v_ref[...] = v
    vb_ref[...] = v.astype(BF16)
    zf = _dot(xb, wf_ref[...]) + bf_ref[...]
    logf_ref[...] = _log_sigmoid(zf)[:, :n_heads]
    ga_ref[...] = _dot(xb, wga_ref[...])
    xl_ref[...] = _dot(xb, wxl_ref[...])
    gl_ref[...] = _dot(xb, wgl_ref[...])


def _in_proj(x2d, wts, tm):
    M, D = x2d.shape
    wq, wk, wv, wf, wga, wxl, wgl, bf, n_heads = wts
    d_attn, d_lru = wq.shape[1], wxl.shape[1]
    row = lambda n: pl.BlockSpec((tm, n), lambda i: (i, 0))
    out_shape = (
        jax.ShapeDtypeStruct((M, d_attn), BF16),
        jax.ShapeDtypeStruct((M, d_attn), F32),
        jax.ShapeDtypeStruct((M, d_attn), F32),
        jax.ShapeDtypeStruct((M, d_attn), BF16),
        jax.ShapeDtypeStruct((M, d_attn), BF16),
        jax.ShapeDtypeStruct((M, n_heads), F32),
        jax.ShapeDtypeStruct((M, d_attn), F32),
        jax.ShapeDtypeStruct((M, d_lru), F32),
        jax.ShapeDtypeStruct((M, d_lru), F32),
    )
    return pl.pallas_call(
        _in_proj_kernel,
        grid=(M // tm,),
        in_specs=[row(D)] + [_resident(w.shape) for w in (wq, wk, wv, wf, wga, wxl, wgl, bf)],
        out_specs=tuple(row(s.shape[1]) for s in out_shape),
        out_shape=out_shape,
        compiler_params=_params("parallel"),
        name="in_proj",
    )(x2d, wq, wk, wv, wf, wga, wxl, wgl, bf)


def _cumsum_lanes(x):
    n = x.shape[-1]
    lane = lax.broadcasted_iota(jnp.int32, x.shape, x.ndim - 1)
    step = 1
    while step < n:
        x = x + jnp.where(lane >= step, pltpu.roll(x, step, x.ndim - 1), 0.0)
        step *= 2
    return x


def _cumsum_kernel(f_ref, c_ref):
    c_ref[0] = _cumsum_lanes(f_ref[0]) * LOG2E


def _cumsum_carry_kernel(past_ref, new_ref, cpast_ref, cnew_ref):
    cp = _cumsum_lanes(past_ref[0])
    cpast_ref[0] = cp * LOG2E
    cnew_ref[0] = (_cumsum_lanes(new_ref[0]) + cp[:, cp.shape[1] - 1:]) * LOG2E


def _cumsum(fT):
    B, H, S = fT.shape
    spec = pl.BlockSpec((1, H, S), lambda b: (b, 0, 0))
    return pl.pallas_call(
        _cumsum_kernel, grid=(B,), in_specs=[spec], out_specs=spec,
        out_shape=jax.ShapeDtypeStruct(fT.shape, F32),
        compiler_params=_params("parallel"), name="cumsum",
    )(fT)


def _cumsum_carry(pastT, newT):
    B, H, P = pastT.shape
    N = newT.shape[2]
    pspec = pl.BlockSpec((1, H, P), lambda b: (b, 0, 0))
    nspec = pl.BlockSpec((1, H, N), lambda b: (b, 0, 0))
    return pl.pallas_call(
        _cumsum_carry_kernel, grid=(B,), in_specs=[pspec, nspec], out_specs=(pspec, nspec),
        out_shape=(jax.ShapeDtypeStruct(pastT.shape, F32), jax.ShapeDtypeStruct(newT.shape, F32)),
        compiler_params=_params("parallel"), name="cumsum_carry",
    )(pastT, newT)


def _head_epilogue(acc, l, gain, gate):
    return (_group_rms(acc / l) * gain * _silu(gate)).astype(BF16)


def _fox_prompt_kernel(q_ref, kb_ref, vb_ref, cT_ref, ga_ref, gattn_ref, ya_ref,
                       vp_s, m_s, acc_s, *, tq):
    n_heads = cT_ref.shape[1]
    qi = pl.program_id(1)
    q0 = pl.multiple_of(qi * tq, tq)

    @pl.when(qi == 0)
    def _():
        ones = jnp.ones((vb_ref.shape[1], HEAD_DIM), BF16)
        for h in range(n_heads):
            vp_s[h, :, :HEAD_DIM] = vb_ref[0, :, h * HEAD_DIM:(h + 1) * HEAD_DIM]
            vp_s[h, :, HEAD_DIM:] = ones

    def scores(h, k0):
        hs = slice(h * HEAD_DIM, (h + 1) * HEAD_DIM)
        s = _dot_nt(q_ref[0, :, hs], kb_ref[0, pl.ds(k0, tq), hs])
        return s - cT_ref[0, h:h + 1, pl.ds(k0, tq)]

    def row_max(s):
        m = s[:, :LANES]
        for c in range(1, s.shape[1] // LANES):
            m = jnp.maximum(m, s[:, c * LANES:(c + 1) * LANES])
        return jnp.max(m, axis=-1, keepdims=True)

    def weights(s, m):
        parts = [jnp.exp2(s[:, c * LANES:(c + 1) * LANES] - m) for c in range(s.shape[1] // LANES)]
        return jnp.concatenate(parts, axis=1).astype(BF16)

    row = lax.broadcasted_iota(jnp.int32, (tq, tq), 0)
    col = lax.broadcasted_iota(jnp.int32, (tq, tq), 1)
    s_next = scores(0, q0)
    for h in range(n_heads):
        s = jnp.where(col <= row, s_next, -jnp.inf)
        if h + 1 < n_heads:
            s_next = scores(h + 1, q0)
        m = jnp.broadcast_to(row_max(s), (tq, LANES))
        m_s[h] = m
        acc_s[h] = _dot(weights(s, m), vp_s[h, pl.ds(q0, tq), :])

    def body(j, carry):
        k0 = pl.multiple_of(j * tq, tq)
        s_next = scores(0, k0)
        for h in range(n_heads):
            s = s_next
            if h + 1 < n_heads:
                s_next = scores(h + 1, k0)
            m_old = m_s[h]
            m_new = jnp.maximum(m_old, row_max(s))
            alpha = jnp.exp2(m_old - m_new)
            m_s[h] = m_new
            pv = _dot(weights(s, m_new), vp_s[h, pl.ds(k0, tq), :])
            acc_s[h, :, :HEAD_DIM] = alpha * acc_s[h, :, :HEAD_DIM] + pv[:, :HEAD_DIM]
            acc_s[h, :, HEAD_DIM:] = alpha * acc_s[h, :, HEAD_DIM:] + pv[:, HEAD_DIM:]
        return carry

    lax.fori_loop(0, qi, body, 0)

    for h in range(n_heads):
        hs = slice(h * HEAD_DIM, (h + 1) * HEAD_DIM)
        ya_ref[0, :, hs] = _head_epilogue(acc_s[h, :, :HEAD_DIM], acc_s[h, :, HEAD_DIM:],
                                          gattn_ref[:, hs], ga_ref[0, :, hs])


def _fox_prompt(q, kb, vb, cT, ga, gattn, tq):
    B, S, DA = q.shape
    H = cT.shape[1]
    blk = pl.BlockSpec((1, tq, DA), lambda b, i: (b, i, 0))
    seq = pl.BlockSpec((1, S, DA), lambda b, i: (b, 0, 0))
    return pl.pallas_call(
        functools.partial(_fox_prompt_kernel, tq=tq),
        grid=(B, S // tq),
        in_specs=[blk, seq, seq, pl.BlockSpec((1, H, S), lambda b, i: (b, 0, 0)), blk,
                  _resident(gattn.shape)],
        out_specs=blk,
        out_shape=jax.ShapeDtypeStruct((B, S, DA), BF16),
        scratch_shapes=[pltpu.VMEM((H, S, 2 * HEAD_DIM), BF16),
                        pltpu.VMEM((H, tq, LANES), F32),
                        pltpu.VMEM((H, tq, 2 * HEAD_DIM), F32)],
        compiler_params=_params("parallel", "arbitrary"),
        name="fox_prompt",
    )(q, kb, vb, cT, ga, gattn)


def _fox_sample_kernel(q_ref, ck_ref, cv_ref, cpT_ref, kn_ref, vn_ref, cnT_ref, cnc_ref, ga_ref,
                       gattn_ref, ya_ref, m_s, l_s, acc_s):
    n_heads = cpT_ref.shape[1]
    T = q_ref.shape[1]
    tk = cpT_ref.shape[2]
    j = pl.program_id(1)

    @pl.when(j == 0)
    def _():
        m_s[...] = jnp.full(m_s.shape, -jnp.inf, F32)
        l_s[...] = jnp.zeros(l_s.shape, F32)
        acc_s[...] = jnp.zeros(acc_s.shape, F32)

    def update(h, s, vals):
        m_old = m_s[h]
        m_new = jnp.maximum(m_old, jnp.max(s, axis=-1, keepdims=True))
        alpha = jnp.exp2(m_old - m_new)
        p = jnp.exp2(s - m_new)
        m_s[h] = m_new
        l_s[h] = alpha * l_s[h] + jnp.sum(p, axis=-1, keepdims=True)
        acc_s[h] = alpha * acc_s[h] + _dot(p.astype(BF16), vals)

    def cached_scores(h):
        hs = slice(h * HEAD_DIM, (h + 1) * HEAD_DIM)
        kh = ck_ref[0, pl.ds(h, tk, stride=n_heads), :].astype(BF16)
        return _dot_nt(q_ref[0, :, hs], kh) + (cnc_ref[0, :, h:h + 1] - cpT_ref[0, h:h + 1, :])

    s_next = cached_scores(0)
    for h in range(n_heads):
        s = s_next
        if h + 1 < n_heads:
            s_next = cached_scores(h + 1)
        update(h, s, cv_ref[0, pl.ds(h, tk, stride=n_heads), :].astype(BF16))

    @pl.when(j == pl.num_programs(1) - 1)
    def _():
        row = lax.broadcasted_iota(jnp.int32, (T, T), 0)
        col = lax.broadcasted_iota(jnp.int32, (T, T), 1)
        heads = [slice(h * HEAD_DIM, (h + 1) * HEAD_DIM) for h in range(n_heads)]
        scores = [_dot_nt(q_ref[0, :, hs], kn_ref[0, :, hs])
                  + (cnc_ref[0, :, h:h + 1] - cnT_ref[0, h:h + 1, :T]) for h, hs in enumerate(heads)]
        for h, hs in enumerate(heads):
            update(h, jnp.where(col <= row, scores[h], -jnp.inf), vn_ref[0, :, hs])
        for h, hs in enumerate(heads):
            ya_ref[0, :, hs] = _head_epilogue(acc_s[h], l_s[h], gattn_ref[:, hs], ga_ref[0, :, hs])


def _fox_sample(q, ck, cv, cpT, kn, vn, cnT, cnc, ga, gattn, tk):
    B, T, DA = q.shape
    H = cpT.shape[1]
    P = ck.shape[1] // H
    new = pl.BlockSpec((1, T, DA), lambda b, j: (b, 0, 0))
    past = pl.BlockSpec((1, tk * H, HEAD_DIM), lambda b, j: (b, j, 0))
    return pl.pallas_call(
        _fox_sample_kernel,
        grid=(B, P // tk),
        in_specs=[new, past, past,
                  pl.BlockSpec((1, H, tk), lambda b, j: (b, 0, j)),
                  new, new,
                  pl.BlockSpec((1, H, cnT.shape[2]), lambda b, j: (b, 0, 0)),
                  pl.BlockSpec((1, T, H), lambda b, j: (b, 0, 0)),
                  new, _resident(gattn.shape)],
        out_specs=new,
        out_shape=jax.ShapeDtypeStruct((B, T, DA), BF16),
        scratch_shapes=[pltpu.VMEM((H, T, 1), F32), pltpu.VMEM((H, T, 1), F32),
                        pltpu.VMEM((H, T, HEAD_DIM), F32)],
        compiler_params=_params("parallel", "arbitrary"),
        name="fox_sample",
    )(q, ck, cv, cpT, kn, vn, cnT, cnc, ga, gattn)


def _segment_pitch(seg_len):
    pitch = -(-seg_len // SUBLANES) * SUBLANES
    return pitch if (pitch // SUBLANES) % 2 else pitch + SUBLANES


def _rglru_kernel(xl_ref, gl_ref, hist_ref, h0_ref, cw_ref, cb_ref, wr_ref, br_ref, wi_ref, bi_ref,
                  lam_ref, glru_ref, yl_ref, hT_ref, hist_out_ref, xn_s, hn_s, tail_s, hcar_s,
                  *, tt, reset_first):
    n_blocks = wr_ref.shape[0]
    seg = tt // SUBLANES
    pitch = _segment_pitch(seg)
    t = pl.program_id(1)

    @pl.when(t == 0)
    def _():
        tail_s[...] = jnp.zeros(tail_s.shape, F32)
        tail_s[SUBLANES - (CONV_W - 1):, :] = hist_ref[0]
        hcar_s[...] = h0_ref[0]

    sub = lax.broadcasted_iota(jnp.int32, (SUBLANES, LRU_BLOCK), 0)
    first = (lax.broadcasted_iota(jnp.int32, (tt, LRU_BLOCK), 0) == 0) & (t == 0)
    decay = -LRU_C * _softplus(-lam_ref[...])

    def shift_segments(v, head):
        return jnp.where(sub == 0, head, pltpu.roll(v, 1, 0))

    for n in range(n_blocks):
        ns = slice(n * LRU_BLOCK, (n + 1) * LRU_BLOCK)
        for s in range(SUBLANES):
            xn_s[n, s * pitch:s * pitch + seg, :] = xl_ref[0, s * seg:(s + 1) * seg, ns]
        x = [xn_s[n, pl.ds(i, SUBLANES, stride=pitch), :] for i in range(seg)]
        before = [shift_segments(x[seg - k], tail_s[SUBLANES - k:SUBLANES - k + 1, ns])
                  for k in range(CONV_W - 1, 0, -1)]
        xs = jnp.concatenate(before + x, axis=0)
        xc = cb_ref[:, ns] + xs[0:tt] * cw_ref[0:1, ns]
        for j in range(1, CONV_W):
            xc = xc + xs[j * SUBLANES:j * SUBLANES + tt] * cw_ref[j:j + 1, ns]

        xcb = xc.astype(BF16)
        r = _sigmoid(_dot(xcb, wr_ref[n]) + br_ref[:, ns])
        i_gate = _sigmoid(_dot(xcb, wi_ref[n]) + bi_ref[:, ns])
        log_a = r * decay[:, ns]
        a = jnp.exp(log_a)
        one_minus_a2 = -jnp.tanh(log_a) * (1.0 + a * a)
        mult = jnp.where(one_minus_a2 > 0.0, one_minus_a2 * lax.rsqrt(one_minus_a2), 0.0)
        if reset_first:
            mult = jnp.where(first, 1.0, mult)
        u = mult * i_gate * xc

        rows = lambda v, i: v[i * SUBLANES:(i + 1) * SUBLANES]
        h_loc, a_cum = [rows(u, 0)], [rows(a, 0)]
        for i in range(1, seg):
            h_loc.append(rows(a, i) * h_loc[-1] + rows(u, i))
            a_cum.append(rows(a, i) * a_cum[-1])
        e, p = h_loc[-1], a_cum[-1]
        for step in (1, 2, 4):
            keep = sub >= step
            e = e + p * jnp.where(keep, pltpu.roll(e, step, 0), 0.0)
            p = p * jnp.where(keep, pltpu.roll(p, step, 0), 1.0)
        h_end = e + p * hcar_s[:, ns]
        carry_in = shift_segments(h_end, hcar_s[:, ns])
        hcar_s[:, ns] = h_end[SUBLANES - 1:, :]
        hT_ref[0, :, ns] = h_end[SUBLANES - 1:, :]

        for i in range(seg):
            hn_s[n, pl.ds(i, SUBLANES, stride=pitch), :] = h_loc[i] + a_cum[i] * carry_in
        h = jnp.concatenate([hn_s[n, s * pitch:s * pitch + seg, :] for s in range(SUBLANES)], axis=0)
        y = _group_rms(h) * glru_ref[:, ns] * _silu(gl_ref[0, :, ns])
        yl_ref[0, :, ns] = y.astype(BF16)

    tail_s[...] = xl_ref[0, tt - SUBLANES:, :]
    hist_out_ref[0] = xl_ref[0, tt - (CONV_W - 1):, :]


def _rglru(xl, gl, hist, h0, lw, tt, reset_first):
    B, T, DL = xl.shape
    cw, cb, wr, br, wi, bi, lam, glru = lw
    blk = pl.BlockSpec((1, tt, DL), lambda b, t: (b, t, 0))
    per_b = lambda n: pl.BlockSpec((1, n, DL), lambda b, t: (b, 0, 0))
    slab_rows = SUBLANES * _segment_pitch(tt // SUBLANES)
    return pl.pallas_call(
        functools.partial(_rglru_kernel, tt=tt, reset_first=reset_first),
        grid=(B, T // tt),
        in_specs=[blk, blk, per_b(CONV_W - 1), per_b(1)]
                 + [_resident(w.shape) for w in (cw, cb, wr, br, wi, bi, lam, glru)],
        out_specs=(blk, per_b(1), per_b(CONV_W - 1)),
        out_shape=(jax.ShapeDtypeStruct((B, T, DL), BF16),
                   jax.ShapeDtypeStruct((B, 1, DL), F32),
                   jax.ShapeDtypeStruct((B, CONV_W - 1, DL), F32)),
        scratch_shapes=[pltpu.VMEM((wr.shape[0], slab_rows, LRU_BLOCK), F32),
                        pltpu.VMEM((wr.shape[0], slab_rows, LRU_BLOCK), F32),
                        pltpu.VMEM((SUBLANES, DL), F32), pltpu.VMEM((1, DL), F32)],
        compiler_params=_params("parallel", "arbitrary"),
        name="rglru",
    )(xl, gl, hist, h0, cw, cb, wr, br, wi, bi, lam, glru)


def _out_proj_kernel(x_ref, ya_ref, yl_ref, wa_ref, wl_ref, g_ref, b_ref, o_ref, *, alpha):
    tm = x_ref.shape[0]
    rows = [slice(r, r + OUT_PROJ_ROWS) for r in range(0, tm, OUT_PROJ_ROWS)]
    outs = [_dot(ya_ref[r, :], wa_ref[...]) + _dot(yl_ref[r, :], wl_ref[...]) for r in rows]
    for r, out in zip(rows, outs):
        h = alpha * x_ref[r, :] + out
        mu = jnp.mean(h, axis=-1, keepdims=True)
        d = h - mu
        var = jnp.mean(d * d, axis=-1, keepdims=True)
        o_ref[r, :] = d * lax.rsqrt(var + LN_EPS) * g_ref[...] + b_ref[...]


def _out_proj(x2d, ya, yl, wa, wl, g, b, alpha, tm):
    M, D = x2d.shape
    row = lambda n: pl.BlockSpec((tm, n), lambda i: (i, 0))
    return pl.pallas_call(
        functools.partial(_out_proj_kernel, alpha=alpha),
        grid=(M // tm,),
        in_specs=[row(D), row(ya.shape[1]), row(yl.shape[1])]
                 + [_resident(w.shape) for w in (wa, wl, g, b)],
        out_specs=row(D),
        out_shape=jax.ShapeDtypeStruct((M, D), F32),
        compiler_params=_params("parallel"),
        name="out_proj",
    )(x2d, ya, yl, wa, wl, g, b)


def _tile(n, pref):
    return pref if n % pref == 0 else n


def kernel(x_prompt, x_sample, cache_k, cache_v, cache_logf, state_h, state_conv, w_in, b_f, conv_w,
           conv_b, w_r, b_r, w_i, b_i, lru_lambda, g_attn, g_lru, w_out, ln_g, ln_b):
    depth, d_model, _ = w_in.shape
    n_heads = b_f.shape[1]
    d_attn = n_heads * HEAD_DIM
    d_lru = lru_lambda.shape[1]
    alpha = (2.0 * depth) ** 0.25
    B, S, _ = x_prompt.shape
    DB, T, _ = x_sample.shape
    P = cache_k.shape[2]

    xp = x_prompt.reshape(B * S, d_model)
    xs = x_sample.reshape(DB * T, d_model)
    outs_p, outs_s = [], []
    for l in range(depth):
        w = w_in[l]
        o = (d_attn, 2 * d_attn, 3 * d_attn, 3 * d_attn + n_heads, 4 * d_attn + n_heads,
             4 * d_attn + n_heads + d_lru)
        wf = jnp.pad(w[:, o[2]:o[3]], ((0, 0), (0, LANES - n_heads)))
        bf = jnp.pad(b_f[l][None, :], ((0, 0), (0, LANES - n_heads)))
        wts = (w[:, :o[0]].astype(BF16), w[:, o[0]:o[1]].astype(BF16), w[:, o[1]:o[2]].astype(BF16),
               wf.astype(BF16), w[:, o[3]:o[4]].astype(BF16), w[:, o[4]:o[5]].astype(BF16),
               w[:, o[5]:].astype(BF16), bf, n_heads)
        lw = (conv_w[l], conv_b[l][None], w_r[l].astype(BF16), b_r[l][None], w_i[l].astype(BF16),
              b_i[l][None], lru_lambda[l][None], g_lru[l][None])
        gattn = g_attn[l][None]
        wa = w_out[l][:d_attn].astype(BF16)
        wl = w_out[l][d_attn:].astype(BF16)
        lng, lnb = ln_g[l][None], ln_b[l][None]

        q, k, v, kb, vb, logf, ga, xl, gl = _in_proj(xp, wts, _tile(B * S, 256))
        r3 = lambda a: a.reshape(B, S, a.shape[-1])
        cT = _cumsum(jnp.transpose(r3(logf), (0, 2, 1)))
        ya = _fox_prompt(r3(q), r3(kb), r3(vb), cT, r3(ga), gattn, _tile(S, 512))
        yl, h_T, hist_T = _rglru(r3(xl), r3(gl), jnp.zeros((B, CONV_W - 1, d_lru), F32),
                                 jnp.zeros((B, 1, d_lru), F32), lw, _tile(S, 256), True)
        outs_p.append((k.reshape(B, S, n_heads, HEAD_DIM), v.reshape(B, S, n_heads, HEAD_DIM),
                       r3(logf), h_T.reshape(B, d_lru), hist_T))
        xp = _out_proj(xp, ya.reshape(B * S, d_attn), yl.reshape(B * S, d_lru), wa, wl, lng, lnb,
                       alpha, _tile(B * S, 512))

        q, k, v, kb, vb, logf, ga, xl, gl = _in_proj(xs, wts, _tile(DB * T, 256))
        r3 = lambda a: a.reshape(DB, T, a.shape[-1])
        newT = jnp.pad(jnp.transpose(r3(logf), (0, 2, 1)), ((0, 0), (0, 0), (0, LANES - T)))
        cpT, cnT = _cumsum_carry(jnp.transpose(cache_logf[l].astype(F32), (0, 2, 1)), newT)
        cnc = jnp.transpose(cnT[:, :, :T], (0, 2, 1))
        ya = _fox_sample(r3(q), cache_k[l].reshape(DB, P * n_heads, HEAD_DIM),
                         cache_v[l].reshape(DB, P * n_heads, HEAD_DIM),
                         cpT, r3(kb), r3(vb), cnT, cnc, r3(ga), gattn, _tile(P, 1024))
        yl, h_T, hist_T = _rglru(r3(xl), r3(gl), state_conv[l], state_h[l][:, None, :], lw,
                                 _tile(T, 256), False)
        outs_s.append((k.reshape(DB, T, n_heads, HEAD_DIM), v.reshape(DB, T, n_heads, HEAD_DIM),
                       r3(logf), h_T.reshape(DB, d_lru), hist_T))
        xs = _out_proj(xs, ya.reshape(DB * T, d_attn), yl.reshape(DB * T, d_lru), wa, wl, lng, lnb,
                       alpha, _tile(DB * T, 256))

    stack = lambda outs, i: jnp.stack([o[i] for o in outs], 0)
    return (xp.reshape(B, S, d_model), xs.reshape(DB, T, d_model),
            stack(outs_p, 0), stack(outs_p, 1), stack(outs_p, 2), stack(outs_p, 3), stack(outs_p, 4),
            stack(outs_s, 0), stack(outs_s, 1), stack(outs_s, 2), stack(outs_s, 3), stack(outs_s, 4))
```

```python
import functools

import jax
import jax.numpy as jnp
from jax import lax
from jax.experimental import pallas as pl
from jax.experimental.pallas import tpu as pltpu

F32 = jnp.float32
BF16 = jnp.bfloat16

HEAD_DIM = 128
LRU_BLOCK = 128
CONV_W = 4
LRU_C = 8.0
LN_EPS = 1e-5
RMS_EPS = 1e-6
LOG2E = 1.4426950408889634
Q_SCALE = HEAD_DIM ** -0.5 * LOG2E

LANES = 128
SUBLANES = 8
V7X_VMEM_LIMIT = 56 * 2 ** 20
OUT_PROJ_ROWS = 128


def _dot(a, b):
    return jnp.dot(a, b, preferred_element_type=F32)


def _dot_nt(a, b):
    return lax.dot_general(a, b, (((1,), (1,)), ((), ())), preferred_element_type=F32)


def _softplus(y):
    return jnp.maximum(y, 0.0) + jnp.log1p(jnp.exp(-jnp.abs(y)))


def _log_sigmoid(y):
    return -_softplus(-y)


def _sigmoid(y):
    return 1.0 / (1.0 + jnp.exp(-y))


def _silu(y):
    return y * _sigmoid(y)


def _group_rms(y):
    return y * lax.rsqrt(jnp.mean(y * y, axis=-1, keepdims=True) + RMS_EPS)


def _params(*semantics):
    return pltpu.CompilerParams(dimension_semantics=semantics, vmem_limit_bytes=V7X_VMEM_LIMIT)


def _resident(shape):
    return pl.BlockSpec(shape, lambda *_: (0,) * len(shape), pipeline_mode=pl.Buffered(1))


def _in_proj_kernel(x_ref, wq_ref, wk_ref, wv_ref, wf_ref, wga_ref, wxl_ref, wgl_ref, bf_ref,
                    q_ref, k_ref, v_ref, kb_ref, vb_ref, logf_ref, ga_ref, xl_ref, gl_ref):
    n_heads = logf_ref.shape[-1]
    xb = x_ref[...].astype(BF16)
    q_ref[...] = (_dot(xb, wq_ref[...]) * Q_SCALE).astype(BF16)
    k = _dot(xb, wk_ref[...])
    k_ref[...] = k
    kb_ref[...] = k.astype(BF16)
    v = _dot(xb, wv_ref[...])
    v_ref[...] = v
    vb_ref[...] = v.astype(BF16)
    zf = _dot(xb, wf_ref[...]) + bf_ref[...]
    logf_ref[...] = _log_sigmoid(zf)[:, :n_heads]
    ga_ref[...] = _dot(xb, wga_ref[...])
    xl_ref[...] = _dot(xb, wxl_ref[...])
    gl_ref[...] = _dot(xb, wgl_ref[...])


def _in_proj(x2d, wts, tm):
    M, D = x2d.shape
    wq, wk, wv, wf, wga, wxl, wgl, bf, n_heads = wts
    d_attn, d_lru = wq.shape[1], wxl.shape[1]
    row = lambda n: pl.BlockSpec((tm, n), lambda i: (i, 0))
    out_shape = (
        jax.ShapeDtypeStruct((M, d_attn), BF16),
        jax.ShapeDtypeStruct((M, d_attn), F32),
        jax.ShapeDtypeStruct((M, d_attn), F32),
        jax.ShapeDtypeStruct((M, d_attn), BF16),
        jax.ShapeDtypeStruct((M, d_attn), BF16),
        jax.ShapeDtypeStruct((M, n_heads), F32),
        jax.ShapeDtypeStruct((M, d_attn), F32),
        jax.ShapeDtypeStruct((M, d_lru), F32),
        jax.ShapeDtypeStruct((M, d_lru), F32),
    )
    return pl.pallas_call(
        _in_proj_kernel,
        grid=(M // tm,),
        in_specs=[row(D)] + [_resident(w.shape) for w in (wq, wk, wv, wf, wga, wxl, wgl, bf)],
        out_specs=tuple(row(s.shape[1]) for s in out_shape),
        out_shape=out_shape,
        compiler_params=_params("parallel"),
        name="in_proj",
    )(x2d, wq, wk, wv, wf, wga, wxl, wgl, bf)


def _cumsum_lanes(x):
    n = x.shape[-1]
    lane = lax.broadcasted_iota(jnp.int32, x.shape, x.ndim - 1)
    step = 1
    while step < n:
        x = x + jnp.where(lane >= step, pltpu.roll(x, step, x.ndim - 1), 0.0)
        step *= 2
    return x


def _rows(ref):
    return ref[...].reshape(ref.shape[0] * ref.shape[1], ref.shape[2])


def _cumsum_kernel(f_ref, c_ref):
    c_ref[...] = (_cumsum_lanes(_rows(f_ref)) * LOG2E).reshape(c_ref.shape)


def _cumsum_carry_kernel(past_ref, new_ref, cpast_ref, cnew_ref):
    cp = _cumsum_lanes(_rows(past_ref))
    cpast_ref[...] = (cp * LOG2E).reshape(cpast_ref.shape)
    cn = (_cumsum_lanes(_rows(new_ref)) + cp[:, cp.shape[1] - 1:]) * LOG2E
    cnew_ref[...] = cn.reshape(cnew_ref.shape)


CUMSUM_STREAMS = 4


def _cumsum(fT):
    B, H, S = fT.shape
    nb = _tile(B, CUMSUM_STREAMS)
    spec = pl.BlockSpec((nb, H, S), lambda b: (b, 0, 0))
    return pl.pallas_call(
        _cumsum_kernel, grid=(B // nb,), in_specs=[spec], out_specs=spec,
        out_shape=jax.ShapeDtypeStruct(fT.shape, F32),
        compiler_params=_params("parallel"), name="cumsum",
    )(fT)


def _cumsum_carry(pastT, newT):
    B, H, P = pastT.shape
    N = newT.shape[2]
    nb = _tile(B, CUMSUM_STREAMS)
    pspec = pl.BlockSpec((nb, H, P), lambda b: (b, 0, 0))
    nspec = pl.BlockSpec((nb, H, N), lambda b: (b, 0, 0))
    return pl.pallas_call(
        _cumsum_carry_kernel, grid=(B // nb,), in_specs=[pspec, nspec], out_specs=(pspec, nspec),
        out_shape=(jax.ShapeDtypeStruct(pastT.shape, F32), jax.ShapeDtypeStruct(newT.shape, F32)),
        compiler_params=_params("parallel"), name="cumsum_carry",
    )(pastT, newT)


def _head_epilogue(acc, l, gain, gate):
    l1 = l[:, :1]
    normed = acc * lax.rsqrt(jnp.mean(acc * acc, axis=-1, keepdims=True) + RMS_EPS * (l1 * l1))
    return (normed * gain * _silu(gate)).astype(BF16)


def _fox_prompt_kernel(q_ref, kb_ref, vb_ref, cT_ref, ga_ref, gattn_ref, ya_ref,
                       vp_s, m_s, acc_s, *, tq):
    n_heads = cT_ref.shape[1]
    qi = pl.program_id(1)
    q0 = pl.multiple_of(qi * tq, tq)

    @pl.when(qi == 0)
    def _():
        ones = jnp.ones((vb_ref.shape[1], HEAD_DIM), BF16)
        for h in range(n_heads):
            vp_s[h, :, :HEAD_DIM] = vb_ref[0, :, h * HEAD_DIM:(h + 1) * HEAD_DIM]
            vp_s[h, :, HEAD_DIM:] = ones

    def scores(h, k0):
        hs = slice(h * HEAD_DIM, (h + 1) * HEAD_DIM)
        s = _dot_nt(q_ref[0, :, hs], kb_ref[0, pl.ds(k0, tq), hs])
        return s - cT_ref[0, h:h + 1, pl.ds(k0, tq)]

    def row_max(s):
        m = s[:, :LANES]
        for c in range(1, s.shape[1] // LANES):
            m = jnp.maximum(m, s[:, c * LANES:(c + 1) * LANES])
        return jnp.max(m, axis=-1, keepdims=True)

    def weights(s, m):
        parts = [jnp.exp2(s[:, c * LANES:(c + 1) * LANES] - m) for c in range(s.shape[1] // LANES)]
        return jnp.concatenate(parts, axis=1).astype(BF16)

    half = tq // 2
    visible_a = (lax.broadcasted_iota(jnp.int32, (tq, half), 1)
                 <= lax.broadcasted_iota(jnp.int32, (tq, half), 0))
    visible_b = (lax.broadcasted_iota(jnp.int32, (half, half), 1)
                 <= lax.broadcasted_iota(jnp.int32, (half, half), 0))

    def diag_scores(h):
        hs = slice(h * HEAD_DIM, (h + 1) * HEAD_DIM)
        sa = _dot_nt(q_ref[0, :, hs], kb_ref[0, pl.ds(q0, half), hs])
        sb = _dot_nt(q_ref[0, half:, hs], kb_ref[0, pl.ds(q0 + half, half), hs])
        return (sa - cT_ref[0, h:h + 1, pl.ds(q0, half)],
                sb - cT_ref[0, h:h + 1, pl.ds(q0 + half, half)])

    s_next = diag_scores(0)
    for h in range(n_heads):
        sa = jnp.where(visible_a, s_next[0], -jnp.inf)
        sb = jnp.where(visible_b, s_next[1], -jnp.inf)
        if h + 1 < n_heads:
            s_next = diag_scores(h + 1)
        s_top = sa[:half]
        s_bot = jnp.concatenate([sa[half:], sb], axis=1)
        m_top = jnp.broadcast_to(row_max(s_top), (half, LANES))
        m_bot = jnp.broadcast_to(row_max(s_bot), (half, LANES))
        m_s[h, :half, :] = m_top
        m_s[h, half:, :] = m_bot
        acc_s[h, :half, :] = _dot(weights(s_top, m_top), vp_s[h, pl.ds(q0, half), :])
        acc_s[h, half:, :] = _dot(weights(s_bot, m_bot), vp_s[h, pl.ds(q0, tq), :])

    def body(j, carry):
        k0 = pl.multiple_of(j * tq, tq)
        s_next = scores(0, k0)
        for h in range(n_heads):
            s = s_next
            if h + 1 < n_heads:
                s_next = scores(h + 1, k0)
            m_old = m_s[h]
            m_new = jnp.maximum(m_old, row_max(s))
            alpha = jnp.exp2(m_old - m_new)
            m_s[h] = m_new
            pv = _dot(weights(s, m_new), vp_s[h, pl.ds(k0, tq), :])
            acc_s[h, :, :HEAD_DIM] = alpha * acc_s[h, :, :HEAD_DIM] + pv[:, :HEAD_DIM]
            acc_s[h, :, HEAD_DIM:] = alpha * acc_s[h, :, HEAD_DIM:] + pv[:, HEAD_DIM:]
        return carry

    lax.fori_loop(0, qi, body, 0)

    for h in range(n_heads):
        hs = slice(h * HEAD_DIM, (h + 1) * HEAD_DIM)
        ya_ref[0, :, hs] = _head_epilogue(acc_s[h, :, :HEAD_DIM], acc_s[h, :, HEAD_DIM:],
                                          gattn_ref[:, hs], ga_ref[0, :, hs])


def _fox_prompt(q, kb, vb, cT, ga, gattn, tq):
    B, S, DA = q.shape
    H = cT.shape[1]
    blk = pl.BlockSpec((1, tq, DA), lambda b, i: (b, i, 0))
    seq = pl.BlockSpec((1, S, DA), lambda b, i: (b, 0, 0))
    return pl.pallas_call(
        functools.partial(_fox_prompt_kernel, tq=tq),
        grid=(B, S // tq),
        in_specs=[blk, seq, seq, pl.BlockSpec((1, H, S), lambda b, i: (b, 0, 0)), blk,
                  _resident(gattn.shape)],
        out_specs=blk,
        out_shape=jax.ShapeDtypeStruct((B, S, DA), BF16),
        scratch_shapes=[pltpu.VMEM((H, S, 2 * HEAD_DIM), BF16),
                        pltpu.VMEM((H, tq, LANES), F32),
                        pltpu.VMEM((H, tq, 2 * HEAD_DIM), F32)],
        compiler_params=_params("parallel", "arbitrary"),
        name="fox_prompt",
    )(q, kb, vb, cT, ga, gattn)


def _fox_sample_kernel(q_ref, ck_ref, cv_ref, cpT_ref, kn_ref, vn_ref, cnT_ref, cnc_ref, ga_ref,
                       gattn_ref, ya_ref, m_s, l_s, acc_s):
    n_heads = cpT_ref.shape[1]
    T = q_ref.shape[1]
    tk = cpT_ref.shape[2]
    j = pl.program_id(1)

    @pl.when(j == 0)
    def _():
        m_s[...] = jnp.full(m_s.shape, -jnp.inf, F32)
        l_s[...] = jnp.zeros(l_s.shape, F32)
        acc_s[...] = jnp.zeros(acc_s.shape, F32)

    def update(h, s, vals):
        m_old = m_s[h]
        m_new = jnp.maximum(m_old, jnp.max(s, axis=-1, keepdims=True))
        alpha = jnp.exp2(m_old - m_new)
        p = jnp.exp2(s - m_new)
        m_s[h] = m_new
        l_s[h] = alpha * l_s[h] + jnp.sum(p, axis=-1, keepdims=True)
        acc_s[h] = alpha * acc_s[h] + _dot(p.astype(BF16), vals)

    def cached_scores(h):
        hs = slice(h * HEAD_DIM, (h + 1) * HEAD_DIM)
        kh = ck_ref[0, pl.ds(h, tk, stride=n_heads), :].astype(BF16)
        return _dot_nt(q_ref[0, :, hs], kh) + (cnc_ref[0, :, h:h + 1] - cpT_ref[0, h:h + 1, :])

    s_next = cached_scores(0)
    for h in range(n_heads):
        s = s_next
        if h + 1 < n_heads:
            s_next = cached_scores(h + 1)
        update(h, s, cv_ref[0, pl.ds(h, tk, stride=n_heads), :].astype(BF16))

    @pl.when(j == pl.num_programs(1) - 1)
    def _():
        row = lax.broadcasted_iota(jnp.int32, (T, T), 0)
        col = lax.broadcasted_iota(jnp.int32, (T, T), 1)
        heads = [slice(h * HEAD_DIM, (h + 1) * HEAD_DIM) for h in range(n_heads)]
        scores = [_dot_nt(q_ref[0, :, hs], kn_ref[0, :, hs])
                  + (cnc_ref[0, :, h:h + 1] - cnT_ref[0, h:h + 1, :T]) for h, hs in enumerate(heads)]
        for h, hs in enumerate(heads):
            update(h, jnp.where(col <= row, scores[h], -jnp.inf), vn_ref[0, :, hs])
        for h, hs in enumerate(heads):
            ya_ref[0, :, hs] = _head_epilogue(acc_s[h], l_s[h], gattn_ref[:, hs], ga_ref[0, :, hs])


def _fox_sample(q, ck, cv, cpT, kn, vn, cnT, cnc, ga, gattn, tk):
    B, T, DA = q.shape
    H = cpT.shape[1]
    P = ck.shape[1] // H
    new = pl.BlockSpec((1, T, DA), lambda b, j: (b, 0, 0))
    past = pl.BlockSpec((1, tk * H, HEAD_DIM), lambda b, j: (b, j, 0))
    return pl.pallas_call(
        _fox_sample_kernel,
        grid=(B, P // tk),
        in_specs=[new, past, past,
                  pl.BlockSpec((1, H, tk), lambda b, j: (b, 0, j)),
                  new, new,
                  pl.BlockSpec((1, H, cnT.shape[2]), lambda b, j: (b, 0, 0)),
                  pl.BlockSpec((1, T, H), lambda b, j: (b, 0, 0)),
                  new, _resident(gattn.shape)],
        out_specs=new,
        out_shape=jax.ShapeDtypeStruct((B, T, DA), BF16),
        scratch_shapes=[pltpu.VMEM((H, T, 1), F32), pltpu.VMEM((H, T, 1), F32),
                        pltpu.VMEM((H, T, HEAD_DIM), F32)],
        compiler_params=_params("parallel", "arbitrary"),
        name="fox_sample",
    )(q, ck, cv, cpT, kn, vn, cnT, cnc, ga, gattn)


def _segment_pitch(seg_len):
    pitch = -(-seg_len // SUBLANES) * SUBLANES
    return pitch if (pitch // SUBLANES) % 2 else pitch + SUBLANES


def _rglru_init(t, hist_ref, h0_ref, tail_s, hcar_s):
    @pl.when(t == 0)
    def _():
        tail_s[...] = jnp.zeros(tail_s.shape, F32)
        tail_s[SUBLANES - (CONV_W - 1):, :] = hist_ref[...]
        hcar_s[...] = h0_ref[...]


def _rglru_block(t, xl_ref, gl_ref, cw_ref, cb_ref, wr_ref, br_ref, wi_ref, bi_ref,
                 lam_ref, glru_ref, yl_ref, hT_ref, hist_out_ref, xn_s, hn_s, hl_s, ac_s, tail_s, hcar_s,
                 *, tt, reset_first, run_second=lambda second: second()):
    n_blocks = wr_ref.shape[0]
    seg = tt // SUBLANES
    pitch = _segment_pitch(seg)
    sub =lax.broadcasted_iota(jnp.int32, (SUBLANES, LRU_BLOCK), 0)
    first = (lax.broadcasted_iota(jnp.int32, (tt, LRU_BLOCK), 0) == 0) & (t == 0)
    decay = -LRU_C * _softplus(-lam_ref[...])

    def shift_segments(v, head):
        return jnp.where(sub == 0, head, pltpu.roll(v, 1, 0))

    gated = []
    for n in range(n_blocks):
        ns = slice(n * LRU_BLOCK, (n + 1) * LRU_BLOCK)
        for s in range(SUBLANES):
            xn_s[n, s * pitch:s * pitch + seg, :] = xl_ref[s * seg:(s + 1) * seg, ns]
        x = [xn_s[n, pl.ds(i, SUBLANES, stride=pitch), :] for i in range(seg)]
        before = [shift_segments(x[seg - k], tail_s[SUBLANES - k:SUBLANES - k + 1, ns])
                  for k in range(CONV_W - 1, 0, -1)]
        xs = jnp.concatenate(before + x, axis=0)
        xc = cb_ref[:, ns] + xs[0:tt] * cw_ref[0:1, ns]
        for j in range(1, CONV_W):
            xc = xc + xs[j * SUBLANES:j * SUBLANES + tt] * cw_ref[j:j + 1, ns]

        xcb = xc.astype(BF16)
        gated.append((xc, _dot(xcb, wr_ref[n]), _dot(xcb, wi_ref[n])))

    tail_s[...] = xl_ref[tt - SUBLANES:, :]
    hist_out_ref[...] = xl_ref[tt - (CONV_W - 1):, :]
    run_second(functools.partial(_rglru_recurrence, gated, sub, first, decay, shift_segments, br_ref,
                                 bi_ref, glru_ref, gl_ref, yl_ref, hT_ref, hn_s, hl_s, ac_s, hcar_s,
                                 seg=seg, pitch=pitch, reset_first=reset_first))


def _rglru_recurrence(gated, sub, first, decay, shift_segments, br_ref, bi_ref, glru_ref, gl_ref,
                      yl_ref, hT_ref, hn_s, hl_s, ac_s, hcar_s, *, seg, pitch, reset_first):
    for n, (xc, zr, zi) in enumerate(gated):
        ns = slice(n * LRU_BLOCK, (n + 1) * LRU_BLOCK)
        r = _sigmoid(zr + br_ref[:, ns])
        i_gate = _sigmoid(zi + bi_ref[:, ns])
        log_a = r * decay[:, ns]
        a = jnp.exp(log_a)
        one_minus_a2 = -jnp.tanh(log_a) * (1.0 + a * a)
        mult = jnp.where(one_minus_a2 > 0.0, one_minus_a2 * lax.rsqrt(one_minus_a2), 0.0)
        if reset_first:
            mult = jnp.where(first, 1.0, mult)
        u = mult * i_gate * xc

        rows = lambda v, i: v[i * SUBLANES:(i + 1) * SUBLANES]
        e, p = rows(u, 0), rows(a, 0)
        hl_s[n, 0:SUBLANES, :] = e
        ac_s[n, 0:SUBLANES, :] = p
        for i in range(1, seg):
            e = rows(a, i) * e + rows(u, i)
            p = rows(a, i) * p
            hl_s[n, i * SUBLANES:(i + 1) * SUBLANES, :] = e
            ac_s[n, i * SUBLANES:(i + 1) * SUBLANES, :] = p
        for step in (1, 2, 4):
            keep = sub >= step
            e = e + p * jnp.where(keep, pltpu.roll(e, step, 0), 0.0)
            p = p * jnp.where(keep, pltpu.roll(p, step, 0), 1.0)
        h_end = e + p * hcar_s[:, ns]
        carry_in = shift_segments(h_end, hcar_s[:, ns])
        hcar_s[:, ns] = h_end[SUBLANES - 1:, :]
        hT_ref[:, ns] = h_end[SUBLANES - 1:, :]

        for i in range(seg):
            hn_s[n, pl.ds(i, SUBLANES, stride=pitch), :] = (
                hl_s[n, i * SUBLANES:(i + 1) * SUBLANES, :]
                + ac_s[n, i * SUBLANES:(i + 1) * SUBLANES, :] * carry_in)
        h = jnp.concatenate([hn_s[n, s * pitch:s * pitch + seg, :] for s in range(SUBLANES)], axis=0)
        y = _group_rms(h) * glru_ref[:, ns] * _silu(gl_ref[:, ns])
        yl_ref[:, ns] = y.astype(BF16)


def _rglru_kernel(xl_ref, gl_ref, hist_ref, h0_ref, *rest, tt, reset_first):
    weights, (yl_ref, hT_ref, hist_out_ref), scratch = rest[:8], rest[8:11], rest[11:]
    t = pl.program_id(1)
    _rglru_init(t, hist_ref.at[0], h0_ref.at[0], *scratch[4:])
    _rglru_block(t, xl_ref.at[0], gl_ref.at[0], *weights, yl_ref.at[0], hT_ref.at[0],
                 hist_out_ref.at[0], *scratch, tt=tt, reset_first=reset_first)


def _in_proj_rglru_kernel(x_ref, wq_ref, wk_ref, wv_ref, wf_ref, wga_ref, wxl_ref, wgl_ref, bf_ref,
                          hist_ref, h0_ref, *rest, tt, reset_first):
    lru_w = rest[:8]
    q_ref, k_ref, v_ref, kb_ref, vb_ref, logf_ref, ga_ref, yl_ref, hT_ref, hist_out_ref = rest[8:18]
    xl_s, gl_s, *lru_scratch = rest[18:]
    n_heads = logf_ref.shape[-1]
    t = pl.program_id(1)
    _rglru_init(t, hist_ref.at[0], h0_ref.at[0], *lru_scratch[4:])
    xb = x_ref[0].astype(BF16)
    xl_s[...] = _dot(xb, wxl_ref[...])
    gl_s[...] = _dot(xb, wgl_ref[...])

    def second_region(recurrence):
        recurrence()
        q_ref[0] = (_dot(xb, wq_ref[...]) * Q_SCALE).astype(BF16)
        k = _dot(xb, wk_ref[...])
        k_ref[0] = k
        kb_ref[0] = k.astype(BF16)
        v = _dot(xb, wv_ref[...])
        v_ref[0] = v
        vb_ref[0] = v.astype(BF16)
        zf = _dot(xb, wf_ref[...]) + bf_ref[...]
        logf_ref[0] = _log_sigmoid(zf)[:, :n_heads]
        ga_ref[0] = _dot(xb, wga_ref[...])

    _rglru_block(t, xl_s, gl_s, *lru_w, yl_ref.at[0], hT_ref.at[0], hist_out_ref.at[0],
                 *lru_scratch, tt=tt, reset_first=reset_first, run_second=second_region)


def _in_proj_rglru(x, wts, hist, h0, lw, tt, reset_first):
    B, S, D = x.shape
    wq, wk, wv, wf, wga, wxl, wgl, bf, n_heads = wts
    d_attn, d_lru = wq.shape[1], wxl.shape[1]
    blk = lambda n: pl.BlockSpec((1, tt, n), lambda b, t: (b, t, 0))
    per_b = lambda n: pl.BlockSpec((1, n, d_lru), lambda b, t: (b, 0, 0))
    slab_rows = SUBLANES * _segment_pitch(tt // SUBLANES)
    seq = lambda n, dt: jax.ShapeDtypeStruct((B, S, n), dt)
    out_shape = (seq(d_attn, BF16), seq(d_attn, F32), seq(d_attn, F32), seq(d_attn, BF16),
                 seq(d_attn, BF16), seq(n_heads, F32), seq(d_attn, F32), seq(d_lru, BF16),
                 jax.ShapeDtypeStruct((B, 1, d_lru), F32),
                 jax.ShapeDtypeStruct((B, CONV_W - 1, d_lru), F32))
    return pl.pallas_call(
        functools.partial(_in_proj_rglru_kernel, tt=tt, reset_first=reset_first),
        grid=(B, S // tt),
        in_specs=[blk(D)] + [_resident(w.shape) for w in (wq, wk, wv, wf, wga, wxl, wgl, bf)]
                 + [per_b(CONV_W - 1), per_b(1)] + [_resident(w.shape) for w in lw],
        out_specs=tuple(blk(s.shape[2]) for s in out_shape[:8]) + (per_b(1), per_b(CONV_W - 1)),
        out_shape=out_shape,
        scratch_shapes=[pltpu.VMEM((tt, d_lru), F32), pltpu.VMEM((tt, d_lru), F32),
                        pltpu.VMEM((lw[2].shape[0], slab_rows, LRU_BLOCK), F32),
                        pltpu.VMEM((lw[2].shape[0], slab_rows, LRU_BLOCK), F32),
                        pltpu.VMEM((lw[2].shape[0], tt, LRU_BLOCK), F32),
                        pltpu.VMEM((lw[2].shape[0], tt, LRU_BLOCK), F32),
                        pltpu.VMEM((SUBLANES, d_lru), F32), pltpu.VMEM((1, d_lru), F32)],
        compiler_params=_params("parallel", "arbitrary"),
        name="in_proj_rglru",
    )(x, wq, wk, wv, wf, wga, wxl, wgl, bf, hist, h0, *lw)


def _rglru(xl, gl, hist, h0, lw, tt, reset_first):
    B, T, DL = xl.shape
    cw, cb, wr, br, wi, bi, lam, glru = lw
    blk = pl.BlockSpec((1, tt, DL), lambda b, t: (b, t, 0))
    per_b = lambda n: pl.BlockSpec((1, n, DL), lambda b, t: (b, 0, 0))
    slab_rows = SUBLANES * _segment_pitch(tt // SUBLANES)
    return pl.pallas_call(
        functools.partial(_rglru_kernel, tt=tt, reset_first=reset_first),
        grid=(B, T // tt),
        in_specs=[blk, blk, per_b(CONV_W - 1), per_b(1)]
                 + [_resident(w.shape) for w in (cw, cb, wr, br, wi, bi, lam, glru)],
        out_specs=(blk, per_b(1), per_b(CONV_W - 1)),
        out_shape=(jax.ShapeDtypeStruct((B, T, DL), BF16),
                   jax.ShapeDtypeStruct((B, 1, DL), F32),
                   jax.ShapeDtypeStruct((B, CONV_W - 1, DL), F32)),
        scratch_shapes=[pltpu.VMEM((wr.shape[0], slab_rows, LRU_BLOCK), F32),
                        pltpu.VMEM((wr.shape[0], slab_rows, LRU_BLOCK), F32),
                        pltpu.VMEM((wr.shape[0], tt, LRU_BLOCK), F32),
                        pltpu.VMEM((wr.shape[0], tt, LRU_BLOCK), F32),
                        pltpu.VMEM((SUBLANES, DL), F32), pltpu.VMEM((1, DL), F32)],
        compiler_params=_params("parallel", "arbitrary"),
        name="rglru",
    )(xl, gl, hist, h0, cw, cb, wr, br, wi, bi, lam, glru)


def _out_proj_kernel(x_ref, ya_ref, yl_ref, wa_ref, wl_ref, g_ref, b_ref, o_ref, *, alpha):
    tm = x_ref.shape[0]
    rows = [slice(r, r + OUT_PROJ_ROWS) for r in range(0, tm, OUT_PROJ_ROWS)]
    outs = [_dot(ya_ref[r, :], wa_ref[...]) + _dot(yl_ref[r, :], wl_ref[...]) for r in rows]
    for r, out in zip(rows, outs):
        h = alpha * x_ref[r, :] + out
        mu = jnp.mean(h, axis=-1, keepdims=True)
        d = h - mu
        var = jnp.mean(d * d, axis=-1, keepdims=True)
        o_ref[r, :] = d * lax.rsqrt(var + LN_EPS) * g_ref[...] + b_ref[...]


def _out_proj(x2d, ya, yl, wa, wl, g, b, alpha, tm):
    M, D = x2d.shape
    row = lambda n: pl.BlockSpec((tm, n), lambda i: (i, 0))
    return pl.pallas_call(
        functools.partial(_out_proj_kernel, alpha=alpha),
        grid=(M // tm,),
        in_specs=[row(D), row(ya.shape[1]), row(yl.shape[1])]
                 + [_resident(w.shape) for w in (wa, wl, g, b)],
        out_specs=row(D),
        out_shape=jax.ShapeDtypeStruct((M, D), F32),
        compiler_params=_params("parallel"),
        name="out_proj",
    )(x2d, ya, yl, wa, wl, g, b)


def _tile(n, pref):
    return pref if n % pref == 0 else n


def kernel(x_prompt, x_sample, cache_k, cache_v, cache_logf, state_h, state_conv, w_in, b_f, conv_w,
           conv_b, w_r, b_r, w_i, b_i, lru_lambda, g_attn, g_lru, w_out, ln_g, ln_b):
    depth, d_model, _ = w_in.shape
    n_heads = b_f.shape[1]
    d_attn = n_heads * HEAD_DIM
    d_lru = lru_lambda.shape[1]
    alpha = (2.0 * depth) ** 0.25
    B, S, _ = x_prompt.shape
    DB, T, _ = x_sample.shape
    P = cache_k.shape[2]

    xp = x_prompt.reshape(B * S, d_model)
    xs = x_sample.reshape(DB * T, d_model)
    outs_p, outs_s = [], []
    for l in range(depth):
        w = w_in[l]
        o = (d_attn, 2 * d_attn, 3 * d_attn, 3 * d_attn + n_heads, 4 * d_attn + n_heads,
             4 * d_attn + n_heads + d_lru)
        wf = jnp.pad(w[:, o[2]:o[3]], ((0, 0), (0, LANES - n_heads)))
        bf = jnp.pad(b_f[l][None, :], ((0, 0), (0, LANES - n_heads)))
        wts = (w[:, :o[0]].astype(BF16), w[:, o[0]:o[1]].astype(BF16), w[:, o[1]:o[2]].astype(BF16),
               wf.astype(BF16), w[:, o[3]:o[4]].astype(BF16), w[:, o[4]:o[5]].astype(BF16),
               w[:, o[5]:].astype(BF16), bf, n_heads)
        lw = (conv_w[l], conv_b[l][None], w_r[l].astype(BF16), b_r[l][None], w_i[l].astype(BF16),
              b_i[l][None], lru_lambda[l][None], g_lru[l][None])
        gattn = g_attn[l][None]
        wa = w_out[l][:d_attn].astype(BF16)
        wl = w_out[l][d_attn:].astype(BF16)
        lng, lnb = ln_g[l][None], ln_b[l][None]

        q, k, v, kb, vb, logf, ga, yl, h_T, hist_T = _in_proj_rglru(
            xp.reshape(B, S, d_model), wts, jnp.zeros((B, CONV_W - 1, d_lru), F32),
            jnp.zeros((B, 1, d_lru), F32), lw, _tile(S, 256), True)
        cT = _cumsum(jnp.transpose(logf, (0, 2, 1)))
        ya = _fox_prompt(q, kb, vb, cT, ga, gattn, _tile(S, 512))
        outs_p.append((k.reshape(B, S, n_heads, HEAD_DIM), v.reshape(B, S, n_heads, HEAD_DIM),
                       logf, h_T.reshape(B, d_lru), hist_T))
        xp = _out_proj(xp, ya.reshape(B * S, d_attn), yl.reshape(B * S, d_lru), wa, wl, lng, lnb,
                       alpha, _tile(B * S, 512))

        q, k, v, kb, vb, logf, ga, xl, gl = _in_proj(xs, wts, _tile(DB * T, 256))
        r3 = lambda a: a.reshape(DB, T, a.shape[-1])
        newT = jnp.pad(jnp.transpose(r3(logf), (0, 2, 1)), ((0, 0), (0, 0), (0, LANES - T)))
        cpT, cnT = _cumsum_carry(jnp.transpose(cache_logf[l].astype(F32), (0, 2, 1)), newT)
        cnc = jnp.transpose(cnT[:, :, :T], (0, 2, 1))
        ya = _fox_sample(r3(q), cache_k[l].reshape(DB, P * n_heads, HEAD_DIM),
                         cache_v[l].reshape(DB, P * n_heads, HEAD_DIM),
                         cpT, r3(kb), r3(vb), cnT, cnc, r3(ga), gattn, _tile(P, 1024))
        yl, h_T, hist_T = _rglru(r3(xl), r3(gl), state_conv[l], state_h[l][:, None, :], lw,
                                 _tile(T, 256), False)
        outs_s.append((k.reshape(DB, T, n_heads, HEAD_DIM), v.reshape(DB, T, n_heads, HEAD_DIM),
                       r3(logf), h_T.reshape(DB, d_lru), hist_T))
        xs = _out_proj(xs, ya.reshape(DB * T, d_attn), yl.reshape(DB * T, d_lru), wa, wl, lng, lnb,
                       alpha, _tile(DB * T, 256))

    stack = lambda outs, i: jnp.stack([o[i] for o in outs], 0)
    return (xp.reshape(B, S, d_model), xs.reshape(DB, T, d_model),
            stack(outs_p, 0), stack(outs_p, 1), stack(outs_p, 2), stack(outs_p, 3), stack(outs_p, 4),
            stack(outs_s, 0), stack(outs_s, 1), stack(outs_s, 2), stack(outs_s, 3), stack(outs_s, 4))
```

```python
import functools

import jax
import jax.numpy as jnp
from jax import lax
from jax.experimental import pallas as pl
from jax.experimental.pallas import tpu as pltpu

F32 = jnp.float32
BF16 = jnp.bfloat16

HEAD_DIM = 128
LRU_BLOCK = 128
CONV_W = 4
LRU_C = 8.0
LN_EPS = 1e-5
RMS_EPS = 1e-6
LOG2E = 1.4426950408889634
Q_SCALE = HEAD_DIM ** -0.5 * LOG2E

LANES = 128
SUBLANES = 8
V7X_VMEM_LIMIT = 56 * 2 ** 20
OUT_PROJ_ROWS = 128


def _dot(a, b):
    return jnp.dot(a, b, preferred_element_type=F32)


def _dot_nt(a, b):
    return lax.dot_general(a, b, (((1,), (1,)), ((), ())), preferred_element_type=F32)


def _softplus(y):
    return jnp.maximum(y, 0.0) + jnp.log1p(jnp.exp(-jnp.abs(y)))


def _log_sigmoid(y):
    return -_softplus(-y)


def _sigmoid(y):
    return 1.0 / (1.0 + jnp.exp2(y * -LOG2E))


def _silu(y):
    return y * _sigmoid(y)


def _group_rms(y):
    return y * lax.rsqrt(jnp.mean(y * y, axis=-1, keepdims=True) + RMS_EPS)


def _params(*semantics):
    return pltpu.CompilerParams(dimension_semantics=semantics, vmem_limit_bytes=V7X_VMEM_LIMIT)


def _resident(shape):
    return pl.BlockSpec(shape, lambda *_: (0,) * len(shape), pipeline_mode=pl.Buffered(1))


def _in_proj_kernel(x_ref, wq_ref, wk_ref, wv_ref, wf_ref, wga_ref, wxl_ref, wgl_ref, bf_ref,
                    q_ref, k_ref, v_ref, kb_ref, vb_ref, logf_ref, ga_ref, xl_ref, gl_ref):
    n_heads = logf_ref.shape[-1]
    xb = x_ref[...].astype(BF16)
    q_ref[...] = (_dot(xb, wq_ref[...]) * Q_SCALE).astype(BF16)
    k = _dot(xb, wk_ref[...])
    k_ref[...] = k
    kb_ref[...] = k.astype(BF16)
    v = _dot(xb, wv_ref[...])
    v_ref[...] = v
    vb_ref[...] = v.astype(BF16)
    zf = _dot(xb, wf_ref[...]) + bf_ref[...]
    logf_ref[...] = _log_sigmoid(zf)[:, :n_heads]
    ga_ref[...] = _dot(xb, wga_ref[...])
    xl_ref[...] = _dot(xb, wxl_ref[...])
    gl_ref[...] = _dot(xb, wgl_ref[...])


def _in_proj(x2d, wts, tm):
    M, D = x2d.shape
    wq, wk, wv, wf, wga, wxl, wgl, bf, n_heads = wts
    d_attn, d_lru = wq.shape[1], wxl.shape[1]
    row = lambda n: pl.BlockSpec((tm, n), lambda i: (i, 0))
    out_shape = (
        jax.ShapeDtypeStruct((M, d_attn), BF16),
        jax.ShapeDtypeStruct((M, d_attn), F32),
        jax.ShapeDtypeStruct((M, d_attn), F32),
        jax.ShapeDtypeStruct((M, d_attn), BF16),
        jax.ShapeDtypeStruct((M, d_attn), BF16),
        jax.ShapeDtypeStruct((M, n_heads), F32),
        jax.ShapeDtypeStruct((M, d_attn), F32),
        jax.ShapeDtypeStruct((M, d_lru), F32),
        jax.ShapeDtypeStruct((M, d_lru), F32),
    )
    return pl.pallas_call(
        _in_proj_kernel,
        grid=(M // tm,),
        in_specs=[row(D)] + [_resident(w.shape) for w in (wq, wk, wv, wf, wga, wxl, wgl, bf)],
        out_specs=tuple(row(s.shape[1]) for s in out_shape),
        out_shape=out_shape,
        compiler_params=_params("parallel"),
        name="in_proj",
    )(x2d, wq, wk, wv, wf, wga, wxl, wgl, bf)


def _cumsum_lanes(x):
    n = x.shape[-1]
    lane = lax.broadcasted_iota(jnp.int32, x.shape, x.ndim - 1)
    step = 1
    while step < n:
        x = x + jnp.where(lane >= step, pltpu.roll(x, step, x.ndim - 1), 0.0)
        step *= 2
    return x


def _rows(ref):
    return ref[...].reshape(ref.shape[0] * ref.shape[1], ref.shape[2])


def _cumsum_kernel(f_ref, c_ref):
    c_ref[...] = (_cumsum_lanes(_rows(f_ref)) * LOG2E).reshape(c_ref.shape)


def _cumsum_carry_kernel(past_ref, new_ref, cpast_ref, cnew_ref):
    cp = _cumsum_lanes(_rows(past_ref))
    cpast_ref[...] = (cp * LOG2E).reshape(cpast_ref.shape)
    cn = (_cumsum_lanes(_rows(new_ref)) + cp[:, cp.shape[1] - 1:]) * LOG2E
    cnew_ref[...] = cn.reshape(cnew_ref.shape)


CUMSUM_STREAMS = 4


def _cumsum(fT):
    B, H, S = fT.shape
    nb = _tile(B, CUMSUM_STREAMS)
    spec = pl.BlockSpec((nb, H, S), lambda b: (b, 0, 0))
    return pl.pallas_call(
        _cumsum_kernel, grid=(B // nb,), in_specs=[spec], out_specs=spec,
        out_shape=jax.ShapeDtypeStruct(fT.shape, F32),
        compiler_params=_params("parallel"), name="cumsum",
    )(fT)


def _cumsum_carry(pastT, newT):
    B, H, P = pastT.shape
    N = newT.shape[2]
    nb = _tile(B, CUMSUM_STREAMS)
    pspec = pl.BlockSpec((nb, H, P), lambda b: (b, 0, 0))
    nspec = pl.BlockSpec((nb, H, N), lambda b: (b, 0, 0))
    return pl.pallas_call(
        _cumsum_carry_kernel, grid=(B // nb,), in_specs=[pspec, nspec], out_specs=(pspec, nspec),
        out_shape=(jax.ShapeDtypeStruct(pastT.shape, F32), jax.ShapeDtypeStruct(newT.shape, F32)),
        compiler_params=_params("parallel"), name="cumsum_carry",
    )(pastT, newT)


def _head_epilogue(acc, l, gain, gate):
    normed = acc * lax.rsqrt(jnp.mean(acc * acc, axis=-1, keepdims=True) + RMS_EPS * (l * l))
    return (normed * gain * _silu(gate)).astype(BF16)


def _fox_prompt_kernel(q_ref, kb_ref, vb_ref, cT_ref, ga_ref, gattn_ref, ya_ref,
                       vp_s, m_s, acc_s, *, tq):
    n_heads = cT_ref.shape[1]
    qi = pl.program_id(1)
    q0 = pl.multiple_of(qi * tq, tq)

    @pl.when(qi == 0)
    def _():
        ones = jnp.ones((vb_ref.shape[1], HEAD_DIM), BF16)
        for h in range(n_heads):
            vp_s[h, :, :HEAD_DIM] = vb_ref[0, :, h * HEAD_DIM:(h + 1) * HEAD_DIM]
            vp_s[h, :, HEAD_DIM:] = ones

    def scores(h, k0):
        hs = slice(h * HEAD_DIM, (h + 1) * HEAD_DIM)
        s = _dot_nt(q_ref[0, :, hs], kb_ref[0, pl.ds(k0, tq), hs])
        return s - cT_ref[0, h:h + 1, pl.ds(k0, tq)]

    def row_max(s):
        m = s[:, :LANES]
        for c in range(1, s.shape[1] // LANES):
            m = jnp.maximum(m, s[:, c * LANES:(c + 1) * LANES])
        return jnp.max(m, axis=-1, keepdims=True)

    def weights(s, m):
        parts = [jnp.exp2(s[:, c * LANES:(c + 1) * LANES] - m) for c in range(s.shape[1] // LANES)]
        return jnp.concatenate(parts, axis=1).astype(BF16)

    half = tq // 2
    visible_a = (lax.broadcasted_iota(jnp.int32, (tq, half), 1)
                 <= lax.broadcasted_iota(jnp.int32, (tq, half), 0))
    visible_b = (lax.broadcasted_iota(jnp.int32, (half, half), 1)
                 <= lax.broadcasted_iota(jnp.int32, (half, half), 0))

    def diag_scores(h):
        hs = slice(h * HEAD_DIM, (h + 1) * HEAD_DIM)
        sa = _dot_nt(q_ref[0, :, hs], kb_ref[0, pl.ds(q0, half), hs])
        sb = _dot_nt(q_ref[0, half:, hs], kb_ref[0, pl.ds(q0 + half, half), hs])
        return (sa - cT_ref[0, h:h + 1, pl.ds(q0, half)],
                sb - cT_ref[0, h:h + 1, pl.ds(q0 + half, half)])

    s_next = diag_scores(0)
    for h in range(n_heads):
        sa = jnp.where(visible_a, s_next[0], -jnp.inf)
        sb = jnp.where(visible_b, s_next[1], -jnp.inf)
        if h + 1 < n_heads:
            s_next = diag_scores(h + 1)
        s_top = sa[:half]
        s_bot = jnp.concatenate([sa[half:], sb], axis=1)
        m_top = jnp.broadcast_to(row_max(s_top), (half, LANES))
        m_bot = jnp.broadcast_to(row_max(s_bot), (half, LANES))
        m_s[h, :half, :] = m_top
        m_s[h, half:, :] = m_bot
        acc_s[h, :half, :] = _dot(weights(s_top, m_top), vp_s[h, pl.ds(q0, half), :])
        acc_s[h, half:, :] = _dot(weights(s_bot, m_bot), vp_s[h, pl.ds(q0, tq), :])

    def body(j, carry):
        k0 = pl.multiple_of(j * tq, tq)
        s_next = scores(0, k0)
        for h in range(n_heads):
            s = s_next
            if h + 1 < n_heads:
                s_next = scores(h + 1, k0)
            m_old = m_s[h]
            m_new = jnp.maximum(m_old, row_max(s))
            alpha = jnp.exp2(m_old - m_new)
            m_s[h] = m_new
            pv = _dot(weights(s, m_new), vp_s[h, pl.ds(k0, tq), :])
            acc_s[h, :, :HEAD_DIM] = alpha * acc_s[h, :, :HEAD_DIM] + pv[:, :HEAD_DIM]
            acc_s[h, :, HEAD_DIM:] = alpha * acc_s[h, :, HEAD_DIM:] + pv[:, HEAD_DIM:]
        return carry

    lax.fori_loop(0, qi, body, 0)

    for h in range(n_heads):
        hs = slice(h * HEAD_DIM, (h + 1) * HEAD_DIM)
        ya_ref[0, :, hs] = _head_epilogue(acc_s[h, :, :HEAD_DIM], acc_s[h, :, HEAD_DIM:],
                                          gattn_ref[:, hs], ga_ref[0, :, hs])


def _fox_prompt(q, kb, vb, cT, ga, gattn, tq):
    B, S, DA = q.shape
    H = cT.shape[1]
    blk = pl.BlockSpec((1, tq, DA), lambda b, i: (b, i, 0))
    seq = pl.BlockSpec((1, S, DA), lambda b, i: (b, 0, 0))
    return pl.pallas_call(
        functools.partial(_fox_prompt_kernel, tq=tq),
        grid=(B, S // tq),
        in_specs=[blk, seq, seq, pl.BlockSpec((1, H, S), lambda b, i: (b, 0, 0)), blk,
                  _resident(gattn.shape)],
        out_specs=blk,
        out_shape=jax.ShapeDtypeStruct((B, S, DA), BF16),
        scratch_shapes=[pltpu.VMEM((H, S, 2 * HEAD_DIM), BF16),
                        pltpu.VMEM((H, tq, LANES), F32),
                        pltpu.VMEM((H, tq, 2 * HEAD_DIM), F32)],
        compiler_params=_params("parallel", "arbitrary"),
        name="fox_prompt",
    )(q, kb, vb, cT, ga, gattn)


def _fox_sample_kernel(q_ref, ck_ref, cv_ref, cpT_ref, kn_ref, vn_ref, cnT_ref, cnc_ref, ga_ref,
                       gattn_ref, ya_ref, m_s, l_s, acc_s):
    n_heads = cpT_ref.shape[1]
    T = q_ref.shape[1]
    tk = cpT_ref.shape[2]
    j = pl.program_id(1)

    @pl.when(j == 0)
    def _():
        m_s[...] = jnp.full(m_s.shape, -jnp.inf, F32)
        l_s[...] = jnp.zeros(l_s.shape, F32)
        acc_s[...] = jnp.zeros(acc_s.shape, F32)

    def update(h, s, vals):
        m_old = m_s[h]
        m_new = jnp.maximum(m_old, jnp.max(s, axis=-1, keepdims=True))
        alpha = jnp.exp2(m_old - m_new)
        p = jnp.exp2(s - m_new)
        m_s[h] = m_new
        l_s[h] = alpha * l_s[h] + jnp.sum(p, axis=-1, keepdims=True)
        acc_s[h] = alpha * acc_s[h] + _dot(p.astype(BF16), vals)

    def cached_scores(h):
        hs = slice(h * HEAD_DIM, (h + 1) * HEAD_DIM)
        kh = ck_ref[0, pl.ds(h, tk, stride=n_heads), :].astype(BF16)
        return _dot_nt(q_ref[0, :, hs], kh) + (cnc_ref[0, :, h:h + 1] - cpT_ref[0, h:h + 1, :])

    s_next = cached_scores(0)
    for h in range(n_heads):
        s = s_next
        if h + 1 < n_heads:
            s_next = cached_scores(h + 1)
        update(h, s, cv_ref[0, pl.ds(h, tk, stride=n_heads), :].astype(BF16))

    @pl.when(j == pl.num_programs(1) - 1)
    def _():
        row = lax.broadcasted_iota(jnp.int32, (T, T), 0)
        col = lax.broadcasted_iota(jnp.int32, (T, T), 1)
        heads = [slice(h * HEAD_DIM, (h + 1) * HEAD_DIM) for h in range(n_heads)]
        scores = [_dot_nt(q_ref[0, :, hs], kn_ref[0, :, hs])
                  + (cnc_ref[0, :, h:h + 1] - cnT_ref[0, h:h + 1, :T]) for h, hs in enumerate(heads)]
        for h, hs in enumerate(heads):
            update(h, jnp.where(col <= row, scores[h], -jnp.inf), vn_ref[0, :, hs])
        for h, hs in enumerate(heads):
            ya_ref[0, :, hs] = _head_epilogue(acc_s[h], l_s[h], gattn_ref[:, hs], ga_ref[0, :, hs])


def _fox_sample(q, ck, cv, cpT, kn, vn, cnT, cnc, ga, gattn, tk):
    B, T, DA = q.shape
    H = cpT.shape[1]
    P = ck.shape[1] // H
    new = pl.BlockSpec((1, T, DA), lambda b, j: (b, 0, 0))
    past = pl.BlockSpec((1, tk * H, HEAD_DIM), lambda b, j: (b, j, 0))
    return pl.pallas_call(
        _fox_sample_kernel,
        grid=(B, P // tk),
        in_specs=[new, past, past,
                  pl.BlockSpec((1, H, tk), lambda b, j: (b, 0, j)),
                  new, new,
                  pl.BlockSpec((1, H, cnT.shape[2]), lambda b, j: (b, 0, 0)),
                  pl.BlockSpec((1, T, H), lambda b, j: (b, 0, 0)),
                  new, _resident(gattn.shape)],
        out_specs=new,
        out_shape=jax.ShapeDtypeStruct((B, T, DA), BF16),
        scratch_shapes=[pltpu.VMEM((H, T, 1), F32), pltpu.VMEM((H, T, 1), F32),
                        pltpu.VMEM((H, T, HEAD_DIM), F32)],
        compiler_params=_params("parallel", "arbitrary"),
        name="fox_sample",
    )(q, ck, cv, cpT, kn, vn, cnT, cnc, ga, gattn)


def _segment_pitch(seg_len):
    pitch = -(-seg_len // SUBLANES) * SUBLANES
    return pitch if (pitch // SUBLANES) % 2 else pitch + SUBLANES


def _rglru_init(t, hist_ref, h0_ref, tail_s, hcar_s):
    @pl.when(t == 0)
    def _():
        tail_s[...] = jnp.zeros(tail_s.shape, F32)
        tail_s[SUBLANES - (CONV_W - 1):, :] = hist_ref[...]
        hcar_s[...] = h0_ref[...]


def _rglru_block(t, xl_ref, gl_ref, cw_ref, cb_ref, wr_ref, br_ref, wi_ref, bi_ref,
                 lam_ref, glru_ref, yl_ref, hT_ref, hist_out_ref, xn_s, hn_s, hl_s, ac_s, tail_s, hcar_s,
                 *, tt, reset_first, run_second=lambda second: second()):
    n_blocks = wr_ref.shape[0]
    seg = tt // SUBLANES
    pitch = _segment_pitch(seg)
    sub =lax.broadcasted_iota(jnp.int32, (SUBLANES, LRU_BLOCK), 0)
    first = (lax.broadcasted_iota(jnp.int32, (tt, LRU_BLOCK), 0) == 0) & (t == 0)
    decay = -LRU_C * _softplus(-lam_ref[...])

    def shift_segments(v, head):
        return jnp.where(sub == 0, head, pltpu.roll(v, 1, 0))

    gated = []
    for n in range(n_blocks):
        ns = slice(n * LRU_BLOCK, (n + 1) * LRU_BLOCK)
        for s in range(SUBLANES):
            xn_s[n, s * pitch:s * pitch + seg, :] = xl_ref[s * seg:(s + 1) * seg, ns]
        x = [xn_s[n, pl.ds(i, SUBLANES, stride=pitch), :] for i in range(seg)]
        before = [shift_segments(x[seg - k], tail_s[SUBLANES - k:SUBLANES - k + 1, ns])
                  for k in range(CONV_W - 1, 0, -1)]
        xs = jnp.concatenate(before + x, axis=0)
        xc = cb_ref[:, ns] + xs[0:tt] * cw_ref[0:1, ns]
        for j in range(1, CONV_W):
            xc = xc + xs[j * SUBLANES:j * SUBLANES + tt] * cw_ref[j:j + 1, ns]

        xcb = xc.astype(BF16)
        gated.append((xc, _dot(xcb, wr_ref[n]), _dot(xcb, wi_ref[n])))

    tail_s[...] = xl_ref[tt - SUBLANES:, :]
    hist_out_ref[...] = xl_ref[tt - (CONV_W - 1):, :]
    run_second(functools.partial(_rglru_recurrence, gated, sub, first, decay, shift_segments, br_ref,
                                 bi_ref, glru_ref, gl_ref, yl_ref, hT_ref, hn_s, hl_s, ac_s, hcar_s,
                                 seg=seg, pitch=pitch, reset_first=reset_first))


def _rglru_recurrence(gated, sub, first, decay, shift_segments, br_ref, bi_ref, glru_ref, gl_ref,
                      yl_ref, hT_ref, hn_s, hl_s, ac_s, hcar_s, *, seg, pitch, reset_first):
    for n, (xc, zr, zi) in enumerate(gated):
        ns = slice(n * LRU_BLOCK, (n + 1) * LRU_BLOCK)
        r = _sigmoid(zr + br_ref[:, ns])
        i_gate = _sigmoid(zi + bi_ref[:, ns])
        log_a = r * decay[:, ns]
        a = jnp.exp(log_a)
        one_minus_a2 = -jnp.tanh(log_a) * (1.0 + a * a)
        mult = jnp.where(one_minus_a2 > 0.0, one_minus_a2 * lax.rsqrt(one_minus_a2), 0.0)
        if reset_first:
            mult = jnp.where(first, 1.0, mult)
        u = mult * i_gate * xc

        rows = lambda v, i: v[i * SUBLANES:(i + 1) * SUBLANES]
        e, p = rows(u, 0), rows(a, 0)
        hl_s[n, 0:SUBLANES, :] = e
        ac_s[n, 0:SUBLANES, :] = p
        for i in range(1, seg):
            e = rows(a, i) * e + rows(u, i)
            p = rows(a, i) * p
            hl_s[n, i * SUBLANES:(i + 1) * SUBLANES, :] = e
            ac_s[n, i * SUBLANES:(i + 1) * SUBLANES, :] = p
        for step in (1, 2, 4):
            keep = sub >= step
            e = e + p * jnp.where(keep, pltpu.roll(e, step, 0), 0.0)
            p = p * jnp.where(keep, pltpu.roll(p, step, 0), 1.0)
        h_end = e + p * hcar_s[:, ns]
        carry_in = shift_segments(h_end, hcar_s[:, ns])
        hcar_s[:, ns] = h_end[SUBLANES - 1:, :]
        hT_ref[:, ns] = h_end[SUBLANES - 1:, :]

        for i in range(seg):
            hn_s[n, pl.ds(i, SUBLANES, stride=pitch), :] = (
                hl_s[n, i * SUBLANES:(i + 1) * SUBLANES, :]
                + ac_s[n, i * SUBLANES:(i + 1) * SUBLANES, :] * carry_in)
        h = jnp.concatenate([hn_s[n, s * pitch:s * pitch + seg, :] for s in range(SUBLANES)], axis=0)
        y = _group_rms(h) * glru_ref[:, ns] * _silu(gl_ref[:, ns])
        yl_ref[:, ns] = y.astype(BF16)


def _rglru_kernel(xl_ref, gl_ref, hist_ref, h0_ref, *rest, tt, reset_first):
    weights, (yl_ref, hT_ref, hist_out_ref), scratch = rest[:8], rest[8:11], rest[11:]
    t = pl.program_id(1)
    _rglru_init(t, hist_ref.at[0], h0_ref.at[0], *scratch[4:])
    _rglru_block(t, xl_ref.at[0], gl_ref.at[0], *weights, yl_ref.at[0], hT_ref.at[0],
                 hist_out_ref.at[0], *scratch, tt=tt, reset_first=reset_first)


def _in_proj_rglru_kernel(x_ref, wq_ref, wk_ref, wv_ref, wf_ref, wga_ref, wxl_ref, wgl_ref, bf_ref,
                          hist_ref, h0_ref, *rest, tt, reset_first):
    lru_w = rest[:8]
    (q_ref, k_ref, v_ref, kb_ref, vb_ref, logf_ref, ga_ref, yl_ref, hT_ref, hist_out_ref,
     logf_t_ref) = rest[8:19]
    xl_s, gl_s, *lru_scratch = rest[19:]
    n_heads = logf_ref.shape[-1]
    t = pl.program_id(1)
    _rglru_init(t, hist_ref.at[0], h0_ref.at[0], *lru_scratch[4:])
    xb = x_ref[0].astype(BF16)
    xl_s[...] = _dot(xb, wxl_ref[...])
    gl_s[...] = _dot(xb, wgl_ref[...])

    def second_region(recurrence):
        recurrence()
        q_ref[0] = (_dot(xb, wq_ref[...]) * Q_SCALE).astype(BF16)
        k = _dot(xb, wk_ref[...])
        k_ref[0] = k
        kb_ref[0] = k.astype(BF16)
        v = _dot(xb, wv_ref[...])
        v_ref[0] = v
        vb_ref[0] = v.astype(BF16)
        logf = _log_sigmoid(_dot(xb, wf_ref[...]) + bf_ref[...])
        logf_ref[0] = logf[:, :n_heads]
        logf_t_ref[0] = logf.T[:n_heads, :]
        ga_ref[0] = _dot(xb, wga_ref[...])

    _rglru_block(t, xl_s, gl_s, *lru_w, yl_ref.at[0], hT_ref.at[0], hist_out_ref.at[0],
                 *lru_scratch, tt=tt, reset_first=reset_first, run_second=second_region)


def _in_proj_rglru(x, wts, hist, h0, lw, tt, reset_first):
    B, S, D = x.shape
    wq, wk, wv, wf, wga, wxl, wgl, bf, n_heads = wts
    d_attn, d_lru = wq.shape[1], wxl.shape[1]
    blk = lambda n: pl.BlockSpec((1, tt, n), lambda b, t: (b, t, 0))
    per_b = lambda n: pl.BlockSpec((1, n, d_lru), lambda b, t: (b, 0, 0))
    slab_rows = SUBLANES * _segment_pitch(tt // SUBLANES)
    seq = lambda n, dt: jax.ShapeDtypeStruct((B, S, n), dt)
    out_shape = (seq(d_attn, BF16), seq(d_attn, F32), seq(d_attn, F32), seq(d_attn, BF16),
                 seq(d_attn, BF16), seq(n_heads, F32), seq(d_attn, F32), seq(d_lru, BF16),
                 jax.ShapeDtypeStruct((B, 1, d_lru), F32),
                 jax.ShapeDtypeStruct((B, CONV_W - 1, d_lru), F32),
                 jax.ShapeDtypeStruct((B, n_heads, S), F32))
    return pl.pallas_call(
        functools.partial(_in_proj_rglru_kernel, tt=tt, reset_first=reset_first),
        grid=(B, S // tt),
        in_specs=[blk(D)] + [_resident(w.shape) for w in (wq, wk, wv, wf, wga, wxl, wgl, bf)]
                 + [per_b(CONV_W - 1), per_b(1)] + [_resident(w.shape) for w in lw],
        out_specs=tuple(blk(s.shape[2]) for s in out_shape[:8]) + (per_b(1), per_b(CONV_W - 1))
                  + (pl.BlockSpec((1, n_heads, tt), lambda b, t: (b, 0, t)),),
        out_shape=out_shape,
        scratch_shapes=[pltpu.VMEM((tt, d_lru), F32), pltpu.VMEM((tt, d_lru), F32),
                        pltpu.VMEM((lw[2].shape[0], slab_rows, LRU_BLOCK), F32),
                        pltpu.VMEM((lw[2].shape[0], slab_rows, LRU_BLOCK), F32),
                        pltpu.VMEM((lw[2].shape[0], tt, LRU_BLOCK), F32),
                        pltpu.VMEM((lw[2].shape[0], tt, LRU_BLOCK), F32),
                        pltpu.VMEM((SUBLANES, d_lru), F32), pltpu.VMEM((1, d_lru), F32)],
        compiler_params=_params("parallel", "arbitrary"),
        name="in_proj_rglru",
    )(x, wq, wk, wv, wf, wga, wxl, wgl, bf, hist, h0, *lw)


def _rglru(xl, gl, hist, h0, lw, tt, reset_first):
    B, T, DL = xl.shape
    cw, cb, wr, br, wi, bi, lam, glru = lw
    blk = pl.BlockSpec((1, tt, DL), lambda b, t: (b, t, 0))
    per_b = lambda n: pl.BlockSpec((1, n, DL), lambda b, t: (b, 0, 0))
    slab_rows = SUBLANES * _segment_pitch(tt // SUBLANES)
    return pl.pallas_call(
        functools.partial(_rglru_kernel, tt=tt, reset_first=reset_first),
        grid=(B, T // tt),
        in_specs=[blk, blk, per_b(CONV_W - 1), per_b(1)]
                 + [_resident(w.shape) for w in (cw, cb, wr, br, wi, bi, lam, glru)],
        out_specs=(blk, per_b(1), per_b(CONV_W - 1)),
        out_shape=(jax.ShapeDtypeStruct((B, T, DL), BF16),
                   jax.ShapeDtypeStruct((B, 1, DL), F32),
                   jax.ShapeDtypeStruct((B, CONV_W - 1, DL), F32)),
        scratch_shapes=[pltpu.VMEM((wr.shape[0], slab_rows, LRU_BLOCK), F32),
                        pltpu.VMEM((wr.shape[0], slab_rows, LRU_BLOCK), F32),
                        pltpu.VMEM((wr.shape[0], tt, LRU_BLOCK), F32),
                        pltpu.VMEM((wr.shape[0], tt, LRU_BLOCK), F32),
                        pltpu.VMEM((SUBLANES, DL), F32), pltpu.VMEM((1, DL), F32)],
        compiler_params=_params("parallel", "arbitrary"),
        name="rglru",
    )(xl, gl, hist, h0, cw, cb, wr, br, wi, bi, lam, glru)


def _out_proj_kernel(x_ref, ya_ref, yl_ref, wa_ref, wl_ref, g_ref, b_ref, o_ref, *, alpha):
    tm = x_ref.shape[0]
    rows = [slice(r, r + OUT_PROJ_ROWS) for r in range(0, tm, OUT_PROJ_ROWS)]
    outs = [_dot(ya_ref[r, :], wa_ref[...]) + _dot(yl_ref[r, :], wl_ref[...]) for r in rows]
    for r, out in zip(rows, outs):
        h = alpha * x_ref[r, :] + out
        mu = jnp.mean(h, axis=-1, keepdims=True)
        d = h - mu
        var = jnp.mean(d * d, axis=-1, keepdims=True)
        o_ref[r, :] = d * lax.rsqrt(var + LN_EPS) * g_ref[...] + b_ref[...]


def _out_proj(x2d, ya, yl, wa, wl, g, b, alpha, tm):
    M, D = x2d.shape
    row = lambda n: pl.BlockSpec((tm, n), lambda i: (i, 0))
    return pl.pallas_call(
        functools.partial(_out_proj_kernel, alpha=alpha),
        grid=(M // tm,),
        in_specs=[row(D), row(ya.shape[1]), row(yl.shape[1])]
                 + [_resident(w.shape) for w in (wa, wl, g, b)],
        out_specs=row(D),
        out_shape=jax.ShapeDtypeStruct((M, D), F32),
        compiler_params=_params("parallel"),
        name="out_proj",
    )(x2d, ya, yl, wa, wl, g, b)


def _tile(n, pref):
    return pref if n % pref == 0 else n


def kernel(x_prompt, x_sample, cache_k, cache_v, cache_logf, state_h, state_conv, w_in, b_f, conv_w,
           conv_b, w_r, b_r, w_i, b_i, lru_lambda, g_attn, g_lru, w_out, ln_g, ln_b):
    depth, d_model, _ = w_in.shape
    n_heads = b_f.shape[1]
    d_attn = n_heads * HEAD_DIM
    d_lru = lru_lambda.shape[1]
    alpha = (2.0 * depth) ** 0.25
    B, S, _ = x_prompt.shape
    DB, T, _ = x_sample.shape
    P = cache_k.shape[2]

    xp = x_prompt.reshape(B * S, d_model)
    xs = x_sample.reshape(DB * T, d_model)
    outs_p, outs_s = [], []
    for l in range(depth):
        w = w_in[l]
        o = (d_attn, 2 * d_attn, 3 * d_attn, 3 * d_attn + n_heads, 4 * d_attn + n_heads,
             4 * d_attn + n_heads + d_lru)
        wf = jnp.pad(w[:, o[2]:o[3]], ((0, 0), (0, LANES - n_heads)))
        bf = jnp.pad(b_f[l][None, :], ((0, 0), (0, LANES - n_heads)))
        wts = (w[:, :o[0]].astype(BF16), w[:, o[0]:o[1]].astype(BF16), w[:, o[1]:o[2]].astype(BF16),
               wf.astype(BF16), w[:, o[3]:o[4]].astype(BF16), w[:, o[4]:o[5]].astype(BF16),
               w[:, o[5]:].astype(BF16), bf, n_heads)
        lw = (conv_w[l], conv_b[l][None], w_r[l].astype(BF16), b_r[l][None], w_i[l].astype(BF16),
              b_i[l][None], lru_lambda[l][None], g_lru[l][None])
        gattn = g_attn[l][None]
        wa = w_out[l][:d_attn].astype(BF16)
        wl = w_out[l][d_attn:].astype(BF16)
        lng, lnb = ln_g[l][None], ln_b[l][None]

        q, k, v, kb, vb, logf, ga, yl, h_T, hist_T, logf_t = _in_proj_rglru(
            xp.reshape(B, S, d_model), wts, jnp.zeros((B, CONV_W - 1, d_lru), F32),
            jnp.zeros((B, 1, d_lru), F32), lw, _tile(S, 256), True)
        cT = _cumsum(logf_t)
        ya = _fox_prompt(q, kb, vb, cT, ga, gattn, _tile(S, 512))
        outs_p.append((k.reshape(B, S, n_heads, HEAD_DIM), v.reshape(B, S, n_heads, HEAD_DIM),
                       logf, h_T.reshape(B, d_lru), hist_T))
        xp = _out_proj(xp, ya.reshape(B * S, d_attn), yl.reshape(B * S, d_lru), wa, wl, lng, lnb,
                       alpha, _tile(B * S, 512))

        q, k, v, kb, vb, logf, ga, xl, gl = _in_proj(xs, wts, _tile(DB * T, 256))
        r3 = lambda a: a.reshape(DB, T, a.shape[-1])
        newT = jnp.pad(jnp.transpose(r3(logf), (0, 2, 1)), ((0, 0), (0, 0), (0, LANES - T)))
        cpT, cnT = _cumsum_carry(jnp.transpose(cache_logf[l].astype(F32), (0, 2, 1)), newT)
        cnc = jnp.transpose(cnT[:, :, :T], (0, 2, 1))
        ya = _fox_sample(r3(q), cache_k[l].reshape(DB, P * n_heads, HEAD_DIM),
                         cache_v[l].reshape(DB, P * n_heads, HEAD_DIM),
                         cpT, r3(kb), r3(vb), cnT, cnc, r3(ga), gattn, _tile(P, 2048))
        yl, h_T, hist_T = _rglru(r3(xl), r3(gl), state_conv[l], state_h[l][:, None, :], lw,
                                 _tile(T, 256), False)
        outs_s.append((k.reshape(DB, T, n_heads, HEAD_DIM), v.reshape(DB, T, n_heads, HEAD_DIM),
                       r3(logf), h_T.reshape(DB, d_lru), hist_T))
        xs = _out_proj(xs, ya.reshape(DB * T, d_attn), yl.reshape(DB * T, d_lru), wa, wl, lng, lnb,
                       alpha, _tile(DB * T, 256))

    stack = lambda outs, i: jnp.stack([o[i] for o in outs], 0)
    return (xp.reshape(B, S, d_model), xs.reshape(DB, T, d_model),
            stack(outs_p, 0), stack(outs_p, 1), stack(outs_p, 2), stack(outs_p, 3), stack(outs_p, 4),
            stack(outs_s, 0), stack(outs_s, 1), stack(outs_s, 2), stack(outs_s, 3), stack(outs_s, 4))
```

```python
import functools

import jax
import jax.numpy as jnp
from jax import lax
from jax.experimental import pallas as pl
from jax.experimental.pallas import tpu as pltpu

F32 = jnp.float32
BF16 = jnp.bfloat16

HEAD_DIM = 128
LRU_BLOCK = 128
CONV_W = 4
LRU_C = 8.0
LN_EPS = 1e-5
RMS_EPS = 1e-6
LOG2E = 1.4426950408889634
Q_SCALE = HEAD_DIM ** -0.5 * LOG2E

LANES = 128
SUBLANES = 8
V7X_VMEM_LIMIT = 56 * 2 ** 20
OUT_PROJ_ROWS = 128


def _dot(a, b):
    return jnp.dot(a, b, preferred_element_type=F32)


def _dot_nt(a, b):
    return lax.dot_general(a, b, (((1,), (1,)), ((), ())), preferred_element_type=F32)


def _softplus(y):
    return jnp.maximum(y, 0.0) + jnp.log1p(jnp.exp(-jnp.abs(y)))


def _log_sigmoid(y):
    return -_softplus(-y)


def _sigmoid(y):
    return 1.0 / (1.0 + jnp.exp2(y * -LOG2E))


def _silu(y):
    return y * _sigmoid(y)


def _group_rms(y):
    return y * lax.rsqrt(jnp.mean(y * y, axis=-1, keepdims=True) + RMS_EPS)


def _params(*semantics):
    return pltpu.CompilerParams(dimension_semantics=semantics, vmem_limit_bytes=V7X_VMEM_LIMIT)


def _resident(shape):
    return pl.BlockSpec(shape, lambda *_: (0,) * len(shape), pipeline_mode=pl.Buffered(1))


def _in_proj_kernel(x_ref, wq_ref, wk_ref, wv_ref, wf_ref, wga_ref, wxl_ref, wgl_ref, bf_ref,
                    q_ref, k_ref, v_ref, kb_ref, vb_ref, logf_ref, ga_ref, xl_ref, gl_ref):
    n_heads = logf_ref.shape[-1]
    xb = x_ref[...].astype(BF16)
    q_ref[...] = (_dot(xb, wq_ref[...]) * Q_SCALE).astype(BF16)
    k = _dot(xb, wk_ref[...])
    k_ref[...] = k
    kb_ref[...] = k.astype(BF16)
    v = _dot(xb, wv_ref[...])
    v_ref[...] = v
    vb_ref[...] = v.astype(BF16)
    zf = _dot(xb, wf_ref[...]) + bf_ref[...]
    logf_ref[...] = _log_sigmoid(zf)[:, :n_heads]
    ga_ref[...] = _silu(_dot(xb, wga_ref[...]))
    xl_ref[...] = _dot(xb, wxl_ref[...])
    gl_ref[...] = _silu(_dot(xb, wgl_ref[...]))


def _in_proj(x2d, wts, tm):
    M, D = x2d.shape
    wq, wk, wv, wf, wga, wxl, wgl, bf, n_heads = wts
    d_attn, d_lru = wq.shape[1], wxl.shape[1]
    row = lambda n: pl.BlockSpec((tm, n), lambda i: (i, 0))
    out_shape = (
        jax.ShapeDtypeStruct((M, d_attn), BF16),
        jax.ShapeDtypeStruct((M, d_attn), F32),
        jax.ShapeDtypeStruct((M, d_attn), F32),
        jax.ShapeDtypeStruct((M, d_attn), BF16),
        jax.ShapeDtypeStruct((M, d_attn), BF16),
        jax.ShapeDtypeStruct((M, n_heads), F32),
        jax.ShapeDtypeStruct((M, d_attn), F32),
        jax.ShapeDtypeStruct((M, d_lru), F32),
        jax.ShapeDtypeStruct((M, d_lru), F32),
    )
    return pl.pallas_call(
        _in_proj_kernel,
        grid=(M // tm,),
        in_specs=[row(D)] + [_resident(w.shape) for w in (wq, wk, wv, wf, wga, wxl, wgl, bf)],
        out_specs=tuple(row(s.shape[1]) for s in out_shape),
        out_shape=out_shape,
        compiler_params=_params("parallel"),
        name="in_proj",
    )(x2d, wq, wk, wv, wf, wga, wxl, wgl, bf)


def _cumsum_lanes(x):
    n = x.shape[-1]
    lane = lax.broadcasted_iota(jnp.int32, x.shape, x.ndim - 1)
    step = 1
    while step < n:
        x = x + jnp.where(lane >= step, pltpu.roll(x, step, x.ndim - 1), 0.0)
        step *= 2
    return x


def _rows(ref):
    return ref[...].reshape(ref.shape[0] * ref.shape[1], ref.shape[2])


def _cumsum_kernel(f_ref, c_ref):
    c_ref[...] = (_cumsum_lanes(_rows(f_ref)) * LOG2E).reshape(c_ref.shape)


def _cumsum_carry_kernel(past_ref, new_ref, cpast_ref, cnew_ref):
    cp = _cumsum_lanes(_rows(past_ref))
    cpast_ref[...] = (cp * LOG2E).reshape(cpast_ref.shape)
    cn = (_cumsum_lanes(_rows(new_ref)) + cp[:, cp.shape[1] - 1:]) * LOG2E
    cnew_ref[...] = cn.reshape(cnew_ref.shape)


CUMSUM_STREAMS = 4


def _cumsum(fT):
    B, H, S = fT.shape
    nb = _tile(B, CUMSUM_STREAMS)
    spec = pl.BlockSpec((nb, H, S), lambda b: (b, 0, 0))
    return pl.pallas_call(
        _cumsum_kernel, grid=(B // nb,), in_specs=[spec], out_specs=spec,
        out_shape=jax.ShapeDtypeStruct(fT.shape, F32),
        compiler_params=_params("parallel"), name="cumsum",
    )(fT)


def _cumsum_carry(pastT, newT):
    B, H, P = pastT.shape
    N = newT.shape[2]
    nb = _tile(B, CUMSUM_STREAMS)
    pspec = pl.BlockSpec((nb, H, P), lambda b: (b, 0, 0))
    nspec = pl.BlockSpec((nb, H, N), lambda b: (b, 0, 0))
    return pl.pallas_call(
        _cumsum_carry_kernel, grid=(B // nb,), in_specs=[pspec, nspec], out_specs=(pspec, nspec),
        out_shape=(jax.ShapeDtypeStruct(pastT.shape, F32), jax.ShapeDtypeStruct(newT.shape, F32)),
        compiler_params=_params("parallel"), name="cumsum_carry",
    )(pastT, newT)


def _head_epilogue(acc, l, gain, gate):
    normed = acc * lax.rsqrt(jnp.mean(acc * acc, axis=-1, keepdims=True) + RMS_EPS * (l * l))
    return (normed * gain * gate).astype(BF16)


def _fox_prompt_kernel(q_ref, kb_ref, vp_ref, cT_ref, ga_ref, gattn_ref, ya_ref,
                       m_s, acc_s, *, tq):
    n_heads = cT_ref.shape[1]
    qi = pl.program_id(1)
    q0 = pl.multiple_of(qi * tq, tq)

    def values(h, k0, rows):
        return vp_ref[0, pl.ds(k0, rows), 2 * h * HEAD_DIM:(2 * h + 2) * HEAD_DIM]

    def scores(h, k0):
        hs = slice(h * HEAD_DIM, (h + 1) * HEAD_DIM)
        s = _dot_nt(q_ref[0, :, hs], kb_ref[0, pl.ds(k0, tq), hs])
        return s - cT_ref[0, h:h + 1, pl.ds(k0, tq)]

    def row_max(s):
        m = s[:, :LANES]
        for c in range(1, s.shape[1] // LANES):
            m = jnp.maximum(m, s[:, c * LANES:(c + 1) * LANES])
        return jnp.max(m, axis=-1, keepdims=True)

    def weights(s, m):
        parts = [jnp.exp2(s[:, c * LANES:(c + 1) * LANES] - m) for c in range(s.shape[1] // LANES)]
        return jnp.concatenate(parts, axis=1).astype(BF16)

    half = tq // 2
    visible_a = (lax.broadcasted_iota(jnp.int32, (tq, half), 1)
                 <= lax.broadcasted_iota(jnp.int32, (tq, half), 0))
    visible_b = (lax.broadcasted_iota(jnp.int32, (half, half), 1)
                 <= lax.broadcasted_iota(jnp.int32, (half, half), 0))

    def diag_scores(h):
        hs = slice(h * HEAD_DIM, (h + 1) * HEAD_DIM)
        sa = _dot_nt(q_ref[0, :, hs], kb_ref[0, pl.ds(q0, half), hs])
        sb = _dot_nt(q_ref[0, half:, hs], kb_ref[0, pl.ds(q0 + half, half), hs])
        return (sa - cT_ref[0, h:h + 1, pl.ds(q0, half)],
                sb - cT_ref[0, h:h + 1, pl.ds(q0 + half, half)])

    s_next = diag_scores(0)
    for h in range(n_heads):
        sa = jnp.where(visible_a, s_next[0], -jnp.inf)
        sb = jnp.where(visible_b, s_next[1], -jnp.inf)
        if h + 1 < n_heads:
            s_next = diag_scores(h + 1)
        s_top = sa[:half]
        s_bot = jnp.concatenate([sa[half:], sb], axis=1)
        m_top = jnp.broadcast_to(row_max(s_top), (half, LANES))
        m_bot = jnp.broadcast_to(row_max(s_bot), (half, LANES))
        m_s[h, :half, :] = m_top
        m_s[h, half:, :] = m_bot
        acc_s[h, :half, :] = _dot(weights(s_top, m_top), values(h, q0, half))
        acc_s[h, half:, :] = _dot(weights(s_bot, m_bot), values(h, q0, tq))

    def body(j, carry):
        k0 = pl.multiple_of(j * tq, tq)
        s_next = scores(0, k0)
        for h in range(n_heads):
            s = s_next
            if h + 1 < n_heads:
                s_next = scores(h + 1, k0)
            m_old = m_s[h]
            m_new = jnp.maximum(m_old, row_max(s))
            alpha = jnp.exp2(m_old - m_new)
            m_s[h] = m_new
            pv = _dot(weights(s, m_new), values(h, k0, tq))
            acc_s[h, :, :HEAD_DIM] = alpha * acc_s[h, :, :HEAD_DIM] + pv[:, :HEAD_DIM]
            acc_s[h, :, HEAD_DIM:] = alpha * acc_s[h, :, HEAD_DIM:] + pv[:, HEAD_DIM:]
        return carry

    lax.fori_loop(0, qi, body, 0)

    for h in range(n_heads):
        hs = slice(h * HEAD_DIM, (h + 1) * HEAD_DIM)
        ya_ref[0, :, hs] = _head_epilogue(acc_s[h, :, :HEAD_DIM], acc_s[h, :, HEAD_DIM:],
                                          gattn_ref[:, hs], ga_ref[0, :, hs])


def _fox_prompt(q, kb, vp, cT, ga, gattn, tq):
    B, S, DA = q.shape
    H = cT.shape[1]
    blk = pl.BlockSpec((1, tq, DA), lambda b, i: (b, i, 0))
    seq = lambda n: pl.BlockSpec((1, S, n), lambda b, i: (b, 0, 0))
    return pl.pallas_call(
        functools.partial(_fox_prompt_kernel, tq=tq),
        grid=(B, S // tq),
        in_specs=[blk, seq(DA), seq(vp.shape[2]), pl.BlockSpec((1, H, S), lambda b, i: (b, 0, 0)),
                  blk, _resident(gattn.shape)],
        out_specs=blk,
        out_shape=jax.ShapeDtypeStruct((B, S, DA), BF16),
        scratch_shapes=[pltpu.VMEM((H, tq, LANES), F32),
                        pltpu.VMEM((H, tq, 2 * HEAD_DIM), F32)],
        compiler_params=_params("parallel", "arbitrary"),
        name="fox_prompt",
    )(q, kb, vp, cT, ga, gattn)


def _fox_sample_kernel(q_ref, ck_ref, cv_ref, cpT_ref, kn_ref, vn_ref, cnT_ref, cnc_ref, ga_ref,
                       gattn_ref, ya_ref, m_s, l_s, acc_s):
    n_heads = cpT_ref.shape[1]
    T = q_ref.shape[1]
    tk = cpT_ref.shape[2]
    j = pl.program_id(1)

    @pl.when(j == 0)
    def _():
        m_s[...] = jnp.full(m_s.shape, -jnp.inf, F32)
        l_s[...] = jnp.zeros(l_s.shape, F32)
        acc_s[...] = jnp.zeros(acc_s.shape, F32)

    def update(h, s, vals):
        m_old = m_s[h]
        m_new = jnp.maximum(m_old, jnp.max(s, axis=-1, keepdims=True))
        alpha = jnp.exp2(m_old - m_new)
        p = jnp.exp2(s - m_new)
        m_s[h] = m_new
        l_s[h] = alpha * l_s[h] + jnp.sum(p, axis=-1, keepdims=True)
        acc_s[h] = alpha * acc_s[h] + _dot(p.astype(BF16), vals)

    def cached_scores(h):
        hs = slice(h * HEAD_DIM, (h + 1) * HEAD_DIM)
        kh = ck_ref[0, pl.ds(h, tk, stride=n_heads), :].astype(BF16)
        return _dot_nt(q_ref[0, :, hs], kh) + (cnc_ref[0, :, h:h + 1] - cpT_ref[0, h:h + 1, :])

    s_next = cached_scores(0)
    for h in range(n_heads):
        s = s_next
        if h + 1 < n_heads:
            s_next = cached_scores(h + 1)
        update(h, s, cv_ref[0, pl.ds(h, tk, stride=n_heads), :].astype(BF16))

    @pl.when(j == pl.num_programs(1) - 1)
    def _():
        row = lax.broadcasted_iota(jnp.int32, (T, T), 0)
        col = lax.broadcasted_iota(jnp.int32, (T, T), 1)
        heads = [slice(h * HEAD_DIM, (h + 1) * HEAD_DIM) for h in range(n_heads)]
        scores = [_dot_nt(q_ref[0, :, hs], kn_ref[0, :, hs])
                  + (cnc_ref[0, :, h:h + 1] - cnT_ref[0, h:h + 1, :T]) for h, hs in enumerate(heads)]
        for h, hs in enumerate(heads):
            update(h, jnp.where(col <= row, scores[h], -jnp.inf), vn_ref[0, :, hs])
        for h, hs in enumerate(heads):
            ya_ref[0, :, hs] = _head_epilogue(acc_s[h], l_s[h], gattn_ref[:, hs], ga_ref[0, :, hs])


def _fox_sample(q, ck, cv, cpT, kn, vn, cnT, cnc, ga, gattn, tk):
    B, T, DA = q.shape
    H = cpT.shape[1]
    P = ck.shape[1] // H
    new = pl.BlockSpec((1, T, DA), lambda b, j: (b, 0, 0))
    past = pl.BlockSpec((1, tk * H, HEAD_DIM), lambda b, j: (b, j, 0))
    return pl.pallas_call(
        _fox_sample_kernel,
        grid=(B, P // tk),
        in_specs=[new, past, past,
                  pl.BlockSpec((1, H, tk), lambda b, j: (b, 0, j)),
                  new, new,
                  pl.BlockSpec((1, H, cnT.shape[2]), lambda b, j: (b, 0, 0)),
                  pl.BlockSpec((1, T, H), lambda b, j: (b, 0, 0)),
                  new, _resident(gattn.shape)],
        out_specs=new,
        out_shape=jax.ShapeDtypeStruct((B, T, DA), BF16),
        scratch_shapes=[pltpu.VMEM((H, T, 1), F32), pltpu.VMEM((H, T, 1), F32),
                        pltpu.VMEM((H, T, HEAD_DIM), F32)],
        compiler_params=_params("parallel", "arbitrary"),
        name="fox_sample",
    )(q, ck, cv, cpT, kn, vn, cnT, cnc, ga, gattn)


def _segment_pitch(seg_len):
    pitch = -(-seg_len // SUBLANES) * SUBLANES
    return pitch if (pitch // SUBLANES) % 2 else pitch + SUBLANES


def _rglru_init(t, hist_ref, h0_ref, tail_s, hcar_s):
    @pl.when(t == 0)
    def _():
        tail_s[...] = jnp.zeros(tail_s.shape, F32)
        tail_s[SUBLANES - (CONV_W - 1):, :] = hist_ref[...]
        hcar_s[...] = h0_ref[...]


def _rglru_block(t, xl_ref, gl_ref, cw_ref, cb_ref, wr_ref, br_ref, wi_ref, bi_ref,
                 lam_ref, glru_ref, yl_ref, hT_ref, hist_out_ref, xn_s, hn_s, hl_s, ac_s, tail_s, hcar_s,
                 *, tt, reset_first):
    n_blocks = wr_ref.shape[0]
    seg = tt // SUBLANES
    pitch = _segment_pitch(seg)
    sub = lax.broadcasted_iota(jnp.int32, (SUBLANES, LRU_BLOCK), 0)
    first = (lax.broadcasted_iota(jnp.int32, (tt, LRU_BLOCK), 0) == 0) & (t == 0)
    decay = -LRU_C * _softplus(-lam_ref[...])

    def shift_segments(v, head):
        return jnp.where(sub == 0, head, pltpu.roll(v, 1, 0))

    gated = []
    for n in range(n_blocks):
        ns = slice(n * LRU_BLOCK, (n + 1) * LRU_BLOCK)
        for s in range(SUBLANES):
            xn_s[n, s * pitch:s * pitch + seg, :] = xl_ref[s * seg:(s + 1) * seg, ns]
        x = [xn_s[n, pl.ds(i, SUBLANES, stride=pitch), :] for i in range(seg)]
        before = [shift_segments(x[seg - k], tail_s[SUBLANES - k:SUBLANES - k + 1, ns])
                  for k in range(CONV_W - 1, 0, -1)]
        xs = jnp.concatenate(before + x, axis=0)
        xc = cb_ref[:, ns] + xs[0:tt] * cw_ref[0:1, ns]
        for j in range(1, CONV_W):
            xc = xc + xs[j * SUBLANES:j * SUBLANES + tt] * cw_ref[j:j + 1, ns]

        xcb = xc.astype(BF16)
        gated.append((xc, _dot(xcb, wr_ref[n]), _dot(xcb, wi_ref[n])))

    tail_s[...] = xl_ref[tt - SUBLANES:, :]
    hist_out_ref[...] = xl_ref[tt - (CONV_W - 1):, :]
    _rglru_recurrence(gated, sub, first, decay, shift_segments, br_ref, bi_ref, glru_ref, gl_ref,
                      yl_ref, hT_ref, hn_s, hl_s, ac_s, hcar_s, seg=seg, pitch=pitch,
                      reset_first=reset_first)


def _rglru_recurrence(gated, sub, first, decay, shift_segments, br_ref, bi_ref, glru_ref, gl_ref,
                      yl_ref, hT_ref, hn_s, hl_s, ac_s, hcar_s, *, seg, pitch, reset_first):
    for n, (xc, zr, zi) in enumerate(gated):
        ns = slice(n * LRU_BLOCK, (n + 1) * LRU_BLOCK)
        r = _sigmoid(zr + br_ref[:, ns])
        i_gate = _sigmoid(zi + bi_ref[:, ns])
        log_a = r * decay[:, ns]
        a = jnp.exp(log_a)
        one_minus_a2 = -jnp.tanh(log_a) * (1.0 + a * a)
        mult = jnp.where(one_minus_a2 > 0.0, one_minus_a2 * lax.rsqrt(one_minus_a2), 0.0)
        if reset_first:
            mult = jnp.where(first, 1.0, mult)
        u = mult * i_gate * xc

        rows = lambda v, i: v[i * SUBLANES:(i + 1) * SUBLANES]
        e, p = rows(u, 0), rows(a, 0)
        hl_s[n, 0:SUBLANES, :] = e
        ac_s[n, 0:SUBLANES, :] = p
        for i in range(1, seg):
            e = rows(a, i) * e + rows(u, i)
            p = rows(a, i) * p
            hl_s[n, i * SUBLANES:(i + 1) * SUBLANES, :] = e
            ac_s[n, i * SUBLANES:(i + 1) * SUBLANES, :] = p
        for step in (1, 2, 4):
            keep = sub >= step
            e = e + p * jnp.where(keep, pltpu.roll(e, step, 0), 0.0)
            p = p * jnp.where(keep, pltpu.roll(p, step, 0), 1.0)
        h_end = e + p * hcar_s[:, ns]
        carry_in = shift_segments(h_end, hcar_s[:, ns])
        hcar_s[:, ns] = h_end[SUBLANES - 1:, :]
        hT_ref[:, ns] = h_end[SUBLANES - 1:, :]

        for i in range(seg):
            hn_s[n, pl.ds(i, SUBLANES, stride=pitch), :] = (
                hl_s[n, i * SUBLANES:(i + 1) * SUBLANES, :]
                + ac_s[n, i * SUBLANES:(i + 1) * SUBLANES, :] * carry_in)
        h = jnp.concatenate([hn_s[n, s * pitch:s * pitch + seg, :] for s in range(SUBLANES)], axis=0)
        y = _group_rms(h) * glru_ref[:, ns] * gl_ref[:, ns]
        yl_ref[:, ns] = y.astype(BF16)


def _rglru_kernel(xl_ref, gl_ref, hist_ref, h0_ref, *rest, tt, reset_first):
    weights, (yl_ref, hT_ref, hist_out_ref), scratch = rest[:8], rest[8:11], rest[11:]
    xn_s, hn_s, hl_s, ac_s, tail_s, hcar_s = scratch
    t = pl.program_id(1)
    for g in range(xl_ref.shape[0]):
        _rglru_init(t, hist_ref.at[g], h0_ref.at[g], tail_s.at[g], hcar_s.at[g])
        _rglru_block(t, xl_ref.at[g], gl_ref.at[g], *weights, yl_ref.at[g], hT_ref.at[g],
                     hist_out_ref.at[g], xn_s, hn_s, hl_s, ac_s, tail_s.at[g], hcar_s.at[g],
                     tt=tt, reset_first=reset_first)


def _in_proj_rglru_kernel(x_ref, wq_ref, wk_ref, wv_ref, wf_ref, wga_ref, wxl_ref, wgl_ref, bf_ref,
                          hist_ref, h0_ref, *rest, tt, reset_first):
    lru_w = rest[:8]
    (q_ref, k_ref, v_ref, kb_ref, vp_ref, logf_ref, ga_ref, yl_ref, hT_ref, hist_out_ref,
     logf_t_ref) = rest[8:19]
    xl_s, gl_s, *lru_scratch = rest[19:]
    n_heads = logf_ref.shape[-1]
    t = pl.program_id(1)
    _rglru_init(t, hist_ref.at[0], h0_ref.at[0], *lru_scratch[4:])
    xb = x_ref[0].astype(BF16)
    xl_s[...] = _dot(xb, wxl_ref[...])
    gl_s[...] = _silu(_dot(xb, wgl_ref[...]))
    _rglru_block(t, xl_s, gl_s, *lru_w, yl_ref.at[0], hT_ref.at[0], hist_out_ref.at[0],
                 *lru_scratch, tt=tt, reset_first=reset_first)
    q_ref[0] = (_dot(xb, wq_ref[...]) * Q_SCALE).astype(BF16)
    k = _dot(xb, wk_ref[...])
    k_ref[0] = k
    kb_ref[0] = k.astype(BF16)
    v = _dot(xb, wv_ref[...])
    v_ref[0] = v
    ones = jnp.ones((tt, HEAD_DIM), BF16)
    for h in range(n_heads):
        vp_ref[0, :, 2 * h * HEAD_DIM:(2 * h + 1) * HEAD_DIM] = (
            v[:, h * HEAD_DIM:(h + 1) * HEAD_DIM].astype(BF16))
        vp_ref[0, :, (2 * h + 1) * HEAD_DIM:(2 * h + 2) * HEAD_DIM] = ones
    logf = _log_sigmoid(_dot(xb, wf_ref[...]) + bf_ref[...])
    logf_ref[0] = logf[:, :n_heads]
    logf_t_ref[0] = logf.T[:n_heads, :]
    ga_ref[0] = _silu(_dot(xb, wga_ref[...]))


def _in_proj_rglru(x, wts, hist, h0, lw, tt, reset_first):
    B, S, D = x.shape
    wq, wk, wv, wf, wga, wxl, wgl, bf, n_heads = wts
    d_attn, d_lru = wq.shape[1], wxl.shape[1]
    blk = lambda n: pl.BlockSpec((1, tt, n), lambda b, t: (b, t, 0))
    per_b = lambda n: pl.BlockSpec((1, n, d_lru), lambda b, t: (b, 0, 0))
    slab_rows = SUBLANES * _segment_pitch(tt // SUBLANES)
    seq = lambda n, dt: jax.ShapeDtypeStruct((B, S, n), dt)
    out_shape = (seq(d_attn, BF16), seq(d_attn, F32), seq(d_attn, F32), seq(d_attn, BF16),
                 seq(2 * d_attn, BF16), seq(n_heads, F32), seq(d_attn, F32), seq(d_lru, BF16),
                 jax.ShapeDtypeStruct((B, 1, d_lru), F32),
                 jax.ShapeDtypeStruct((B, CONV_W - 1, d_lru), F32),
                 jax.ShapeDtypeStruct((B, n_heads, S), F32))
    return pl.pallas_call(
        functools.partial(_in_proj_rglru_kernel, tt=tt, reset_first=reset_first),
        grid=(B, S // tt),
        in_specs=[blk(D)] + [_resident(w.shape) for w in (wq, wk, wv, wf, wga, wxl, wgl, bf)]
                 + [per_b(CONV_W - 1), per_b(1)] + [_resident(w.shape) for w in lw],
        out_specs=tuple(blk(s.shape[2]) for s in out_shape[:8]) + (per_b(1), per_b(CONV_W - 1))
                  + (pl.BlockSpec((1, n_heads, tt), lambda b, t: (b, 0, t)),),
        out_shape=out_shape,
        scratch_shapes=[pltpu.VMEM((tt, d_lru), F32), pltpu.VMEM((tt, d_lru), F32),
                        pltpu.VMEM((lw[2].shape[0], slab_rows, LRU_BLOCK), F32),
                        pltpu.VMEM((lw[2].shape[0], slab_rows, LRU_BLOCK), F32),
                        pltpu.VMEM((lw[2].shape[0], tt, LRU_BLOCK), F32),
                        pltpu.VMEM((lw[2].shape[0], tt, LRU_BLOCK), F32),
                        pltpu.VMEM((SUBLANES, d_lru), F32), pltpu.VMEM((1, d_lru), F32)],
        compiler_params=_params("parallel", "arbitrary"),
        name="in_proj_rglru",
    )(x, wq, wk, wv, wf, wga, wxl, wgl, bf, hist, h0, *lw)


RGLRU_STREAMS = 4


def _rglru(xl, gl, hist, h0, lw, tt, reset_first):
    B, T, DL = xl.shape
    cw, cb, wr, br, wi, bi, lam, glru = lw
    G = _tile(B, RGLRU_STREAMS)
    blk = pl.BlockSpec((G, tt, DL), lambda b, t: (b, t, 0))
    per_b = lambda n: pl.BlockSpec((G, n, DL), lambda b, t: (b, 0, 0))
    slab_rows = SUBLANES * _segment_pitch(tt // SUBLANES)
    return pl.pallas_call(
        functools.partial(_rglru_kernel, tt=tt, reset_first=reset_first),
        grid=(B // G, T // tt),
        in_specs=[blk, blk, per_b(CONV_W - 1), per_b(1)]
                 + [_resident(w.shape) for w in (cw, cb, wr, br, wi, bi, lam, glru)],
        out_specs=(blk, per_b(1), per_b(CONV_W - 1)),
        out_shape=(jax.ShapeDtypeStruct((B, T, DL), BF16),
                   jax.ShapeDtypeStruct((B, 1, DL), F32),
                   jax.ShapeDtypeStruct((B, CONV_W - 1, DL), F32)),
        scratch_shapes=[pltpu.VMEM((wr.shape[0], slab_rows, LRU_BLOCK), F32),
                        pltpu.VMEM((wr.shape[0], slab_rows, LRU_BLOCK), F32),
                        pltpu.VMEM((wr.shape[0], tt, LRU_BLOCK), F32),
                        pltpu.VMEM((wr.shape[0], tt, LRU_BLOCK), F32),
                        pltpu.VMEM((G, SUBLANES, DL), F32), pltpu.VMEM((G, 1, DL), F32)],
        compiler_params=_params("parallel", "arbitrary"),
        name="rglru",
    )(xl, gl, hist, h0, cw, cb, wr, br, wi, bi, lam, glru)


def _out_proj_kernel(x_ref, ya_ref, yl_ref, wa_ref, wl_ref, g_ref, b_ref, o_ref, *, alpha):
    tm = x_ref.shape[0]
    rows = [slice(r, r + OUT_PROJ_ROWS) for r in range(0, tm, OUT_PROJ_ROWS)]
    outs = [_dot(ya_ref[r, :], wa_ref[...]) + _dot(yl_ref[r, :], wl_ref[...]) for r in rows]
    for r, out in zip(rows, outs):
        h = alpha * x_ref[r, :] + out
        mu = jnp.mean(h, axis=-1, keepdims=True)
        d = h - mu
        var = jnp.mean(d * d, axis=-1, keepdims=True)
        o_ref[r, :] = d * lax.rsqrt(var + LN_EPS) * g_ref[...] + b_ref[...]


def _out_proj(x2d, ya, yl, wa, wl, g, b, alpha, tm):
    M, D = x2d.shape
    row = lambda n: pl.BlockSpec((tm, n), lambda i: (i, 0))
    return pl.pallas_call(
        functools.partial(_out_proj_kernel, alpha=alpha),
        grid=(M // tm,),
        in_specs=[row(D), row(ya.shape[1]), row(yl.shape[1])]
                 + [_resident(w.shape) for w in (wa, wl, g, b)],
        out_specs=row(D),
        out_shape=jax.ShapeDtypeStruct((M, D), F32),
        compiler_params=_params("parallel"),
        name="out_proj",
    )(x2d, ya, yl, wa, wl, g, b)


def _tile(n, pref):
    return pref if n % pref == 0 else n


def kernel(x_prompt, x_sample, cache_k, cache_v, cache_logf, state_h, state_conv, w_in, b_f, conv_w,
           conv_b, w_r, b_r, w_i, b_i, lru_lambda, g_attn, g_lru, w_out, ln_g, ln_b):
    depth, d_model, _ = w_in.shape
    n_heads = b_f.shape[1]
    d_attn = n_heads * HEAD_DIM
    d_lru = lru_lambda.shape[1]
    alpha = (2.0 * depth) ** 0.25
    B, S, _ = x_prompt.shape
    DB, T, _ = x_sample.shape
    P = cache_k.shape[2]

    xp = x_prompt.reshape(B * S, d_model)
    xs = x_sample.reshape(DB * T, d_model)
    outs_p, outs_s = [], []
    for l in range(depth):
        w = w_in[l]
        o = (d_attn, 2 * d_attn, 3 * d_attn, 3 * d_attn + n_heads, 4 * d_attn + n_heads,
             4 * d_attn + n_heads + d_lru)
        wf = jnp.pad(w[:, o[2]:o[3]], ((0, 0), (0, LANES - n_heads)))
        bf = jnp.pad(b_f[l][None, :], ((0, 0), (0, LANES - n_heads)))
        wts = (w[:, :o[0]].astype(BF16), w[:, o[0]:o[1]].astype(BF16), w[:, o[1]:o[2]].astype(BF16),
               wf.astype(BF16), w[:, o[3]:o[4]].astype(BF16), w[:, o[4]:o[5]].astype(BF16),
               w[:, o[5]:].astype(BF16), bf, n_heads)
        lw = (conv_w[l], conv_b[l][None], w_r[l].astype(BF16), b_r[l][None], w_i[l].astype(BF16),
              b_i[l][None], lru_lambda[l][None], g_lru[l][None])
        gattn = g_attn[l][None]
        wa = w_out[l][:d_attn].astype(BF16)
        wl = w_out[l][d_attn:].astype(BF16)
        lng, lnb = ln_g[l][None], ln_b[l][None]

        q, k, v, kb, vp, logf, ga, yl, h_T, hist_T, logf_t = _in_proj_rglru(
            xp.reshape(B, S, d_model), wts, jnp.zeros((B, CONV_W - 1, d_lru), F32),
            jnp.zeros((B, 1, d_lru), F32), lw, _tile(S, 256), True)
        cT = _cumsum(logf_t)
        ya = _fox_prompt(q, kb, vp, cT, ga, gattn, _tile(S, 512))
        outs_p.append((k.reshape(B, S, n_heads, HEAD_DIM), v.reshape(B, S, n_heads, HEAD_DIM),
                       logf, h_T.reshape(B, d_lru), hist_T))
        xp = _out_proj(xp, ya.reshape(B * S, d_attn), yl.reshape(B * S, d_lru), wa, wl, lng, lnb,
                       alpha, _tile(B * S, 512))

        q, k, v, kb, vb, logf, ga, xl, gl = _in_proj(xs, wts, _tile(DB * T, 256))
        r3 = lambda a: a.reshape(DB, T, a.shape[-1])
        newT = jnp.pad(jnp.transpose(r3(logf), (0, 2, 1)), ((0, 0), (0, 0), (0, LANES - T)))
        cpT, cnT = _cumsum_carry(jnp.transpose(cache_logf[l].astype(F32), (0, 2, 1)), newT)
        cnc = jnp.transpose(cnT[:, :, :T], (0, 2, 1))
        ya = _fox_sample(r3(q), cache_k[l].reshape(DB, P * n_heads, HEAD_DIM),
                         cache_v[l].reshape(DB, P * n_heads, HEAD_DIM),
                         cpT, r3(kb), r3(vb), cnT, cnc, r3(ga), gattn, _tile(P, 2048))
        yl, h_T, hist_T = _rglru(r3(xl), r3(gl), state_conv[l], state_h[l][:, None, :], lw,
                                 _tile(T, 256), False)
        outs_s.append((k.reshape(DB, T, n_heads, HEAD_DIM), v.reshape(DB, T, n_heads, HEAD_DIM),
                       r3(logf), h_T.reshape(DB, d_lru), hist_T))
        xs = _out_proj(xs, ya.reshape(DB * T, d_attn), yl.reshape(DB * T, d_lru), wa, wl, lng, lnb,
                       alpha, _tile(DB * T, 256))

    stack = lambda outs, i: jnp.stack([o[i] for o in outs], 0)
    return (xp.reshape(B, S, d_model), xs.reshape(DB, T, d_model),
            stack(outs_p, 0), stack(outs_p, 1), stack(outs_p, 2), stack(outs_p, 3), stack(outs_p, 4),
            stack(outs_s, 0), stack(outs_s, 1), stack(outs_s, 2), stack(outs_s, 3), stack(outs_s, 4))
```

```python
import functools

import jax
import jax.numpy as jnp
from jax import lax
from jax.experimental import pallas as pl
from jax.experimental.pallas import tpu as pltpu

F32 = jnp.float32
BF16 = jnp.bfloat16

HEAD_DIM = 128
LRU_BLOCK = 128
CONV_W = 4
LRU_C = 8.0
LN_EPS = 1e-5
RMS_EPS = 1e-6
LOG2E = 1.4426950408889634
Q_SCALE = HEAD_DIM ** -0.5 * LOG2E

LANES = 128
SUBLANES = 8
V7X_VMEM_LIMIT = 56 * 2 ** 20
OUT_PROJ_ROWS = 128


def _dot(a, b):
    return jnp.dot(a, b, preferred_element_type=F32)


def _dot_nt(a, b):
    return lax.dot_general(a, b, (((1,), (1,)), ((), ())), preferred_element_type=F32)


def _softplus(y):
    return jnp.maximum(y, 0.0) + jnp.log1p(jnp.exp(-jnp.abs(y)))


def _log_sigmoid(y):
    return -_softplus(-y)


def _sigmoid(y):
    return 1.0 / (1.0 + jnp.exp2(y * -LOG2E))


def _silu(y):
    return y * _sigmoid(y)


def _group_rms(y):
    return y * lax.rsqrt(jnp.mean(y * y, axis=-1, keepdims=True) + RMS_EPS)


def _params(*semantics):
    return pltpu.CompilerParams(dimension_semantics=semantics, vmem_limit_bytes=V7X_VMEM_LIMIT)


def _resident(shape):
    return pl.BlockSpec(shape, lambda *_: (0,) * len(shape), pipeline_mode=pl.Buffered(1))


def _in_proj_kernel(x_ref, wq_ref, wk_ref, wv_ref, wf_ref, wga_ref, wxl_ref, wgl_ref, bf_ref,
                    q_ref, k_ref, v_ref, kb_ref, vb_ref, logf_ref, ga_ref, xl_ref, gl_ref):
    n_heads = logf_ref.shape[-1]
    xb = x_ref[...].astype(BF16)
    q_ref[...] = (_dot(xb, wq_ref[...]) * Q_SCALE).astype(BF16)
    k = _dot(xb, wk_ref[...])
    k_ref[...] = k
    kb_ref[...] = k.astype(BF16)
    v = _dot(xb, wv_ref[...])
    v_ref[...] = v
    vb_ref[...] = v.astype(BF16)
    zf = _dot(xb, wf_ref[...]) + bf_ref[...]
    logf_ref[...] = _log_sigmoid(zf)[:, :n_heads]
    ga_ref[...] = _silu(_dot(xb, wga_ref[...]))
    xl_ref[...] = _dot(xb, wxl_ref[...])
    gl_ref[...] = _silu(_dot(xb, wgl_ref[...]))


def _in_proj(x2d, wts, tm):
    M, D = x2d.shape
    wq, wk, wv, wf, wga, wxl, wgl, bf, n_heads = wts
    d_attn, d_lru = wq.shape[1], wxl.shape[1]
    row = lambda n: pl.BlockSpec((tm, n), lambda i: (i, 0))
    out_shape = (
        jax.ShapeDtypeStruct((M, d_attn), BF16),
        jax.ShapeDtypeStruct((M, d_attn), F32),
        jax.ShapeDtypeStruct((M, d_attn), F32),
        jax.ShapeDtypeStruct((M, d_attn), BF16),
        jax.ShapeDtypeStruct((M, d_attn), BF16),
        jax.ShapeDtypeStruct((M, n_heads), F32),
        jax.ShapeDtypeStruct((M, d_attn), F32),
        jax.ShapeDtypeStruct((M, d_lru), F32),
        jax.ShapeDtypeStruct((M, d_lru), F32),
    )
    return pl.pallas_call(
        _in_proj_kernel,
        grid=(M // tm,),
        in_specs=[row(D)] + [_resident(w.shape) for w in (wq, wk, wv, wf, wga, wxl, wgl, bf)],
        out_specs=tuple(row(s.shape[1]) for s in out_shape),
        out_shape=out_shape,
        compiler_params=_params("parallel"),
        name="in_proj",
    )(x2d, wq, wk, wv, wf, wga, wxl, wgl, bf)


def _cumsum_lanes(x):
    n = x.shape[-1]
    lane = lax.broadcasted_iota(jnp.int32, x.shape, x.ndim - 1)
    step = 1
    while step < n:
        x = x + jnp.where(lane >= step, pltpu.roll(x, step, x.ndim - 1), 0.0)
        step *= 2
    return x


def _rows(ref):
    return ref[...].reshape(ref.shape[0] * ref.shape[1], ref.shape[2])


def _cumsum_kernel(f_ref, c_ref):
    c_ref[...] = (_cumsum_lanes(_rows(f_ref)) * LOG2E).reshape(c_ref.shape)


def _cumsum_carry_kernel(past_ref, new_ref, cpast_ref, cnew_ref):
    cp = _cumsum_lanes(_rows(past_ref))
    cpast_ref[...] = (cp * LOG2E).reshape(cpast_ref.shape)
    cn = (_cumsum_lanes(_rows(new_ref)) + cp[:, cp.shape[1] - 1:]) * LOG2E
    cnew_ref[...] = cn.reshape(cnew_ref.shape)


CUMSUM_STREAMS = 4


def _cumsum(fT):
    B, H, S = fT.shape
    nb = _tile(B, CUMSUM_STREAMS)
    spec = pl.BlockSpec((nb, H, S), lambda b: (b, 0, 0))
    return pl.pallas_call(
        _cumsum_kernel, grid=(B // nb,), in_specs=[spec], out_specs=spec,
        out_shape=jax.ShapeDtypeStruct(fT.shape, F32),
        compiler_params=_params("parallel"), name="cumsum",
    )(fT)


def _cumsum_carry(pastT, newT):
    B, H, P = pastT.shape
    N = newT.shape[2]
    nb = _tile(B, CUMSUM_STREAMS)
    pspec = pl.BlockSpec((nb, H, P), lambda b: (b, 0, 0))
    nspec = pl.BlockSpec((nb, H, N), lambda b: (b, 0, 0))
    return pl.pallas_call(
        _cumsum_carry_kernel, grid=(B // nb,), in_specs=[pspec, nspec], out_specs=(pspec, nspec),
        out_shape=(jax.ShapeDtypeStruct(pastT.shape, F32), jax.ShapeDtypeStruct(newT.shape, F32)),
        compiler_params=_params("parallel"), name="cumsum_carry",
    )(pastT, newT)


def _head_epilogue(acc, l, gain, gate):
    normed = acc * lax.rsqrt(jnp.mean(acc * acc, axis=-1, keepdims=True) + RMS_EPS * (l * l))
    return (normed * gain * gate).astype(BF16)


def _fox_prompt_kernel(q_ref, kb_ref, vp_ref, cT_ref, ga_ref, gattn_ref, ya_ref,
                       m_s, acc_s, *, tq):
    n_heads = cT_ref.shape[1]
    qi = pl.program_id(1)
    q0 = pl.multiple_of(qi * tq, tq)

    def values(h, k0, rows):
        return vp_ref[0, pl.ds(k0, rows), 2 * h * HEAD_DIM:(2 * h + 2) * HEAD_DIM]

    def scores(h, k0):
        hs = slice(h * HEAD_DIM, (h + 1) * HEAD_DIM)
        s = _dot_nt(q_ref[0, :, hs], kb_ref[0, pl.ds(k0, tq), hs])
        return s - cT_ref[0, h:h + 1, pl.ds(k0, tq)]

    def row_max(s):
        m = s[:, :LANES]
        for c in range(1, s.shape[1] // LANES):
            m = jnp.maximum(m, s[:, c * LANES:(c + 1) * LANES])
        return jnp.max(m, axis=-1, keepdims=True)

    def weights(s, m):
        parts = [jnp.exp2(s[:, c * LANES:(c + 1) * LANES] - m) for c in range(s.shape[1] // LANES)]
        return jnp.concatenate(parts, axis=1).astype(BF16)

    half = tq // 2
    visible_a = (lax.broadcasted_iota(jnp.int32, (tq, half), 1)
                 <= lax.broadcasted_iota(jnp.int32, (tq, half), 0))
    visible_b = (lax.broadcasted_iota(jnp.int32, (half, half), 1)
                 <= lax.broadcasted_iota(jnp.int32, (half, half), 0))

    def diag_scores(h):
        hs = slice(h * HEAD_DIM, (h + 1) * HEAD_DIM)
        sa = _dot_nt(q_ref[0, :, hs], kb_ref[0, pl.ds(q0, half), hs])
        sb = _dot_nt(q_ref[0, half:, hs], kb_ref[0, pl.ds(q0 + half, half), hs])
        return (sa - cT_ref[0, h:h + 1, pl.ds(q0, half)],
                sb - cT_ref[0, h:h + 1, pl.ds(q0 + half, half)])

    s_next = diag_scores(0)
    for h in range(n_heads):
        sa = jnp.where(visible_a, s_next[0], -jnp.inf)
        sb = jnp.where(visible_b, s_next[1], -jnp.inf)
        if h + 1 < n_heads:
            s_next = diag_scores(h + 1)
        s_top = sa[:half]
        s_bot = jnp.concatenate([sa[half:], sb], axis=1)
        m_top = jnp.broadcast_to(row_max(s_top), (half, LANES))
        m_bot = jnp.broadcast_to(row_max(s_bot), (half, LANES))
        m_s[h, :half, :] = m_top
        m_s[h, half:, :] = m_bot
        acc_s[h, :half, :] = _dot(weights(s_top, m_top), values(h, q0, half))
        acc_s[h, half:, :] = _dot(weights(s_bot, m_bot), values(h, q0, tq))

    def body(j, carry):
        k0 = pl.multiple_of(j * tq, tq)
        s_next = scores(0, k0)
        for h in range(n_heads):
            s = s_next
            if h + 1 < n_heads:
                s_next = scores(h + 1, k0)
            m_old = m_s[h]
            m_new = jnp.maximum(m_old, row_max(s))
            alpha = jnp.exp2(m_old - m_new)
            m_s[h] = m_new
            pv = _dot(weights(s, m_new), values(h, k0, tq))
            acc_s[h, :, :HEAD_DIM] = alpha * acc_s[h, :, :HEAD_DIM] + pv[:, :HEAD_DIM]
            acc_s[h, :, HEAD_DIM:] = alpha * acc_s[h, :, HEAD_DIM:] + pv[:, HEAD_DIM:]
        return carry

    lax.fori_loop(0, qi, body, 0)

    for h in range(n_heads):
        hs = slice(h * HEAD_DIM, (h + 1) * HEAD_DIM)
        ya_ref[0, :, hs] = _head_epilogue(acc_s[h, :, :HEAD_DIM], acc_s[h, :, HEAD_DIM:],
                                          gattn_ref[:, hs], ga_ref[0, :, hs])


def _fox_prompt(q, kb, vp, cT, ga, gattn, tq):
    B, S, DA = q.shape
    H = cT.shape[1]
    blk = pl.BlockSpec((1, tq, DA), lambda b, i: (b, i, 0))
    seq = lambda n: pl.BlockSpec((1, S, n), lambda b, i: (b, 0, 0))
    return pl.pallas_call(
        functools.partial(_fox_prompt_kernel, tq=tq),
        grid=(B, S // tq),
        in_specs=[blk, seq(DA), seq(vp.shape[2]), pl.BlockSpec((1, H, S), lambda b, i: (b, 0, 0)),
                  blk, _resident(gattn.shape)],
        out_specs=blk,
        out_shape=jax.ShapeDtypeStruct((B, S, DA), BF16),
        scratch_shapes=[pltpu.VMEM((H, tq, LANES), F32),
                        pltpu.VMEM((H, tq, 2 * HEAD_DIM), F32)],
        compiler_params=_params("parallel", "arbitrary"),
        name="fox_prompt",
    )(q, kb, vp, cT, ga, gattn)


def _fox_sample_kernel(q_ref, ck_ref, cv_ref, cpT_ref, kn_ref, vn_ref, cnT_ref, cnc_ref, ga_ref,
                       gattn_ref, ya_ref, m_s, l_s, acc_s):
    n_heads = cpT_ref.shape[1]
    T = q_ref.shape[1]
    tk = cpT_ref.shape[2]
    j = pl.program_id(1)

    @pl.when(j == 0)
    def _():
        m_s[...] = jnp.full(m_s.shape, -jnp.inf, F32)
        l_s[...] = jnp.zeros(l_s.shape, F32)
        acc_s[...] = jnp.zeros(acc_s.shape, F32)

    def update(h, s, vals):
        m_old = m_s[h]
        m_new = jnp.maximum(m_old, jnp.max(s, axis=-1, keepdims=True))
        alpha = jnp.exp2(m_old - m_new)
        p = jnp.exp2(s - m_new)
        m_s[h] = m_new
        l_s[h] = alpha * l_s[h] + jnp.sum(p, axis=-1, keepdims=True)
        acc_s[h] = alpha * acc_s[h] + _dot(p.astype(BF16), vals)

    def cached_scores(h):
        hs = slice(h * HEAD_DIM, (h + 1) * HEAD_DIM)
        kh = ck_ref[0, pl.ds(h, tk, stride=n_heads), :].astype(BF16)
        return _dot_nt(q_ref[0, :, hs], kh) + (cnc_ref[0, :, h:h + 1] - cpT_ref[0, h:h + 1, :])

    s_next = cached_scores(0)
    for h in range(n_heads):
        s = s_next
        if h + 1 < n_heads:
            s_next = cached_scores(h + 1)
        update(h, s, cv_ref[0, pl.ds(h, tk, stride=n_heads), :].astype(BF16))

    @pl.when(j == pl.num_programs(1) - 1)
    def _():
        row = lax.broadcasted_iota(jnp.int32, (T, T), 0)
        col = lax.broadcasted_iota(jnp.int32, (T, T), 1)
        heads = [slice(h * HEAD_DIM, (h + 1) * HEAD_DIM) for h in range(n_heads)]
        scores = [_dot_nt(q_ref[0, :, hs], kn_ref[0, :, hs])
                  + (cnc_ref[0, :, h:h + 1] - cnT_ref[0, h:h + 1, :T]) for h, hs in enumerate(heads)]
        for h, hs in enumerate(heads):
            update(h, jnp.where(col <= row, scores[h], -jnp.inf), vn_ref[0, :, hs])
        for h, hs in enumerate(heads):
            ya_ref[0, :, hs] = _head_epilogue(acc_s[h], l_s[h], gattn_ref[:, hs], ga_ref[0, :, hs])


def _fox_sample(q, ck, cv, cpT, kn, vn, cnT, cnc, ga, gattn, tk):
    B, T, DA = q.shape
    H = cpT.shape[1]
    P = ck.shape[1] // H
    new = pl.BlockSpec((1, T, DA), lambda b, j: (b, 0, 0))
    past = pl.BlockSpec((1, tk * H, HEAD_DIM), lambda b, j: (b, j, 0))
    return pl.pallas_call(
        _fox_sample_kernel,
        grid=(B, P // tk),
        in_specs=[new, past, past,
                  pl.BlockSpec((1, H, tk), lambda b, j: (b, 0, j)),
                  new, new,
                  pl.BlockSpec((1, H, cnT.shape[2]), lambda b, j: (b, 0, 0)),
                  pl.BlockSpec((1, T, H), lambda b, j: (b, 0, 0)),
                  new, _resident(gattn.shape)],
        out_specs=new,
        out_shape=jax.ShapeDtypeStruct((B, T, DA), BF16),
        scratch_shapes=[pltpu.VMEM((H, T, 1), F32), pltpu.VMEM((H, T, 1), F32),
                        pltpu.VMEM((H, T, HEAD_DIM), F32)],
        compiler_params=_params("parallel", "arbitrary"),
        name="fox_sample",
    )(q, ck, cv, cpT, kn, vn, cnT, cnc, ga, gattn)


def _segment_pitch(seg_len):
    pitch = -(-seg_len // SUBLANES) * SUBLANES
    return pitch if (pitch // SUBLANES) % 2 else pitch + SUBLANES


def _rglru_init(t, hist_ref, h0_ref, tail_s, hcar_s):
    @pl.when(t == 0)
    def _():
        tail_s[...] = jnp.zeros(tail_s.shape, F32)
        tail_s[SUBLANES - (CONV_W - 1):, :] = hist_ref[...]
        hcar_s[...] = h0_ref[...]


def _rglru_block(t, xl_ref, gl_ref, cw_ref, cb_ref, wr_ref, br_ref, wi_ref, bi_ref,
                 lam_ref, glru_ref, yl_ref, hT_ref, hist_out_ref, xn_s, hn_s, hl_s, ac_s, tail_s, hcar_s,
                 *, tt, reset_first):
    n_blocks = wr_ref.shape[0]
    seg = tt // SUBLANES
    pitch = _segment_pitch(seg)
    sub = lax.broadcasted_iota(jnp.int32, (SUBLANES, LRU_BLOCK), 0)
    first = (lax.broadcasted_iota(jnp.int32, (tt, LRU_BLOCK), 0) == 0) & (t == 0)
    decay = -LRU_C * _softplus(-lam_ref[...])

    def shift_segments(v, head):
        return jnp.where(sub == 0, head, pltpu.roll(v, 1, 0))

    gated = []
    for n in range(n_blocks):
        ns = slice(n * LRU_BLOCK, (n + 1) * LRU_BLOCK)
        for s in range(SUBLANES):
            xn_s[n, s * pitch:s * pitch + seg, :] = xl_ref[s * seg:(s + 1) * seg, ns]
        x = [xn_s[n, pl.ds(i, SUBLANES, stride=pitch), :] for i in range(seg)]
        before = [shift_segments(x[seg - k], tail_s[SUBLANES - k:SUBLANES - k + 1, ns])
                  for k in range(CONV_W - 1, 0, -1)]
        xs = jnp.concatenate(before + x, axis=0)
        xc = cb_ref[:, ns] + xs[0:tt] * cw_ref[0:1, ns]
        for j in range(1, CONV_W):
            xc = xc + xs[j * SUBLANES:j * SUBLANES + tt] * cw_ref[j:j + 1, ns]

        xcb = xc.astype(BF16)
        gated.append((xc, _dot(xcb, wr_ref[n]), _dot(xcb, wi_ref[n])))

    tail_s[...] = xl_ref[tt - SUBLANES:, :]
    hist_out_ref[...] = xl_ref[tt - (CONV_W - 1):, :]
    _rglru_recurrence(gated, sub, first, decay, shift_segments, br_ref, bi_ref, glru_ref, gl_ref,
                      yl_ref, hT_ref, hn_s, hl_s, ac_s, hcar_s, seg=seg, pitch=pitch,
                      reset_first=reset_first)


def _rglru_recurrence(gated, sub, first, decay, shift_segments, br_ref, bi_ref, glru_ref, gl_ref,
                      yl_ref, hT_ref, hn_s, hl_s, ac_s, hcar_s, *, seg, pitch, reset_first):
    for n, (xc, zr, zi) in enumerate(gated):
        ns = slice(n * LRU_BLOCK, (n + 1) * LRU_BLOCK)
        r = _sigmoid(zr + br_ref[:, ns])
        i_gate = _sigmoid(zi + bi_ref[:, ns])
        log_a = r * decay[:, ns]
        a = jnp.exp(log_a)
        one_minus_a2 = -jnp.tanh(log_a) * (1.0 + a * a)
        mult = jnp.where(one_minus_a2 > 0.0, one_minus_a2 * lax.rsqrt(one_minus_a2), 0.0)
        if reset_first:
            mult = jnp.where(first, 1.0, mult)
        u = mult * i_gate * xc

        rows = lambda v, i: v[i * SUBLANES:(i + 1) * SUBLANES]
        e, p = rows(u, 0), rows(a, 0)
        hl_s[n, 0:SUBLANES, :] = e
        ac_s[n, 0:SUBLANES, :] = p
        for i in range(1, seg):
            e = rows(a, i) * e + rows(u, i)
            p = rows(a, i) * p
            hl_s[n, i * SUBLANES:(i + 1) * SUBLANES, :] = e
            ac_s[n, i * SUBLANES:(i + 1) * SUBLANES, :] = p
        for step in (1, 2, 4):
            keep = sub >= step
            e = e + p * jnp.where(keep, pltpu.roll(e, step, 0), 0.0)
            p = p * jnp.where(keep, pltpu.roll(p, step, 0), 1.0)
        h_end = e + p * hcar_s[:, ns]
        carry_in = shift_segments(h_end, hcar_s[:, ns])
        hcar_s[:, ns] = h_end[SUBLANES - 1:, :]
        hT_ref[:, ns] = h_end[SUBLANES - 1:, :]

        for i in range(seg):
            hn_s[n, pl.ds(i, SUBLANES, stride=pitch), :] = (
                hl_s[n, i * SUBLANES:(i + 1) * SUBLANES, :]
                + ac_s[n, i * SUBLANES:(i + 1) * SUBLANES, :] * carry_in)
        h = jnp.concatenate([hn_s[n, s * pitch:s * pitch + seg, :] for s in range(SUBLANES)], axis=0)
        y = _group_rms(h) * glru_ref[:, ns] * gl_ref[:, ns]
        yl_ref[:, ns] = y.astype(BF16)


def _rglru_kernel(xl_ref, gl_ref, hist_ref, h0_ref, *rest, tt, reset_first):
    weights, (yl_ref, hT_ref, hist_out_ref), scratch = rest[:8], rest[8:11], rest[11:]
    xn_s, hn_s, hl_s, ac_s, tail_s, hcar_s = scratch
    t = pl.program_id(1)
    for g in range(xl_ref.shape[0]):
        _rglru_init(t, hist_ref.at[g], h0_ref.at[g], tail_s.at[g], hcar_s.at[g])
        _rglru_block(t, xl_ref.at[g], gl_ref.at[g], *weights, yl_ref.at[g], hT_ref.at[g],
                     hist_out_ref.at[g], xn_s, hn_s, hl_s, ac_s, tail_s.at[g], hcar_s.at[g],
                     tt=tt, reset_first=reset_first)


def _in_proj_rglru_kernel(x_ref, wq_ref, wk_ref, wv_ref, wf_ref, wga_ref, wxl_ref, wgl_ref, bf_ref,
                          hist_ref, h0_ref, *rest, tt, reset_first):
    lru_w = rest[:8]
    (q_ref, k_ref, v_ref, kb_ref, vp_ref, ga_ref, yl_ref, hT_ref, hist_out_ref,
     logf_t_ref) = rest[8:18]
    xl_s, gl_s, *lru_scratch = rest[18:]
    n_heads = logf_t_ref.shape[1]
    t = pl.program_id(1)
    _rglru_init(t, hist_ref.at[0], h0_ref.at[0], *lru_scratch[4:])
    xb = x_ref[0].astype(BF16)
    xl_s[...] = _dot(xb, wxl_ref[...])
    gl_s[...] = _silu(_dot(xb, wgl_ref[...]))
    _rglru_block(t, xl_s, gl_s, *lru_w, yl_ref.at[0], hT_ref.at[0], hist_out_ref.at[0],
                 *lru_scratch, tt=tt, reset_first=reset_first)
    q_ref[0] = (_dot(xb, wq_ref[...]) * Q_SCALE).astype(BF16)
    k = _dot(xb, wk_ref[...])
    k_ref[0] = k
    kb_ref[0] = k.astype(BF16)
    v = _dot(xb, wv_ref[...])
    v_ref[0] = v
    ones = jnp.ones((tt, HEAD_DIM), BF16)
    for h in range(n_heads):
        vp_ref[0, :, 2 * h * HEAD_DIM:(2 * h + 1) * HEAD_DIM] = (
            v[:, h * HEAD_DIM:(h + 1) * HEAD_DIM].astype(BF16))
        vp_ref[0, :, (2 * h + 1) * HEAD_DIM:(2 * h + 2) * HEAD_DIM] = ones
    logf = _log_sigmoid(_dot(xb, wf_ref[...]) + bf_ref[...])
    logf_t_ref[0] = logf.T[:n_heads, :]
    ga_ref[0] = _silu(_dot(xb, wga_ref[...]))


def _in_proj_rglru(x, wts, hist, h0, lw, tt, reset_first):
    B, S, D = x.shape
    wq, wk, wv, wf, wga, wxl, wgl, bf, n_heads = wts
    d_attn, d_lru = wq.shape[1], wxl.shape[1]
    blk = lambda n: pl.BlockSpec((1, tt, n), lambda b, t: (b, t, 0))
    per_b = lambda n: pl.BlockSpec((1, n, d_lru), lambda b, t: (b, 0, 0))
    slab_rows = SUBLANES * _segment_pitch(tt // SUBLANES)
    seq = lambda n, dt: jax.ShapeDtypeStruct((B, S, n), dt)
    out_shape = (seq(d_attn, BF16), seq(d_attn, F32), seq(d_attn, F32), seq(d_attn, BF16),
                 seq(2 * d_attn, BF16), seq(d_attn, F32), seq(d_lru, BF16),
                 jax.ShapeDtypeStruct((B, 1, d_lru), F32),
                 jax.ShapeDtypeStruct((B, CONV_W - 1, d_lru), F32),
                 jax.ShapeDtypeStruct((B, n_heads, S), F32))
    return pl.pallas_call(
        functools.partial(_in_proj_rglru_kernel, tt=tt, reset_first=reset_first),
        grid=(B, S // tt),
        in_specs=[blk(D)] + [_resident(w.shape) for w in (wq, wk, wv, wf, wga, wxl, wgl, bf)]
                 + [per_b(CONV_W - 1), per_b(1)] + [_resident(w.shape) for w in lw],
        out_specs=tuple(blk(s.shape[2]) for s in out_shape[:7]) + (per_b(1), per_b(CONV_W - 1))
                  + (pl.BlockSpec((1, n_heads, tt), lambda b, t: (b, 0, t)),),
        out_shape=out_shape,
        scratch_shapes=[pltpu.VMEM((tt, d_lru), F32), pltpu.VMEM((tt, d_lru), F32),
                        pltpu.VMEM((lw[2].shape[0], slab_rows, LRU_BLOCK), F32),
                        pltpu.VMEM((lw[2].shape[0], slab_rows, LRU_BLOCK), F32),
                        pltpu.VMEM((lw[2].shape[0], tt, LRU_BLOCK), F32),
                        pltpu.VMEM((lw[2].shape[0], tt, LRU_BLOCK), F32),
                        pltpu.VMEM((SUBLANES, d_lru), F32), pltpu.VMEM((1, d_lru), F32)],
        compiler_params=_params("parallel", "arbitrary"),
        name="in_proj_rglru",
    )(x, wq, wk, wv, wf, wga, wxl, wgl, bf, hist, h0, *lw)


RGLRU_STREAMS = 4


def _rglru(xl, gl, hist, h0, lw, tt, reset_first):
    B, T, DL = xl.shape
    cw, cb, wr, br, wi, bi, lam, glru = lw
    G = _tile(B, RGLRU_STREAMS)
    blk = pl.BlockSpec((G, tt, DL), lambda b, t: (b, t, 0))
    per_b = lambda n: pl.BlockSpec((G, n, DL), lambda b, t: (b, 0, 0))
    slab_rows = SUBLANES * _segment_pitch(tt // SUBLANES)
    return pl.pallas_call(
        functools.partial(_rglru_kernel, tt=tt, reset_first=reset_first),
        grid=(B // G, T // tt),
        in_specs=[blk, blk, per_b(CONV_W - 1), per_b(1)]
                 + [_resident(w.shape) for w in (cw, cb, wr, br, wi, bi, lam, glru)],
        out_specs=(blk, per_b(1), per_b(CONV_W - 1)),
        out_shape=(jax.ShapeDtypeStruct((B, T, DL), BF16),
                   jax.ShapeDtypeStruct((B, 1, DL), F32),
                   jax.ShapeDtypeStruct((B, CONV_W - 1, DL), F32)),
        scratch_shapes=[pltpu.VMEM((wr.shape[0], slab_rows, LRU_BLOCK), F32),
                        pltpu.VMEM((wr.shape[0], slab_rows, LRU_BLOCK), F32),
                        pltpu.VMEM((wr.shape[0], tt, LRU_BLOCK), F32),
                        pltpu.VMEM((wr.shape[0], tt, LRU_BLOCK), F32),
                        pltpu.VMEM((G, SUBLANES, DL), F32), pltpu.VMEM((G, 1, DL), F32)],
        compiler_params=_params("parallel", "arbitrary"),
        name="rglru",
    )(xl, gl, hist, h0, cw, cb, wr, br, wi, bi, lam, glru)


def _out_proj_kernel(x_ref, ya_ref, yl_ref, wa_ref, wl_ref, g_ref, b_ref, o_ref, *, alpha):
    tm = x_ref.shape[0]
    rows = [slice(r, r + OUT_PROJ_ROWS) for r in range(0, tm, OUT_PROJ_ROWS)]
    outs = [_dot(ya_ref[r, :], wa_ref[...]) + _dot(yl_ref[r, :], wl_ref[...]) for r in rows]
    for r, out in zip(rows, outs):
        h = alpha * x_ref[r, :] + out
        mu = jnp.mean(h, axis=-1, keepdims=True)
        d = h - mu
        var = jnp.mean(d * d, axis=-1, keepdims=True)
        o_ref[r, :] = d * lax.rsqrt(var + LN_EPS) * g_ref[...] + b_ref[...]


def _out_proj(x2d, ya, yl, wa, wl, g, b, alpha, tm):
    M, D = x2d.shape
    row = lambda n: pl.BlockSpec((tm, n), lambda i: (i, 0))
    return pl.pallas_call(
        functools.partial(_out_proj_kernel, alpha=alpha),
        grid=(M // tm,),
        in_specs=[row(D), row(ya.shape[1]), row(yl.shape[1])]
                 + [_resident(w.shape) for w in (wa, wl, g, b)],
        out_specs=row(D),
        out_shape=jax.ShapeDtypeStruct((M, D), F32),
        compiler_params=_params("parallel"),
        name="out_proj",
    )(x2d, ya, yl, wa, wl, g, b)


def _split_w_out_kernel(w_ref, wa_ref, wl_ref):
    wa_ref[...] = w_ref[0, 0].astype(BF16)
    wl_ref[...] = w_ref[0, 1].astype(BF16)


def _split_w_out(w_out, layer, d_attn, rows):
    depth, DM, D = w_out.shape
    halves = w_out.reshape(depth, 2, d_attn, D)
    spec = pl.BlockSpec((rows, D), lambda i: (i, 0))
    return pl.pallas_call(
        _split_w_out_kernel,
        grid=(d_attn // rows,),
        in_specs=[pl.BlockSpec((1, 2, rows, D), lambda i: (layer, 0, i, 0))],
        out_specs=(spec, spec),
        out_shape=(jax.ShapeDtypeStruct((d_attn, D), BF16),) * 2,
        compiler_params=_params("parallel"),
        name="split_w_out",
    )(halves)


def _tile(n, pref):
    return pref if n % pref == 0 else n


def kernel(x_prompt, x_sample, cache_k, cache_v, cache_logf, state_h, state_conv, w_in, b_f, conv_w,
           conv_b, w_r, b_r, w_i, b_i, lru_lambda, g_attn, g_lru, w_out, ln_g, ln_b):
    depth, d_model, _ = w_in.shape
    n_heads = b_f.shape[1]
    d_attn = n_heads * HEAD_DIM
    d_lru = lru_lambda.shape[1]
    alpha = (2.0 * depth) ** 0.25
    B, S, _ = x_prompt.shape
    DB, T, _ = x_sample.shape
    P = cache_k.shape[2]

    xp = x_prompt.reshape(B * S, d_model)
    xs = x_sample.reshape(DB * T, d_model)
    outs_p, outs_s = [], []
    for l in range(depth):
        w = w_in[l]
        o = (d_attn, 2 * d_attn, 3 * d_attn, 3 * d_attn + n_heads, 4 * d_attn + n_heads,
             4 * d_attn + n_heads + d_lru)
        wf = jnp.pad(w[:, o[2]:o[3]], ((0, 0), (0, LANES - n_heads)))
        bf = jnp.pad(b_f[l][None, :], ((0, 0), (0, LANES - n_heads)))
        wts = (w[:, :o[0]].astype(BF16), w[:, o[0]:o[1]].astype(BF16), w[:, o[1]:o[2]].astype(BF16),
               wf.astype(BF16), w[:, o[3]:o[4]].astype(BF16), w[:, o[4]:o[5]].astype(BF16),
               w[:, o[5]:].astype(BF16), bf, n_heads)
        lw = (conv_w[l], conv_b[l][None], w_r[l].astype(BF16), b_r[l][None], w_i[l].astype(BF16),
              b_i[l][None], lru_lambda[l][None], g_lru[l][None])
        gattn = g_attn[l][None]
        assert d_lru == d_attn, "w_out is split into two equal row halves"
        wa, wl = _split_w_out(w_out, l, d_attn, _tile(d_attn, 256))
        lng, lnb = ln_g[l][None], ln_b[l][None]

        q, k, v, kb, vp, ga, yl, h_T, hist_T, logf_t = _in_proj_rglru(
            xp.reshape(B, S, d_model), wts, jnp.zeros((B, CONV_W - 1, d_lru), F32),
            jnp.zeros((B, 1, d_lru), F32), lw, _tile(S, 256), True)
        cT = _cumsum(logf_t)
        ya = _fox_prompt(q, kb, vp, cT, ga, gattn, _tile(S, 512))
        outs_p.append((k.reshape(B, S, n_heads, HEAD_DIM), v.reshape(B, S, n_heads, HEAD_DIM),
                       jnp.transpose(logf_t, (0, 2, 1)), h_T.reshape(B, d_lru), hist_T))
        xp = _out_proj(xp, ya.reshape(B * S, d_attn), yl.reshape(B * S, d_lru), wa, wl, lng, lnb,
                       alpha, _tile(B * S, 512))

        q, k, v, kb, vb, logf, ga, xl, gl = _in_proj(xs, wts, _tile(DB * T, 256))
        r3 = lambda a: a.reshape(DB, T, a.shape[-1])
        newT = jnp.pad(jnp.transpose(r3(logf), (0, 2, 1)), ((0, 0), (0, 0), (0, LANES - T)))
        cpT, cnT = _cumsum_carry(jnp.transpose(cache_logf[l].astype(F32), (0, 2, 1)), newT)
        cnc = jnp.transpose(cnT[:, :, :T], (0, 2, 1))
        ya = _fox_sample(r3(q), cache_k[l].reshape(DB, P * n_heads, HEAD_DIM),
                         cache_v[l].reshape(DB, P * n_heads, HEAD_DIM),
                         cpT, r3(kb), r3(vb), cnT, cnc, r3(ga), gattn, _tile(P, 2048))
        yl, h_T, hist_T = _rglru(r3(xl), r3(gl), state_conv[l], state_h[l][:, None, :], lw,
                                 _tile(T, 256), False)
        outs_s.append((k.reshape(DB, T, n_heads, HEAD_DIM), v.reshape(DB, T, n_heads, HEAD_DIM),
                       r3(logf), h_T.reshape(DB, d_lru), hist_T))
        xs = _out_proj(xs, ya.reshape(DB * T, d_attn), yl.reshape(DB * T, d_lru), wa, wl, lng, lnb,
                       alpha, _tile(DB * T, 256))

    stack = lambda outs, i: jnp.stack([o[i] for o in outs], 0)
    return (xp.reshape(B, S, d_model), xs.reshape(DB, T, d_model),
            stack(outs_p, 0), stack(outs_p, 1), stack(outs_p, 2), stack(outs_p, 3), stack(outs_p, 4),
            stack(outs_s, 0), stack(outs_s, 1), stack(outs_s, 2), stack(outs_s, 3), stack(outs_s, 4))
```

```python
import functools

import jax
import jax.numpy as jnp
from jax import lax
from jax.experimental import pallas as pl
from jax.experimental.pallas import tpu as pltpu

F32 = jnp.float32
BF16 = jnp.bfloat16

HEAD_DIM = 128
LRU_BLOCK = 128
CONV_W = 4
LRU_C = 8.0
LN_EPS = 1e-5
RMS_EPS = 1e-6
LOG2E = 1.4426950408889634
Q_SCALE = HEAD_DIM ** -0.5 * LOG2E

LANES = 128
SUBLANES = 8
V7X_VMEM_LIMIT = 56 * 2 ** 20
OUT_PROJ_ROWS = 128


def _dot(a, b):
    return jnp.dot(a, b, preferred_element_type=F32)


def _dot_nt(a, b):
    return lax.dot_general(a, b, (((1,), (1,)), ((), ())), preferred_element_type=F32)


def _softplus(y):
    return jnp.maximum(y, 0.0) + jnp.log1p(jnp.exp(-jnp.abs(y)))


def _log_sigmoid(y):
    return -_softplus(-y)


def _sigmoid(y):
    return 1.0 / (1.0 + jnp.exp2(y * -LOG2E))


def _silu(y):
    return y * _sigmoid(y)


def _group_rms(y):
    return y * lax.rsqrt(jnp.mean(y * y, axis=-1, keepdims=True) + RMS_EPS)


def _params(*semantics):
    return pltpu.CompilerParams(dimension_semantics=semantics, vmem_limit_bytes=V7X_VMEM_LIMIT)


def _resident(shape):
    return pl.BlockSpec(shape, lambda *_: (0,) * len(shape), pipeline_mode=pl.Buffered(1))


def _in_proj_kernel(x_ref, wq_ref, wk_ref, wv_ref, wf_ref, wga_ref, wxl_ref, wgl_ref, bf_ref,
                    q_ref, k_ref, v_ref, kb_ref, vb_ref, logf_ref, ga_ref, xl_ref, gl_ref):
    n_heads = logf_ref.shape[-1]
    xb = x_ref[...].astype(BF16)
    q_ref[...] = (_dot(xb, wq_ref[...]) * Q_SCALE).astype(BF16)
    k = _dot(xb, wk_ref[...])
    k_ref[...] = k
    kb_ref[...] = k.astype(BF16)
    v = _dot(xb, wv_ref[...])
    v_ref[...] = v
    vb_ref[...] = v.astype(BF16)
    zf = _dot(xb, wf_ref[...]) + bf_ref[...]
    logf_ref[...] = _log_sigmoid(zf)[:, :n_heads]
    ga_ref[...] = _silu(_dot(xb, wga_ref[...]))
    xl_ref[...] = _dot(xb, wxl_ref[...])
    gl_ref[...] = _silu(_dot(xb, wgl_ref[...]))


def _in_proj(x2d, wts, tm):
    M, D = x2d.shape
    wq, wk, wv, wf, wga, wxl, wgl, bf, n_heads = wts
    d_attn, d_lru = wq.shape[1], wxl.shape[1]
    row = lambda n: pl.BlockSpec((tm, n), lambda i: (i, 0))
    out_shape = (
        jax.ShapeDtypeStruct((M, d_attn), BF16),
        jax.ShapeDtypeStruct((M, d_attn), F32),
        jax.ShapeDtypeStruct((M, d_attn), F32),
        jax.ShapeDtypeStruct((M, d_attn), BF16),
        jax.ShapeDtypeStruct((M, d_attn), BF16),
        jax.ShapeDtypeStruct((M, n_heads), F32),
        jax.ShapeDtypeStruct((M, d_attn), F32),
        jax.ShapeDtypeStruct((M, d_lru), F32),
        jax.ShapeDtypeStruct((M, d_lru), F32),
    )
    return pl.pallas_call(
        _in_proj_kernel,
        grid=(M // tm,),
        in_specs=[row(D)] + [_resident(w.shape) for w in (wq, wk, wv, wf, wga, wxl, wgl, bf)],
        out_specs=tuple(row(s.shape[1]) for s in out_shape),
        out_shape=out_shape,
        compiler_params=_params("parallel"),
        name="in_proj",
    )(x2d, wq, wk, wv, wf, wga, wxl, wgl, bf)


def _cumsum_lanes(x):
    n = x.shape[-1]
    lane = lax.broadcasted_iota(jnp.int32, x.shape, x.ndim - 1)
    step = 1
    while step < n:
        x = x + jnp.where(lane >= step, pltpu.roll(x, step, x.ndim - 1), 0.0)
        step *= 2
    return x


def _rows(ref):
    return ref[...].reshape(ref.shape[0] * ref.shape[1], ref.shape[2])


def _cumsum_kernel(f_ref, c_ref):
    c_ref[...] = (_cumsum_lanes(_rows(f_ref)) * LOG2E).reshape(c_ref.shape)


def _cumsum_carry_kernel(past_ref, new_ref, cpast_ref, cnew_ref):
    cp = _cumsum_lanes(_rows(past_ref))
    cpast_ref[...] = (cp * LOG2E).reshape(cpast_ref.shape)
    cn = (_cumsum_lanes(_rows(new_ref)) + cp[:, cp.shape[1] - 1:]) * LOG2E
    cnew_ref[...] = cn.reshape(cnew_ref.shape)


CUMSUM_STREAMS = 4


def _cumsum(fT):
    B, H, S = fT.shape
    nb = _tile(B, CUMSUM_STREAMS)
    spec = pl.BlockSpec((nb, H, S), lambda b: (b, 0, 0))
    return pl.pallas_call(
        _cumsum_kernel, grid=(B // nb,), in_specs=[spec], out_specs=spec,
        out_shape=jax.ShapeDtypeStruct(fT.shape, F32),
        compiler_params=_params("parallel"), name="cumsum",
    )(fT)


def _cumsum_carry(pastT, newT):
    B, H, P = pastT.shape
    N = newT.shape[2]
    nb = _tile(B, CUMSUM_STREAMS)
    pspec = pl.BlockSpec((nb, H, P), lambda b: (b, 0, 0))
    nspec = pl.BlockSpec((nb, H, N), lambda b: (b, 0, 0))
    return pl.pallas_call(
        _cumsum_carry_kernel, grid=(B // nb,), in_specs=[pspec, nspec], out_specs=(pspec, nspec),
        out_shape=(jax.ShapeDtypeStruct(pastT.shape, F32), jax.ShapeDtypeStruct(newT.shape, F32)),
        compiler_params=_params("parallel"), name="cumsum_carry",
    )(pastT, newT)


def _head_epilogue(acc, l, gain, gate):
    normed = acc * lax.rsqrt(jnp.mean(acc * acc, axis=-1, keepdims=True) + RMS_EPS * (l * l))
    return (normed * gain * gate).astype(BF16)


def _fox_prompt_kernel(q_ref, kb_ref, vp_ref, cT_ref, ga_ref, gattn_ref, ya_ref,
                       m_s, acc_s, *, tq):
    n_heads = cT_ref.shape[1]
    qi = pl.program_id(1)
    q0 = pl.multiple_of(qi * tq, tq)

    def values(h, k0, rows):
        return vp_ref[0, pl.ds(k0, rows), 2 * h * HEAD_DIM:(2 * h + 2) * HEAD_DIM]

    def scores(h, k0):
        hs = slice(h * HEAD_DIM, (h + 1) * HEAD_DIM)
        s = _dot_nt(q_ref[0, :, hs], kb_ref[0, pl.ds(k0, tq), hs])
        return s - cT_ref[0, h:h + 1, pl.ds(k0, tq)]

    def row_max(s):
        m = s[:, :LANES]
        for c in range(1, s.shape[1] // LANES):
            m = jnp.maximum(m, s[:, c * LANES:(c + 1) * LANES])
        return jnp.max(m, axis=-1, keepdims=True)

    def weights(s, m):
        parts = [jnp.exp2(s[:, c * LANES:(c + 1) * LANES] - m) for c in range(s.shape[1] // LANES)]
        return jnp.concatenate(parts, axis=1).astype(BF16)

    half = tq // 2
    visible_a = (lax.broadcasted_iota(jnp.int32, (tq, half), 1)
                 <= lax.broadcasted_iota(jnp.int32, (tq, half), 0))
    visible_b = (lax.broadcasted_iota(jnp.int32, (half, half), 1)
                 <= lax.broadcasted_iota(jnp.int32, (half, half), 0))

    def diag_scores(h):
        hs = slice(h * HEAD_DIM, (h + 1) * HEAD_DIM)
        sa = _dot_nt(q_ref[0, :, hs], kb_ref[0, pl.ds(q0, half), hs])
        sb = _dot_nt(q_ref[0, half:, hs], kb_ref[0, pl.ds(q0 + half, half), hs])
        return (sa - cT_ref[0, h:h + 1, pl.ds(q0, half)],
                sb - cT_ref[0, h:h + 1, pl.ds(q0 + half, half)])

    s_next = diag_scores(0)
    for h in range(n_heads):
        sa = jnp.where(visible_a, s_next[0], -jnp.inf)
        sb = jnp.where(visible_b, s_next[1], -jnp.inf)
        if h + 1 < n_heads:
            s_next = diag_scores(h + 1)
        s_top = sa[:half]
        s_bot = jnp.concatenate([sa[half:], sb], axis=1)
        m_top = jnp.broadcast_to(row_max(s_top), (half, LANES))
        m_bot = jnp.broadcast_to(row_max(s_bot), (half, LANES))
        m_s[h, :half, :] = m_top
        m_s[h, half:, :] = m_bot
        acc_s[h, :half, :] = _dot(weights(s_top, m_top), values(h, q0, half))
        acc_s[h, half:, :] = _dot(weights(s_bot, m_bot), values(h, q0, tq))

    def body(j, carry):
        k0 = pl.multiple_of(j * tq, tq)
        s_next = scores(0, k0)
        for h in range(n_heads):
            s = s_next
            if h + 1 < n_heads:
                s_next = scores(h + 1, k0)
            m_old = m_s[h]
            m_new = jnp.maximum(m_old, row_max(s))
            alpha = jnp.exp2(m_old - m_new)
            m_s[h] = m_new
            pv = _dot(weights(s, m_new), values(h, k0, tq))
            acc_s[h, :, :HEAD_DIM] = alpha * acc_s[h, :, :HEAD_DIM] + pv[:, :HEAD_DIM]
            acc_s[h, :, HEAD_DIM:] = alpha * acc_s[h, :, HEAD_DIM:] + pv[:, HEAD_DIM:]
        return carry

    lax.fori_loop(0, qi, body, 0)

    for h in range(n_heads):
        hs = slice(h * HEAD_DIM, (h + 1) * HEAD_DIM)
        ya_ref[0, :, hs] = _head_epilogue(acc_s[h, :, :HEAD_DIM], acc_s[h, :, HEAD_DIM:],
                                          gattn_ref[:, hs], ga_ref[0, :, hs])


def _fox_prompt(q, kb, vp, cT, ga, gattn, tq):
    B, S, DA = q.shape
    H = cT.shape[1]
    blk = pl.BlockSpec((1, tq, DA), lambda b, i: (b, i, 0))
    seq = lambda n: pl.BlockSpec((1, S, n), lambda b, i: (b, 0, 0))
    return pl.pallas_call(
        functools.partial(_fox_prompt_kernel, tq=tq),
        grid=(B, S // tq),
        in_specs=[blk, seq(DA), seq(vp.shape[2]), pl.BlockSpec((1, H, S), lambda b, i: (b, 0, 0)),
                  blk, _resident(gattn.shape)],
        out_specs=blk,
        out_shape=jax.ShapeDtypeStruct((B, S, DA), BF16),
        scratch_shapes=[pltpu.VMEM((H, tq, LANES), F32),
                        pltpu.VMEM((H, tq, 2 * HEAD_DIM), F32)],
        compiler_params=_params("parallel", "arbitrary"),
        name="fox_prompt",
    )(q, kb, vp, cT, ga, gattn)


def _fox_sample_kernel(q_ref, ck_hbm, cv_hbm, cpT_ref, kn_ref, vn_ref, cnT_ref, cnc_ref, ga_ref,
                       gattn_ref, ya_ref, kbuf, vbuf, sems, m_s, l_s, acc_s):
    n_heads = cpT_ref.shape[1]
    T = q_ref.shape[1]
    tk = cpT_ref.shape[2]
    b, j = pl.program_id(0), pl.program_id(1)
    nj = pl.num_programs(1)
    step = b * nj + j
    slot = lax.rem(step, 2)

    def block_copies(stream, blk, sl):
        rows = pl.ds(blk * tk, tk)
        copies = []
        for h in range(n_heads):
            copies.append(pltpu.make_async_copy(ck_hbm.at[stream, rows, h, :], kbuf.at[sl, h],
                                                sems.at[0, sl, h]))
            copies.append(pltpu.make_async_copy(cv_hbm.at[stream, rows, h, :], vbuf.at[sl, h],
                                                sems.at[1, sl, h]))
        return copies

    @pl.when(step == 0)
    def _():
        for c in block_copies(b, j, slot):
            c.start()

    @pl.when(step + 1 < pl.num_programs(0) * nj)
    def _():
        last = j + 1 == nj
        for c in block_copies(jnp.where(last, b + 1, b), jnp.where(last, 0, j + 1), 1 - slot):
            c.start()

    for c in block_copies(b, j, slot):
        c.wait()

    @pl.when(j == 0)
    def _():
        m_s[...] = jnp.full(m_s.shape, -jnp.inf, F32)
        l_s[...] = jnp.zeros(l_s.shape, F32)
        acc_s[...] = jnp.zeros(acc_s.shape, F32)

    def update(h, s, vals):
        m_old = m_s[h]
        m_new = jnp.maximum(m_old, jnp.max(s, axis=-1, keepdims=True))
        alpha = jnp.exp2(m_old - m_new)
        p = jnp.exp2(s - m_new)
        m_s[h] = m_new
        l_s[h] = alpha * l_s[h] + jnp.sum(p, axis=-1, keepdims=True)
        acc_s[h] = alpha * acc_s[h] + _dot(p.astype(BF16), vals)

    def cached_scores(h):
        hs = slice(h * HEAD_DIM, (h + 1) * HEAD_DIM)
        kh = kbuf[slot, h].astype(BF16)
        return _dot_nt(q_ref[0, :, hs], kh) + (cnc_ref[0, :, h:h + 1] - cpT_ref[0, h:h + 1, :])

    s_next = cached_scores(0)
    for h in range(n_heads):
        s = s_next
        if h + 1 < n_heads:
            s_next = cached_scores(h + 1)
        update(h, s, vbuf[slot, h].astype(BF16))

    @pl.when(j == pl.num_programs(1) - 1)
    def _():
        row = lax.broadcasted_iota(jnp.int32, (T, T), 0)
        col = lax.broadcasted_iota(jnp.int32, (T, T), 1)
        heads = [slice(h * HEAD_DIM, (h + 1) * HEAD_DIM) for h in range(n_heads)]
        scores = [_dot_nt(q_ref[0, :, hs], kn_ref[0, :, hs])
                  + (cnc_ref[0, :, h:h + 1] - cnT_ref[0, h:h + 1, :T]) for h, hs in enumerate(heads)]
        for h, hs in enumerate(heads):
            update(h, jnp.where(col <= row, scores[h], -jnp.inf), vn_ref[0, :, hs])
        for h, hs in enumerate(heads):
            ya_ref[0, :, hs] = _head_epilogue(acc_s[h], l_s[h], gattn_ref[:, hs], ga_ref[0, :, hs])


def _fox_sample(q, ck, cv, cpT, kn, vn, cnT, cnc, ga, gattn, tk):
    B, T, DA = q.shape
    H = cpT.shape[1]
    P = ck.shape[1]
    new = pl.BlockSpec((1, T, DA), lambda b, j: (b, 0, 0))
    hbm = pl.BlockSpec(memory_space=pl.ANY)
    return pl.pallas_call(
        _fox_sample_kernel,
        grid=(B, P // tk),
        in_specs=[new, hbm, hbm,
                  pl.BlockSpec((1, H, tk), lambda b, j: (b, 0, j)),
                  new, new,
                  pl.BlockSpec((1, H, cnT.shape[2]), lambda b, j: (b, 0, 0)),
                  pl.BlockSpec((1, T, H), lambda b, j: (b, 0, 0)),
                  new, _resident(gattn.shape)],
        out_specs=new,
        out_shape=jax.ShapeDtypeStruct((B, T, DA), BF16),
        scratch_shapes=[pltpu.VMEM((2, H, tk, HEAD_DIM), ck.dtype),
                        pltpu.VMEM((2, H, tk, HEAD_DIM), cv.dtype),
                        pltpu.SemaphoreType.DMA((2, 2, H)),
                        pltpu.VMEM((H, T, 1), F32), pltpu.VMEM((H, T, 1), F32),
                        pltpu.VMEM((H, T, HEAD_DIM), F32)],
        compiler_params=_params("arbitrary", "arbitrary"),
        name="fox_sample",
    )(q, ck, cv, cpT, kn, vn, cnT, cnc, ga, gattn)


def _segment_pitch(seg_len):
    pitch = -(-seg_len // SUBLANES) * SUBLANES
    return pitch if (pitch // SUBLANES) % 2 else pitch + SUBLANES


def _rglru_init(t, hist_ref, h0_ref, tail_s, hcar_s):
    @pl.when(t == 0)
    def _():
        tail_s[...] = jnp.zeros(tail_s.shape, F32)
        tail_s[SUBLANES - (CONV_W - 1):, :] = hist_ref[...]
        hcar_s[...] = h0_ref[...]


def _rglru_block(t, xl_ref, gl_ref, cw_ref, cb_ref, wr_ref, br_ref, wi_ref, bi_ref,
                 lam_ref, glru_ref, yl_ref, hT_ref, hist_out_ref, xn_s, hn_s, hl_s, ac_s, tail_s, hcar_s,
                 *, tt, reset_first):
    n_blocks = wr_ref.shape[0]
    seg = tt // SUBLANES
    pitch = _segment_pitch(seg)
    sub = lax.broadcasted_iota(jnp.int32, (SUBLANES, LRU_BLOCK), 0)
    first = (lax.broadcasted_iota(jnp.int32, (tt, LRU_BLOCK), 0) == 0) & (t == 0)
    decay = -LRU_C * _softplus(-lam_ref[...])

    def shift_segments(v, head):
        return jnp.where(sub == 0, head, pltpu.roll(v, 1, 0))

    gated = []
    for n in range(n_blocks):
        ns = slice(n * LRU_BLOCK, (n + 1) * LRU_BLOCK)
        for s in range(SUBLANES):
            xn_s[n, s * pitch:s * pitch + seg, :] = xl_ref[s * seg:(s + 1) * seg, ns]
        x = [xn_s[n, pl.ds(i, SUBLANES, stride=pitch), :] for i in range(seg)]
        before = [shift_segments(x[seg - k], tail_s[SUBLANES - k:SUBLANES - k + 1, ns])
                  for k in range(CONV_W - 1, 0, -1)]
        xs = jnp.concatenate(before + x, axis=0)
        xc = cb_ref[:, ns] + xs[0:tt] * cw_ref[0:1, ns]
        for j in range(1, CONV_W):
            xc = xc + xs[j * SUBLANES:j * SUBLANES + tt] * cw_ref[j:j + 1, ns]

        xcb = xc.astype(BF16)
        gated.append((xc, _dot(xcb, wr_ref[n]), _dot(xcb, wi_ref[n])))

    tail_s[...] = xl_ref[tt - SUBLANES:, :]
    hist_out_ref[...] = xl_ref[tt - (CONV_W - 1):, :]
    _rglru_recurrence(gated, sub, first, decay, shift_segments, br_ref, bi_ref, glru_ref, gl_ref,
                      yl_ref, hT_ref, hn_s, hl_s, ac_s, hcar_s, seg=seg, pitch=pitch,
                      reset_first=reset_first)


def _rglru_recurrence(gated, sub, first, decay, shift_segments, br_ref, bi_ref, glru_ref, gl_ref,
                      yl_ref, hT_ref, hn_s, hl_s, ac_s, hcar_s, *, seg, pitch, reset_first):
    for n, (xc, zr, zi) in enumerate(gated):
        ns = slice(n * LRU_BLOCK, (n + 1) * LRU_BLOCK)
        r = _sigmoid(zr + br_ref[:, ns])
        i_gate = _sigmoid(zi + bi_ref[:, ns])
        log_a = r * decay[:, ns]
        a = jnp.exp(log_a)
        one_minus_a2 = -jnp.tanh(log_a) * (1.0 + a * a)
        mult = jnp.where(one_minus_a2 > 0.0, one_minus_a2 * lax.rsqrt(one_minus_a2), 0.0)
        if reset_first:
            mult = jnp.where(first, 1.0, mult)
        u = mult * i_gate * xc

        rows = lambda v, i: v[i * SUBLANES:(i + 1) * SUBLANES]
        e, p = rows(u, 0), rows(a, 0)
        hl_s[n, 0:SUBLANES, :] = e
        ac_s[n, 0:SUBLANES, :] = p
        for i in range(1, seg):
            e = rows(a, i) * e + rows(u, i)
            p = rows(a, i) * p
            hl_s[n, i * SUBLANES:(i + 1) * SUBLANES, :] = e
            ac_s[n, i * SUBLANES:(i + 1) * SUBLANES, :] = p
        for step in (1, 2, 4):
            keep = sub >= step
            e = e + p * jnp.where(keep, pltpu.roll(e, step, 0), 0.0)
            p = p * jnp.where(keep, pltpu.roll(p, step, 0), 1.0)
        h_end = e + p * hcar_s[:, ns]
        carry_in = shift_segments(h_end, hcar_s[:, ns])
        hcar_s[:, ns] = h_end[SUBLANES - 1:, :]
        hT_ref[:, ns] = h_end[SUBLANES - 1:, :]

        for i in range(seg):
            hn_s[n, pl.ds(i, SUBLANES, stride=pitch), :] = (
                hl_s[n, i * SUBLANES:(i + 1) * SUBLANES, :]
                + ac_s[n, i * SUBLANES:(i + 1) * SUBLANES, :] * carry_in)
        h = jnp.concatenate([hn_s[n, s * pitch:s * pitch + seg, :] for s in range(SUBLANES)], axis=0)
        y = _group_rms(h) * glru_ref[:, ns] * gl_ref[:, ns]
        yl_ref[:, ns] = y.astype(BF16)


def _rglru_kernel(xl_ref, gl_ref, hist_ref, h0_ref, *rest, tt, reset_first):
    weights, (yl_ref, hT_ref, hist_out_ref), scratch = rest[:8], rest[8:11], rest[11:]
    xn_s, hn_s, hl_s, ac_s, tail_s, hcar_s = scratch
    t = pl.program_id(1)
    for g in range(xl_ref.shape[0]):
        _rglru_init(t, hist_ref.at[g], h0_ref.at[g], tail_s.at[g], hcar_s.at[g])
        _rglru_block(t, xl_ref.at[g], gl_ref.at[g], *weights, yl_ref.at[g], hT_ref.at[g],
                     hist_out_ref.at[g], xn_s, hn_s, hl_s, ac_s, tail_s.at[g], hcar_s.at[g],
                     tt=tt, reset_first=reset_first)


def _in_proj_rglru_kernel(x_ref, wq_ref, wk_ref, wv_ref, wf_ref, wga_ref, wxl_ref, wgl_ref, bf_ref,
                          hist_ref, h0_ref, *rest, tt, reset_first):
    lru_w = rest[:8]
    (q_ref, k_ref, v_ref, kb_ref, vp_ref, ga_ref, yl_ref, hT_ref, hist_out_ref,
     logf_t_ref) = rest[8:18]
    xl_s, gl_s, *lru_scratch = rest[18:]
    n_heads = logf_t_ref.shape[1]
    t = pl.program_id(1)
    _rglru_init(t, hist_ref.at[0], h0_ref.at[0], *lru_scratch[4:])
    xb = x_ref[0].astype(BF16)
    xl_s[...] = _dot(xb, wxl_ref[...])
    gl_s[...] = _silu(_dot(xb, wgl_ref[...]))
    _rglru_block(t, xl_s, gl_s, *lru_w, yl_ref.at[0], hT_ref.at[0], hist_out_ref.at[0],
                 *lru_scratch, tt=tt, reset_first=reset_first)
    q_ref[0] = (_dot(xb, wq_ref[...]) * Q_SCALE).astype(BF16)
    k = _dot(xb, wk_ref[...])
    k_ref[0] = k
    kb_ref[0] = k.astype(BF16)
    v = _dot(xb, wv_ref[...])
    v_ref[0] = v
    ones = jnp.ones((tt, HEAD_DIM), BF16)
    for h in range(n_heads):
        vp_ref[0, :, 2 * h * HEAD_DIM:(2 * h + 1) * HEAD_DIM] = (
            v[:, h * HEAD_DIM:(h + 1) * HEAD_DIM].astype(BF16))
        vp_ref[0, :, (2 * h + 1) * HEAD_DIM:(2 * h + 2) * HEAD_DIM] = ones
    logf = _log_sigmoid(_dot(xb, wf_ref[...]) + bf_ref[...])
    logf_t_ref[0] = logf.T[:n_heads, :]
    ga_ref[0] = _silu(_dot(xb, wga_ref[...]))


def _in_proj_rglru(x, wts, hist, h0, lw, tt, reset_first):
    B, S, D = x.shape
    wq, wk, wv, wf, wga, wxl, wgl, bf, n_heads = wts
    d_attn, d_lru = wq.shape[1], wxl.shape[1]
    blk = lambda n: pl.BlockSpec((1, tt, n), lambda b, t: (b, t, 0))
    per_b = lambda n: pl.BlockSpec((1, n, d_lru), lambda b, t: (b, 0, 0))
    slab_rows = SUBLANES * _segment_pitch(tt // SUBLANES)
    seq = lambda n, dt: jax.ShapeDtypeStruct((B, S, n), dt)
    out_shape = (seq(d_attn, BF16), seq(d_attn, F32), seq(d_attn, F32), seq(d_attn, BF16),
                 seq(2 * d_attn, BF16), seq(d_attn, F32), seq(d_lru, BF16),
                 jax.ShapeDtypeStruct((B, 1, d_lru), F32),
                 jax.ShapeDtypeStruct((B, CONV_W - 1, d_lru), F32),
                 jax.ShapeDtypeStruct((B, n_heads, S), F32))
    return pl.pallas_call(
        functools.partial(_in_proj_rglru_kernel, tt=tt, reset_first=reset_first),
        grid=(B, S // tt),
        in_specs=[blk(D)] + [_resident(w.shape) for w in (wq, wk, wv, wf, wga, wxl, wgl, bf)]
                 + [per_b(CONV_W - 1), per_b(1)] + [_resident(w.shape) for w in lw],
        out_specs=tuple(blk(s.shape[2]) for s in out_shape[:7]) + (per_b(1), per_b(CONV_W - 1))
                  + (pl.BlockSpec((1, n_heads, tt), lambda b, t: (b, 0, t)),),
        out_shape=out_shape,
        scratch_shapes=[pltpu.VMEM((tt, d_lru), F32), pltpu.VMEM((tt, d_lru), F32),
                        pltpu.VMEM((lw[2].shape[0], slab_rows, LRU_BLOCK), F32),
                        pltpu.VMEM((lw[2].shape[0], slab_rows, LRU_BLOCK), F32),
                        pltpu.VMEM((lw[2].shape[0], tt, LRU_BLOCK), F32),
                        pltpu.VMEM((lw[2].shape[0], tt, LRU_BLOCK), F32),
                        pltpu.VMEM((SUBLANES, d_lru), F32), pltpu.VMEM((1, d_lru), F32)],
        compiler_params=_params("parallel", "arbitrary"),
        name="in_proj_rglru",
    )(x, wq, wk, wv, wf, wga, wxl, wgl, bf, hist, h0, *lw)


RGLRU_STREAMS = 4


def _rglru(xl, gl, hist, h0, lw, tt, reset_first):
    B, T, DL = xl.shape
    cw, cb, wr, br, wi, bi, lam, glru = lw
    G = _tile(B, RGLRU_STREAMS)
    blk = pl.BlockSpec((G, tt, DL), lambda b, t: (b, t, 0))
    per_b = lambda n: pl.BlockSpec((G, n, DL), lambda b, t: (b, 0, 0))
    slab_rows = SUBLANES * _segment_pitch(tt // SUBLANES)
    return pl.pallas_call(
        functools.partial(_rglru_kernel, tt=tt, reset_first=reset_first),
        grid=(B // G, T // tt),
        in_specs=[blk, blk, per_b(CONV_W - 1), per_b(1)]
                 + [_resident(w.shape) for w in (cw, cb, wr, br, wi, bi, lam, glru)],
        out_specs=(blk, per_b(1), per_b(CONV_W - 1)),
        out_shape=(jax.ShapeDtypeStruct((B, T, DL), BF16),
                   jax.ShapeDtypeStruct((B, 1, DL), F32),
                   jax.ShapeDtypeStruct((B, CONV_W - 1, DL), F32)),
        scratch_shapes=[pltpu.VMEM((wr.shape[0], slab_rows, LRU_BLOCK), F32),
                        pltpu.VMEM((wr.shape[0], slab_rows, LRU_BLOCK), F32),
                        pltpu.VMEM((wr.shape[0], tt, LRU_BLOCK), F32),
                        pltpu.VMEM((wr.shape[0], tt, LRU_BLOCK), F32),
                        pltpu.VMEM((G, SUBLANES, DL), F32), pltpu.VMEM((G, 1, DL), F32)],
        compiler_params=_params("parallel", "arbitrary"),
        name="rglru",
    )(xl, gl, hist, h0, cw, cb, wr, br, wi, bi, lam, glru)


def _out_proj_kernel(x_ref, ya_ref, yl_ref, wa_ref, wl_ref, g_ref, b_ref, o_ref, *, alpha):
    tm = x_ref.shape[0]
    rows = [slice(r, r + OUT_PROJ_ROWS) for r in range(0, tm, OUT_PROJ_ROWS)]
    outs = [_dot(ya_ref[r, :], wa_ref[...]) + _dot(yl_ref[r, :], wl_ref[...]) for r in rows]
    for r, out in zip(rows, outs):
        h = alpha * x_ref[r, :] + out
        mu = jnp.mean(h, axis=-1, keepdims=True)
        d = h - mu
        var = jnp.mean(d * d, axis=-1, keepdims=True)
        o_ref[r, :] = d * lax.rsqrt(var + LN_EPS) * g_ref[...] + b_ref[...]


def _out_proj(x2d, ya, yl, wa, wl, g, b, alpha, tm):
    M, D = x2d.shape
    row = lambda n: pl.BlockSpec((tm, n), lambda i: (i, 0))
    return pl.pallas_call(
        functools.partial(_out_proj_kernel, alpha=alpha),
        grid=(M // tm,),
        in_specs=[row(D), row(ya.shape[1]), row(yl.shape[1])]
                 + [_resident(w.shape) for w in (wa, wl, g, b)],
        out_specs=row(D),
        out_shape=jax.ShapeDtypeStruct((M, D), F32),
        compiler_params=_params("parallel"),
        name="out_proj",
    )(x2d, ya, yl, wa, wl, g, b)


def _split_w_out_kernel(w_ref, wa_ref, wl_ref):
    wa_ref[...] = w_ref[0, 0].astype(BF16)
    wl_ref[...] = w_ref[0, 1].astype(BF16)


def _split_w_out(w_out, layer, d_attn, rows):
    depth, DM, D = w_out.shape
    halves = w_out.reshape(depth, 2, d_attn, D)
    spec = pl.BlockSpec((rows, D), lambda i: (i, 0))
    return pl.pallas_call(
        _split_w_out_kernel,
        grid=(d_attn // rows,),
        in_specs=[pl.BlockSpec((1, 2, rows, D), lambda i: (layer, 0, i, 0))],
        out_specs=(spec, spec),
        out_shape=(jax.ShapeDtypeStruct((d_attn, D), BF16),) * 2,
        compiler_params=_params("parallel"),
        name="split_w_out",
    )(halves)


def _tile(n, pref):
    return pref if n % pref == 0 else n


def kernel(x_prompt, x_sample, cache_k, cache_v, cache_logf, state_h, state_conv, w_in, b_f, conv_w,
           conv_b, w_r, b_r, w_i, b_i, lru_lambda, g_attn, g_lru, w_out, ln_g, ln_b):
    depth, d_model, _ = w_in.shape
    n_heads = b_f.shape[1]
    d_attn = n_heads * HEAD_DIM
    d_lru = lru_lambda.shape[1]
    alpha = (2.0 * depth) ** 0.25
    B, S, _ = x_prompt.shape
    DB, T, _ = x_sample.shape
    P = cache_k.shape[2]

    xp = x_prompt.reshape(B * S, d_model)
    xs = x_sample.reshape(DB * T, d_model)
    outs_p, outs_s = [], []
    for l in range(depth):
        w = w_in[l]
        o = (d_attn, 2 * d_attn, 3 * d_attn, 3 * d_attn + n_heads, 4 * d_attn + n_heads,
             4 * d_attn + n_heads + d_lru)
        wf = jnp.pad(w[:, o[2]:o[3]], ((0, 0), (0, LANES - n_heads)))
        bf = jnp.pad(b_f[l][None, :], ((0, 0), (0, LANES - n_heads)))
        wts = (w[:, :o[0]].astype(BF16), w[:, o[0]:o[1]].astype(BF16), w[:, o[1]:o[2]].astype(BF16),
               wf.astype(BF16), w[:, o[3]:o[4]].astype(BF16), w[:, o[4]:o[5]].astype(BF16),
               w[:, o[5]:].astype(BF16), bf, n_heads)
        lw = (conv_w[l], conv_b[l][None], w_r[l].astype(BF16), b_r[l][None], w_i[l].astype(BF16),
              b_i[l][None], lru_lambda[l][None], g_lru[l][None])
        gattn = g_attn[l][None]
        assert d_lru == d_attn, "w_out is split into two equal row halves"
        wa, wl = _split_w_out(w_out, l, d_attn, _tile(d_attn, 128))
        lng, lnb = ln_g[l][None], ln_b[l][None]

        q, k, v, kb, vp, ga, yl, h_T, hist_T, logf_t = _in_proj_rglru(
            xp.reshape(B, S, d_model), wts, jnp.zeros((B, CONV_W - 1, d_lru), F32),
            jnp.zeros((B, 1, d_lru), F32), lw, _tile(S, 256), True)
        cT = _cumsum(logf_t)
        ya = _fox_prompt(q, kb, vp, cT, ga, gattn, _tile(S, 512))
        outs_p.append((k.reshape(B, S, n_heads, HEAD_DIM), v.reshape(B, S, n_heads, HEAD_DIM),
                       jnp.transpose(logf_t, (0, 2, 1)), h_T.reshape(B, d_lru), hist_T))
        xp = _out_proj(xp, ya.reshape(B * S, d_attn), yl.reshape(B * S, d_lru), wa, wl, lng, lnb,
                       alpha, _tile(B * S, 512))

        q, k, v, kb, vb, logf, ga, xl, gl = _in_proj(xs, wts, _tile(DB * T, 256))
        r3 = lambda a: a.reshape(DB, T, a.shape[-1])
        newT = jnp.pad(jnp.transpose(r3(logf), (0, 2, 1)), ((0, 0), (0, 0), (0, LANES - T)))
        cpT, cnT = _cumsum_carry(jnp.transpose(cache_logf[l].astype(F32), (0, 2, 1)), newT)
        cnc = jnp.transpose(cnT[:, :, :T], (0, 2, 1))
        ya = _fox_sample(r3(q), cache_k[l], cache_v[l],
                         cpT, r3(kb), r3(vb), cnT, cnc, r3(ga), gattn, _tile(P, 2048))
        yl, h_T, hist_T = _rglru(r3(xl), r3(gl), state_conv[l], state_h[l][:, None, :], lw,
                                 _tile(T, 256), False)
        outs_s.append((k.reshape(DB, T, n_heads, HEAD_DIM), v.reshape(DB, T, n_heads, HEAD_DIM),
                       r3(logf), h_T.reshape(DB, d_lru), hist_T))
        xs = _out_proj(xs, ya.reshape(DB * T, d_attn), yl.reshape(DB * T, d_lru), wa, wl, lng, lnb,
                       alpha, _tile(DB * T, 256))

    stack = lambda outs, i: jnp.stack([o[i] for o in outs], 0)
    return (xp.reshape(B, S, d_model), xs.reshape(DB, T, d_model),
            stack(outs_p, 0), stack(outs_p, 1), stack(outs_p, 2), stack(outs_p, 3), stack(outs_p, 4),
            stack(outs_s, 0), stack(outs_s, 1), stack(outs_s, 2), stack(outs_s, 3), stack(outs_s, 4))
```

```python
import functools

import jax
import jax.numpy as jnp
from jax import lax
from jax.experimental import pallas as pl
from jax.experimental.pallas import tpu as pltpu

F32 = jnp.float32
BF16 = jnp.bfloat16

HEAD_DIM = 128
LRU_BLOCK = 128
CONV_W = 4
LRU_C = 8.0
LN_EPS = 1e-5
RMS_EPS = 1e-6
LOG2E = 1.4426950408889634
Q_SCALE = HEAD_DIM ** -0.5 * LOG2E

LANES = 128
SUBLANES = 8
V7X_VMEM_LIMIT = 56 * 2 ** 20
OUT_PROJ_ROWS = 128


def _dot(a, b):
    return jnp.dot(a, b, preferred_element_type=F32)


def _dot_nt(a, b):
    return lax.dot_general(a, b, (((1,), (1,)), ((), ())), preferred_element_type=F32)


def _softplus(y):
    return jnp.maximum(y, 0.0) + jnp.log1p(jnp.exp(-jnp.abs(y)))


def _log_sigmoid(y):
    return -_softplus(-y)


def _sigmoid(y):
    return 1.0 / (1.0 + jnp.exp2(y * -LOG2E))


def _silu(y):
    return y * _sigmoid(y)


def _group_rms(y):
    return y * lax.rsqrt(jnp.mean(y * y, axis=-1, keepdims=True) + RMS_EPS)


def _params(*semantics):
    return pltpu.CompilerParams(dimension_semantics=semantics, vmem_limit_bytes=V7X_VMEM_LIMIT)


def _resident(shape):
    return pl.BlockSpec(shape, lambda *_: (0,) * len(shape), pipeline_mode=pl.Buffered(1))


def _in_proj_kernel(x_ref, wq_ref, wk_ref, wv_ref, wf_ref, wga_ref, wxl_ref, wgl_ref, bf_ref,
                    q_ref, k_ref, v_ref, kb_ref, vb_ref, logf_ref, ga_ref, xl_ref, gl_ref):
    n_heads = logf_ref.shape[-1]
    xb = x_ref[...].astype(BF16)
    q_ref[...] = (_dot(xb, wq_ref[...]) * Q_SCALE).astype(BF16)
    k = _dot(xb, wk_ref[...])
    k_ref[...] = k
    kb_ref[...] = k.astype(BF16)
    v = _dot(xb, wv_ref[...])
    v_ref[...] = v
    vb_ref[...] = v.astype(BF16)
    zf = _dot(xb, wf_ref[...]) + bf_ref[...]
    logf_ref[...] = _log_sigmoid(zf)[:, :n_heads]
    ga_ref[...] = _silu(_dot(xb, wga_ref[...]))
    xl_ref[...] = _dot(xb, wxl_ref[...])
    gl_ref[...] = _silu(_dot(xb, wgl_ref[...]))


def _in_proj(x2d, wts, tm):
    M, D = x2d.shape
    wq, wk, wv, wf, wga, wxl, wgl, bf, n_heads = wts
    d_attn, d_lru = wq.shape[1], wxl.shape[1]
    row = lambda n: pl.BlockSpec((tm, n), lambda i: (i, 0))
    out_shape = (
        jax.ShapeDtypeStruct((M, d_attn), BF16),
        jax.ShapeDtypeStruct((M, d_attn), F32),
        jax.ShapeDtypeStruct((M, d_attn), F32),
        jax.ShapeDtypeStruct((M, d_attn), BF16),
        jax.ShapeDtypeStruct((M, d_attn), BF16),
        jax.ShapeDtypeStruct((M, n_heads), F32),
        jax.ShapeDtypeStruct((M, d_attn), F32),
        jax.ShapeDtypeStruct((M, d_lru), F32),
        jax.ShapeDtypeStruct((M, d_lru), F32),
    )
    return pl.pallas_call(
        _in_proj_kernel,
        grid=(M // tm,),
        in_specs=[row(D)] + [_resident(w.shape) for w in (wq, wk, wv, wf, wga, wxl, wgl, bf)],
        out_specs=tuple(row(s.shape[1]) for s in out_shape),
        out_shape=out_shape,
        compiler_params=_params("parallel"),
        name="in_proj",
    )(x2d, wq, wk, wv, wf, wga, wxl, wgl, bf)


def _cumsum_lanes(x):
    n = x.shape[-1]
    lane = lax.broadcasted_iota(jnp.int32, x.shape, x.ndim - 1)
    step = 1
    while step < n:
        x = x + jnp.where(lane >= step, pltpu.roll(x, step, x.ndim - 1), 0.0)
        step *= 2
    return x


def _rows(ref):
    return ref[...].reshape(ref.shape[0] * ref.shape[1], ref.shape[2])


def _cumsum_kernel(f_ref, c_ref):
    c_ref[...] = (_cumsum_lanes(_rows(f_ref)) * LOG2E).reshape(c_ref.shape)


def _cumsum_carry_kernel(past_ref, new_ref, cpast_ref, cnew_ref):
    cp = _cumsum_lanes(_rows(past_ref))
    cpast_ref[...] = (cp * LOG2E).reshape(cpast_ref.shape)
    cn = (_cumsum_lanes(_rows(new_ref)) + cp[:, cp.shape[1] - 1:]) * LOG2E
    cnew_ref[...] = cn.reshape(cnew_ref.shape)


CUMSUM_STREAMS = 4


def _cumsum(fT):
    B, H, S = fT.shape
    nb = _tile(B, CUMSUM_STREAMS)
    spec = pl.BlockSpec((nb, H, S), lambda b: (b, 0, 0))
    return pl.pallas_call(
        _cumsum_kernel, grid=(B // nb,), in_specs=[spec], out_specs=spec,
        out_shape=jax.ShapeDtypeStruct(fT.shape, F32),
        compiler_params=_params("parallel"), name="cumsum",
    )(fT)


def _cumsum_carry(pastT, newT):
    B, H, P = pastT.shape
    N = newT.shape[2]
    nb = _tile(B, CUMSUM_STREAMS)
    pspec = pl.BlockSpec((nb, H, P), lambda b: (b, 0, 0))
    nspec = pl.BlockSpec((nb, H, N), lambda b: (b, 0, 0))
    return pl.pallas_call(
        _cumsum_carry_kernel, grid=(B // nb,), in_specs=[pspec, nspec], out_specs=(pspec, nspec),
        out_shape=(jax.ShapeDtypeStruct(pastT.shape, F32), jax.ShapeDtypeStruct(newT.shape, F32)),
        compiler_params=_params("parallel"), name="cumsum_carry",
    )(pastT, newT)


def _head_epilogue(acc, l, gain, gate):
    normed = acc * lax.rsqrt(jnp.mean(acc * acc, axis=-1, keepdims=True) + RMS_EPS * (l * l))
    return (normed * gain * gate).astype(BF16)


def _fox_prompt_kernel(q_ref, kb_ref, vp_ref, cT_ref, ga_ref, gattn_ref, ya_ref,
                       m_s, acc_s, *, tq):
    n_heads = cT_ref.shape[1]
    qi = pl.program_id(1)
    q0 = pl.multiple_of(qi * tq, tq)

    def values(h, k0, rows):
        return vp_ref[0, pl.ds(k0, rows), 2 * h * HEAD_DIM:(2 * h + 2) * HEAD_DIM]

    def scores(h, k0):
        hs = slice(h * HEAD_DIM, (h + 1) * HEAD_DIM)
        s = _dot_nt(q_ref[0, :, hs], kb_ref[0, pl.ds(k0, tq), hs])
        return s - cT_ref[0, h:h + 1, pl.ds(k0, tq)]

    def row_max(s):
        m = s[:, :LANES]
        for c in range(1, s.shape[1] // LANES):
            m = jnp.maximum(m, s[:, c * LANES:(c + 1) * LANES])
        return jnp.max(m, axis=-1, keepdims=True)

    def weights(s, m):
        parts = [jnp.exp2(s[:, c * LANES:(c + 1) * LANES] - m) for c in range(s.shape[1] // LANES)]
        return jnp.concatenate(parts, axis=1).astype(BF16)

    half = tq // 2
    visible_a = (lax.broadcasted_iota(jnp.int32, (tq, half), 1)
                 <= lax.broadcasted_iota(jnp.int32, (tq, half), 0))
    visible_b = (lax.broadcasted_iota(jnp.int32, (half, half), 1)
                 <= lax.broadcasted_iota(jnp.int32, (half, half), 0))

    def diag_scores(h):
        hs = slice(h * HEAD_DIM, (h + 1) * HEAD_DIM)
        sa = _dot_nt(q_ref[0, :, hs], kb_ref[0, pl.ds(q0, half), hs])
        sb = _dot_nt(q_ref[0, half:, hs], kb_ref[0, pl.ds(q0 + half, half), hs])
        return (sa - cT_ref[0, h:h + 1, pl.ds(q0, half)],
                sb - cT_ref[0, h:h + 1, pl.ds(q0 + half, half)])

    s_next = diag_scores(0)
    for h in range(n_heads):
        sa = jnp.where(visible_a, s_next[0], -jnp.inf)
        sb = jnp.where(visible_b, s_next[1], -jnp.inf)
        if h + 1 < n_heads:
            s_next = diag_scores(h + 1)
        s_top = sa[:half]
        s_bot = jnp.concatenate([sa[half:], sb], axis=1)
        m_top = jnp.broadcast_to(row_max(s_top), (half, LANES))
        m_bot = jnp.broadcast_to(row_max(s_bot), (half, LANES))
        m_s[h, :half, :] = m_top
        m_s[h, half:, :] = m_bot
        acc_s[h, :half, :] = _dot(weights(s_top, m_top), values(h, q0, half))
        acc_s[h, half:, :] = _dot(weights(s_bot, m_bot), values(h, q0, tq))

    def body(j, carry):
        k0 = pl.multiple_of(j * tq, tq)
        s_next = scores(0, k0)
        for h in range(n_heads):
            s = s_next
            if h + 1 < n_heads:
                s_next = scores(h + 1, k0)
            m_old = m_s[h]
            m_new = jnp.maximum(m_old, row_max(s))
            alpha = jnp.exp2(m_old - m_new)
            m_s[h] = m_new
            pv = _dot(weights(s, m_new), values(h, k0, tq))
            acc_s[h, :, :HEAD_DIM] = alpha * acc_s[h, :, :HEAD_DIM] + pv[:, :HEAD_DIM]
            acc_s[h, :, HEAD_DIM:] = alpha * acc_s[h, :, HEAD_DIM:] + pv[:, HEAD_DIM:]
        return carry

    lax.fori_loop(0, qi, body, 0)

    for h in range(n_heads):
        hs = slice(h * HEAD_DIM, (h + 1) * HEAD_DIM)
        ya_ref[0, :, hs] = _head_epilogue(acc_s[h, :, :HEAD_DIM], acc_s[h, :, HEAD_DIM:],
                                          gattn_ref[:, hs], ga_ref[0, :, hs])


def _fox_prompt(q, kb, vp, cT, ga, gattn, tq):
    B, S, DA = q.shape
    H = cT.shape[1]
    blk = pl.BlockSpec((1, tq, DA), lambda b, i: (b, i, 0))
    seq = lambda n: pl.BlockSpec((1, S, n), lambda b, i: (b, 0, 0))
    return pl.pallas_call(
        functools.partial(_fox_prompt_kernel, tq=tq),
        grid=(B, S // tq),
        in_specs=[blk, seq(DA), seq(vp.shape[2]), pl.BlockSpec((1, H, S), lambda b, i: (b, 0, 0)),
                  blk, _resident(gattn.shape)],
        out_specs=blk,
        out_shape=jax.ShapeDtypeStruct((B, S, DA), BF16),
        scratch_shapes=[pltpu.VMEM((H, tq, LANES), F32),
                        pltpu.VMEM((H, tq, 2 * HEAD_DIM), F32)],
        compiler_params=_params("parallel", "arbitrary"),
        name="fox_prompt",
    )(q, kb, vp, cT, ga, gattn)


def _fox_sample_kernel(q_ref, ck_hbm, cv_hbm, cpT_ref, kn_ref, vn_ref, cnT_ref, cnc_ref, ga_ref,
                       gattn_ref, ya_ref, kbuf, vbuf, sems, m_s, l_s, acc_s):
    n_heads = cpT_ref.shape[1]
    T = q_ref.shape[1]
    tk = cpT_ref.shape[2]
    b, j = pl.program_id(0), pl.program_id(1)
    nj = pl.num_programs(1)
    step = b * nj + j
    slot = lax.rem(step, 2)

    def block_copies(stream, blk, sl):
        rows = pl.ds(blk * tk, tk)
        copies = []
        for h in range(n_heads):
            copies.append(pltpu.make_async_copy(ck_hbm.at[stream, rows, h, :], kbuf.at[sl, h],
                                                sems.at[0, sl, h]))
            copies.append(pltpu.make_async_copy(cv_hbm.at[stream, rows, h, :], vbuf.at[sl, h],
                                                sems.at[1, sl, h]))
        return copies

    @pl.when(step == 0)
    def _():
        for c in block_copies(b, j, slot):
            c.start()

    @pl.when(step + 1 < pl.num_programs(0) * nj)
    def _():
        last = j + 1 == nj
        for c in block_copies(jnp.where(last, b + 1, b), jnp.where(last, 0, j + 1), 1 - slot):
            c.start()

    for c in block_copies(b, j, slot):
        c.wait()

    @pl.when(j == 0)
    def _():
        m_s[...] = jnp.full(m_s.shape, -jnp.inf, F32)
        l_s[...] = jnp.zeros(l_s.shape, F32)
        acc_s[...] = jnp.zeros(acc_s.shape, F32)

    def update(h, s, vals):
        m_old = m_s[h]
        m_new = jnp.maximum(m_old, jnp.max(s, axis=-1, keepdims=True))
        alpha = jnp.exp2(m_old - m_new)
        p = jnp.exp2(s - m_new)
        m_s[h] = m_new
        l_s[h] = alpha * l_s[h] + jnp.sum(p, axis=-1, keepdims=True)
        acc_s[h] = alpha * acc_s[h] + _dot(p.astype(BF16), vals)

    def cached_scores(h):
        hs = slice(h * HEAD_DIM, (h + 1) * HEAD_DIM)
        kh = kbuf[slot, h].astype(BF16)
        return _dot_nt(q_ref[0, :, hs], kh) + (cnc_ref[0, :, h:h + 1] - cpT_ref[0, h:h + 1, :])

    s_next = cached_scores(0)
    for h in range(n_heads):
        s = s_next
        if h + 1 < n_heads:
            s_next = cached_scores(h + 1)
        update(h, s, vbuf[slot, h].astype(BF16))

    @pl.when(j == pl.num_programs(1) - 1)
    def _():
        row = lax.broadcasted_iota(jnp.int32, (T, T), 0)
        col = lax.broadcasted_iota(jnp.int32, (T, T), 1)
        heads = [slice(h * HEAD_DIM, (h + 1) * HEAD_DIM) for h in range(n_heads)]
        scores = [_dot_nt(q_ref[0, :, hs], kn_ref[0, :, hs])
                  + (cnc_ref[0, :, h:h + 1] - cnT_ref[0, h:h + 1, :T]) for h, hs in enumerate(heads)]
        for h, hs in enumerate(heads):
            update(h, jnp.where(col <= row, scores[h], -jnp.inf), vn_ref[0, :, hs])
        for h, hs in enumerate(heads):
            ya_ref[0, :, hs] = _head_epilogue(acc_s[h], l_s[h], gattn_ref[:, hs], ga_ref[0, :, hs])


def _fox_sample(q, ck, cv, cpT, kn, vn, cnT, cnc, ga, gattn, tk):
    B, T, DA = q.shape
    H = cpT.shape[1]
    P = ck.shape[1]
    new = pl.BlockSpec((1, T, DA), lambda b, j: (b, 0, 0))
    hbm = pl.BlockSpec(memory_space=pl.ANY)
    return pl.pallas_call(
        _fox_sample_kernel,
        grid=(B, P // tk),
        in_specs=[new, hbm, hbm,
                  pl.BlockSpec((1, H, tk), lambda b, j: (b, 0, j)),
                  new, new,
                  pl.BlockSpec((1, H, cnT.shape[2]), lambda b, j: (b, 0, 0)),
                  pl.BlockSpec((1, T, H), lambda b, j: (b, 0, 0)),
                  new, _resident(gattn.shape)],
        out_specs=new,
        out_shape=jax.ShapeDtypeStruct((B, T, DA), BF16),
        scratch_shapes=[pltpu.VMEM((2, H, tk, HEAD_DIM), ck.dtype),
                        pltpu.VMEM((2, H, tk, HEAD_DIM), cv.dtype),
                        pltpu.SemaphoreType.DMA((2, 2, H)),
                        pltpu.VMEM((H, T, 1), F32), pltpu.VMEM((H, T, 1), F32),
                        pltpu.VMEM((H, T, HEAD_DIM), F32)],
        compiler_params=_params("arbitrary", "arbitrary"),
        name="fox_sample",
    )(q, ck, cv, cpT, kn, vn, cnT, cnc, ga, gattn)


def _segment_pitch(seg_len):
    pitch = -(-seg_len // SUBLANES) * SUBLANES
    return pitch if (pitch // SUBLANES) % 2 else pitch + SUBLANES


def _rglru_init(t, hist_ref, h0_ref, tail_s, hcar_s):
    @pl.when(t == 0)
    def _():
        tail_s[...] = jnp.zeros(tail_s.shape, F32)
        tail_s[SUBLANES - (CONV_W - 1):, :] = hist_ref[...]
        hcar_s[...] = h0_ref[...]


def _rglru_block(t, xl_ref, gl_ref, cw_ref, cb_ref, wr_ref, br_ref, wi_ref, bi_ref,
                 lam_ref, glru_ref, yl_ref, hT_ref, hist_out_ref, xn_s, hn_s, hl_s, ac_s, tail_s, hcar_s,
                 *, tt, reset_first):
    n_blocks = wr_ref.shape[0]
    seg = tt // SUBLANES
    pitch = _segment_pitch(seg)
    sub = lax.broadcasted_iota(jnp.int32, (SUBLANES, LRU_BLOCK), 0)
    first = (lax.broadcasted_iota(jnp.int32, (tt, LRU_BLOCK), 0) == 0) & (t == 0)
    decay = -LRU_C * _softplus(-lam_ref[...])

    def shift_segments(v, head):
        return jnp.where(sub == 0, head, pltpu.roll(v, 1, 0))

    gated = []
    for n in range(n_blocks):
        ns = slice(n * LRU_BLOCK, (n + 1) * LRU_BLOCK)
        for s in range(SUBLANES):
            xn_s[n, s * pitch:s * pitch + seg, :] = xl_ref[s * seg:(s + 1) * seg, ns]
        x = [xn_s[n, pl.ds(i, SUBLANES, stride=pitch), :] for i in range(seg)]
        before = [shift_segments(x[seg - k], tail_s[SUBLANES - k:SUBLANES - k + 1, ns])
                  for k in range(CONV_W - 1, 0, -1)]
        xs = jnp.concatenate(before + x, axis=0)
        xc = cb_ref[:, ns] + xs[0:tt] * cw_ref[0:1, ns]
        for j in range(1, CONV_W):
            xc = xc + xs[j * SUBLANES:j * SUBLANES + tt] * cw_ref[j:j + 1, ns]

        xcb = xc.astype(BF16)
        gated.append((xc, _dot(xcb, wr_ref[n]), _dot(xcb, wi_ref[n])))

    tail_s[...] = xl_ref[tt - SUBLANES:, :]
    hist_out_ref[...] = xl_ref[tt - (CONV_W - 1):, :]
    _rglru_recurrence(gated, sub, first, decay, shift_segments, br_ref, bi_ref, glru_ref, gl_ref,
                      yl_ref, hT_ref, hn_s, hl_s, ac_s, hcar_s, seg=seg, pitch=pitch,
                      reset_first=reset_first)


def _rglru_recurrence(gated, sub, first, decay, shift_segments, br_ref, bi_ref, glru_ref, gl_ref,
                      yl_ref, hT_ref, hn_s, hl_s, ac_s, hcar_s, *, seg, pitch, reset_first):
    for n, (xc, zr, zi) in enumerate(gated):
        ns = slice(n * LRU_BLOCK, (n + 1) * LRU_BLOCK)
        r = _sigmoid(zr + br_ref[:, ns])
        i_gate = _sigmoid(zi + bi_ref[:, ns])
        log_a = r * decay[:, ns]
        a = jnp.exp(log_a)
        one_minus_a2 = -jnp.tanh(log_a) * (1.0 + a * a)
        mult = jnp.where(one_minus_a2 > 0.0, one_minus_a2 * lax.rsqrt(one_minus_a2), 0.0)
        if reset_first:
            mult = jnp.where(first, 1.0, mult)
        u = mult * i_gate * xc

        rows = lambda v, i: v[i * SUBLANES:(i + 1) * SUBLANES]
        e, p = rows(u, 0), rows(a, 0)
        hl_s[n, 0:SUBLANES, :] = e
        ac_s[n, 0:SUBLANES, :] = p
        for i in range(1, seg):
            e = rows(a, i) * e + rows(u, i)
            p = rows(a, i) * p
            hl_s[n, i * SUBLANES:(i + 1) * SUBLANES, :] = e
            ac_s[n, i * SUBLANES:(i + 1) * SUBLANES, :] = p
        for step in (1, 2, 4):
            keep = sub >= step
            e = e + p * jnp.where(keep, pltpu.roll(e, step, 0), 0.0)
            p = p * jnp.where(keep, pltpu.roll(p, step, 0), 1.0)
        h_end = e + p * hcar_s[:, ns]
        carry_in = shift_segments(h_end, hcar_s[:, ns])
        hcar_s[:, ns] = h_end[SUBLANES - 1:, :]
        hT_ref[:, ns] = h_end[SUBLANES - 1:, :]

        for i in range(seg):
            hn_s[n, pl.ds(i, SUBLANES, stride=pitch), :] = (
                hl_s[n, i * SUBLANES:(i + 1) * SUBLANES, :]
                + ac_s[n, i * SUBLANES:(i + 1) * SUBLANES, :] * carry_in)
        h = jnp.concatenate([hn_s[n, s * pitch:s * pitch + seg, :] for s in range(SUBLANES)], axis=0)
        y = _group_rms(h) * glru_ref[:, ns] * gl_ref[:, ns]
        yl_ref[:, ns] = y.astype(BF16)


def _rglru_kernel(xl_ref, gl_ref, hist_ref, h0_ref, *rest, tt, reset_first):
    weights, (yl_ref, hT_ref, hist_out_ref), scratch = rest[:8], rest[8:11], rest[11:]
    xn_s, hn_s, hl_s, ac_s, tail_s, hcar_s = scratch
    t = pl.program_id(1)
    for g in range(xl_ref.shape[0]):
        _rglru_init(t, hist_ref.at[g], h0_ref.at[g], tail_s.at[g], hcar_s.at[g])
        _rglru_block(t, xl_ref.at[g], gl_ref.at[g], *weights, yl_ref.at[g], hT_ref.at[g],
                     hist_out_ref.at[g], xn_s, hn_s, hl_s, ac_s, tail_s.at[g], hcar_s.at[g],
                     tt=tt, reset_first=reset_first)


def _in_proj_rglru_kernel(x_ref, wq_ref, wk_ref, wv_ref, wf_ref, wga_ref, wxl_ref, wgl_ref, bf_ref,
                          hist_ref, h0_ref, *rest, tt, reset_first):
    lru_w = rest[:8]
    (q_ref, k_ref, v_ref, kb_ref, vp_ref, ga_ref, yl_ref, hT_ref, hist_out_ref,
     logf_t_ref) = rest[8:18]
    xl_s, gl_s, *lru_scratch = rest[18:]
    n_heads = logf_t_ref.shape[1]
    t = pl.program_id(1)
    _rglru_init(t, hist_ref.at[0], h0_ref.at[0], *lru_scratch[4:])
    xb = x_ref[0].astype(BF16)
    xl_s[...] = _dot(xb, wxl_ref[...])
    gl_s[...] = _silu(_dot(xb, wgl_ref[...]))
    _rglru_block(t, xl_s, gl_s, *lru_w, yl_ref.at[0], hT_ref.at[0], hist_out_ref.at[0],
                 *lru_scratch, tt=tt, reset_first=reset_first)
    q_ref[0] = (_dot(xb, wq_ref[...]) * Q_SCALE).astype(BF16)
    k = _dot(xb, wk_ref[...])
    k_ref[0] = k
    kb_ref[0] = k.astype(BF16)
    v = _dot(xb, wv_ref[...])
    v_ref[0] = v
    ones = jnp.ones((tt, HEAD_DIM), BF16)
    for h in range(n_heads):
        vp_ref[0, :, 2 * h * HEAD_DIM:(2 * h + 1) * HEAD_DIM] = (
            v[:, h * HEAD_DIM:(h + 1) * HEAD_DIM].astype(BF16))
        vp_ref[0, :, (2 * h + 1) * HEAD_DIM:(2 * h + 2) * HEAD_DIM] = ones
    logf = _log_sigmoid(_dot(xb, wf_ref[...]) + bf_ref[...])
    logf_t_ref[0] = logf.T[:n_heads, :]
    ga_ref[0] = _silu(_dot(xb, wga_ref[...]))


def _in_proj_rglru(x, wts, hist, h0, lw, tt, reset_first):
    B, S, D = x.shape
    wq, wk, wv, wf, wga, wxl, wgl, bf, n_heads = wts
    d_attn, d_lru = wq.shape[1], wxl.shape[1]
    blk = lambda n: pl.BlockSpec((1, tt, n), lambda b, t: (b, t, 0))
    per_b = lambda n: pl.BlockSpec((1, n, d_lru), lambda b, t: (b, 0, 0))
    slab_rows = SUBLANES * _segment_pitch(tt // SUBLANES)
    seq = lambda n, dt: jax.ShapeDtypeStruct((B, S, n), dt)
    out_shape = (seq(d_attn, BF16), seq(d_attn, F32), seq(d_attn, F32), seq(d_attn, BF16),
                 seq(2 * d_attn, BF16), seq(d_attn, F32), seq(d_lru, BF16),
                 jax.ShapeDtypeStruct((B, 1, d_lru), F32),
                 jax.ShapeDtypeStruct((B, CONV_W - 1, d_lru), F32),
                 jax.ShapeDtypeStruct((B, n_heads, S), F32))
    return pl.pallas_call(
        functools.partial(_in_proj_rglru_kernel, tt=tt, reset_first=reset_first),
        grid=(B, S // tt),
        in_specs=[blk(D)] + [_resident(w.shape) for w in (wq, wk, wv, wf, wga, wxl, wgl, bf)]
                 + [per_b(CONV_W - 1), per_b(1)] + [_resident(w.shape) for w in lw],
        out_specs=tuple(blk(s.shape[2]) for s in out_shape[:7]) + (per_b(1), per_b(CONV_W - 1))
                  + (pl.BlockSpec((1, n_heads, tt), lambda b, t: (b, 0, t)),),
        out_shape=out_shape,
        scratch_shapes=[pltpu.VMEM((tt, d_lru), F32), pltpu.VMEM((tt, d_lru), F32),
                        pltpu.VMEM((lw[2].shape[0], slab_rows, LRU_BLOCK), F32),
                        pltpu.VMEM((lw[2].shape[0], slab_rows, LRU_BLOCK), F32),
                        pltpu.VMEM((lw[2].shape[0], tt, LRU_BLOCK), F32),
                        pltpu.VMEM((lw[2].shape[0], tt, LRU_BLOCK), F32),
                        pltpu.VMEM((SUBLANES, d_lru), F32), pltpu.VMEM((1, d_lru), F32)],
        compiler_params=_params("parallel", "arbitrary"),
        name="in_proj_rglru",
    )(x, wq, wk, wv, wf, wga, wxl, wgl, bf, hist, h0, *lw)


RGLRU_STREAMS = 4


def _rglru(xl, gl, hist, h0, lw, tt, reset_first):
    B, T, DL = xl.shape
    cw, cb, wr, br, wi, bi, lam, glru = lw
    G = _tile(B, RGLRU_STREAMS)
    blk = pl.BlockSpec((G, tt, DL), lambda b, t: (b, t, 0))
    per_b = lambda n: pl.BlockSpec((G, n, DL), lambda b, t: (b, 0, 0))
    slab_rows = SUBLANES * _segment_pitch(tt // SUBLANES)
    return pl.pallas_call(
        functools.partial(_rglru_kernel, tt=tt, reset_first=reset_first),
        grid=(B // G, T // tt),
        in_specs=[blk, blk, per_b(CONV_W - 1), per_b(1)]
                 + [_resident(w.shape) for w in (cw, cb, wr, br, wi, bi, lam, glru)],
        out_specs=(blk, per_b(1), per_b(CONV_W - 1)),
        out_shape=(jax.ShapeDtypeStruct((B, T, DL), BF16),
                   jax.ShapeDtypeStruct((B, 1, DL), F32),
                   jax.ShapeDtypeStruct((B, CONV_W - 1, DL), F32)),
        scratch_shapes=[pltpu.VMEM((wr.shape[0], slab_rows, LRU_BLOCK), F32),
                        pltpu.VMEM((wr.shape[0], slab_rows, LRU_BLOCK), F32),
                        pltpu.VMEM((wr.shape[0], tt, LRU_BLOCK), F32),
                        pltpu.VMEM((wr.shape[0], tt, LRU_BLOCK), F32),
                        pltpu.VMEM((G, SUBLANES, DL), F32), pltpu.VMEM((G, 1, DL), F32)],
        compiler_params=_params("parallel", "arbitrary"),
        name="rglru",
    )(xl, gl, hist, h0, cw, cb, wr, br, wi, bi, lam, glru)


def _out_proj_kernel(x_ref, ya_ref, yl_ref, wa_ref, wl_ref, g_ref, b_ref, o_ref, *, alpha):
    tm = x_ref.shape[0]
    rows = [slice(r, r + OUT_PROJ_ROWS) for r in range(0, tm, OUT_PROJ_ROWS)]
    outs = [_dot(ya_ref[r, :], wa_ref[...]) + _dot(yl_ref[r, :], wl_ref[...]) for r in rows]
    for r, out in zip(rows, outs):
        h = alpha * x_ref[r, :] + out
        mu = jnp.mean(h, axis=-1, keepdims=True)
        d = h - mu
        var = jnp.mean(d * d, axis=-1, keepdims=True)
        o_ref[r, :] = d * lax.rsqrt(var + LN_EPS) * g_ref[...] + b_ref[...]


def _out_proj(x2d, ya, yl, wa, wl, g, b, alpha, tm):
    M, D = x2d.shape
    row = lambda n: pl.BlockSpec((tm, n), lambda i: (i, 0))
    return pl.pallas_call(
        functools.partial(_out_proj_kernel, alpha=alpha),
        grid=(M // tm,),
        in_specs=[row(D), row(ya.shape[1]), row(yl.shape[1])]
                 + [_resident(w.shape) for w in (wa, wl, g, b)],
        out_specs=row(D),
        out_shape=jax.ShapeDtypeStruct((M, D), F32),
        compiler_params=_params("parallel"),
        name="out_proj",
    )(x2d, ya, yl, wa, wl, g, b)


def _split_w_in_kernel(wt_hbm, wq_ref, wk_ref, wv_ref, wf_ref, wga_ref, wxl_ref, wgl_ref,
                       buf, fbuf, sems, *, splits):
    n_heads = splits[3] - splits[2]
    pieces = ((0, wq_ref), (splits[0], wk_ref), (splits[1], wv_ref), (splits[3], wga_ref),
              (splits[4], wxl_ref), (splits[5], wgl_ref))

    def piece_copy(i):
        start, out = pieces[i]
        rows = out.shape[1]
        return pltpu.make_async_copy(wt_hbm.at[pl.ds(start, rows), :], buf.at[i % 2, :rows, :],
                                     sems.at[i % 2])

    forget_copy = pltpu.make_async_copy(wt_hbm.at[pl.ds(splits[2], n_heads), :],
                                        fbuf.at[:n_heads, :], sems.at[2])
    fbuf[...] = jnp.zeros(fbuf.shape, F32)
    forget_copy.start()
    piece_copy(0).start()
    for i, (_, out) in enumerate(pieces):
        if i + 1 < len(pieces):
            piece_copy(i + 1).start()
        piece_copy(i).wait()
        out[...] = buf[i % 2, :out.shape[1], :].T.astype(BF16)
    forget_copy.wait()
    wf_ref[...] = fbuf[...].T.astype(BF16)


def _split_w_in(wt, splits):
    DIN, D = wt.shape
    widths = (splits[0], splits[1] - splits[0], splits[2] - splits[1], LANES,
              splits[4] - splits[3], splits[5] - splits[4], DIN - splits[5])
    return pl.pallas_call(
        functools.partial(_split_w_in_kernel, splits=splits),
        in_specs=[pl.BlockSpec(memory_space=pl.ANY)],
        out_shape=tuple(jax.ShapeDtypeStruct((D, n), BF16) for n in widths),
        scratch_shapes=[pltpu.VMEM((2, max(widths), D), F32), pltpu.VMEM((LANES, D), F32),
                        pltpu.SemaphoreType.DMA((3,))],
        compiler_params=pltpu.CompilerParams(vmem_limit_bytes=V7X_VMEM_LIMIT),
        name="split_w_in",
    )(wt)


def _split_w_out_kernel(w_ref, wa_ref, wl_ref):
    wa_ref[...] = w_ref[0, 0].astype(BF16)
    wl_ref[...] = w_ref[0, 1].astype(BF16)


def _split_w_out(w_out, layer, d_attn, rows):
    depth, DM, D = w_out.shape
    halves = w_out.reshape(depth, 2, d_attn, D)
    spec = pl.BlockSpec((rows, D), lambda i: (i, 0))
    return pl.pallas_call(
        _split_w_out_kernel,
        grid=(d_attn // rows,),
        in_specs=[pl.BlockSpec((1, 2, rows, D), lambda i: (layer, 0, i, 0))],
        out_specs=(spec, spec),
        out_shape=(jax.ShapeDtypeStruct((d_attn, D), BF16),) * 2,
        compiler_params=_params("parallel"),
        name="split_w_out",
    )(halves)


def _tile(n, pref):
    return pref if n % pref == 0 else n


def kernel(x_prompt, x_sample, cache_k, cache_v, cache_logf, state_h, state_conv, w_in, b_f, conv_w,
           conv_b, w_r, b_r, w_i, b_i, lru_lambda, g_attn, g_lru, w_out, ln_g, ln_b):
    depth, d_model, _ = w_in.shape
    n_heads = b_f.shape[1]
    d_attn = n_heads * HEAD_DIM
    d_lru = lru_lambda.shape[1]
    alpha = (2.0 * depth) ** 0.25
    B, S, _ = x_prompt.shape
    DB, T, _ = x_sample.shape
    P = cache_k.shape[2]

    xp = x_prompt.reshape(B * S, d_model)
    xs = x_sample.reshape(DB * T, d_model)
    outs_p, outs_s = [], []
    for l in range(depth):
        o = (d_attn, 2 * d_attn, 3 * d_attn, 3 * d_attn + n_heads, 4 * d_attn + n_heads,
             4 * d_attn + n_heads + d_lru)
        bf = jnp.pad(b_f[l][None, :], ((0, 0), (0, LANES - n_heads)))
        wts = _split_w_in(jnp.swapaxes(w_in, 1, 2)[l], o) + (bf, n_heads)
        lw = (conv_w[l], conv_b[l][None], w_r[l].astype(BF16), b_r[l][None], w_i[l].astype(BF16),
              b_i[l][None], lru_lambda[l][None], g_lru[l][None])
        gattn = g_attn[l][None]
        assert d_lru == d_attn, "w_out is split into two equal row halves"
        wa, wl = _split_w_out(w_out, l, d_attn, _tile(d_attn, 128))
        lng, lnb = ln_g[l][None], ln_b[l][None]

        q, k, v, kb, vp, ga, yl, h_T, hist_T, logf_t = _in_proj_rglru(
            xp.reshape(B, S, d_model), wts, jnp.zeros((B, CONV_W - 1, d_lru), F32),
            jnp.zeros((B, 1, d_lru), F32), lw, _tile(S, 256), True)
        cT = _cumsum(logf_t)
        ya = _fox_prompt(q, kb, vp, cT, ga, gattn, _tile(S, 512))
        outs_p.append((k.reshape(B, S, n_heads, HEAD_DIM), v.reshape(B, S, n_heads, HEAD_DIM),
                       jnp.transpose(logf_t, (0, 2, 1)), h_T.reshape(B, d_lru), hist_T))
        xp = _out_proj(xp, ya.reshape(B * S, d_attn), yl.reshape(B * S, d_lru), wa, wl, lng, lnb,
                       alpha, _tile(B * S, 512))

        q, k, v, kb, vb, logf, ga, xl, gl = _in_proj(xs, wts, _tile(DB * T, 256))
        r3 = lambda a: a.reshape(DB, T, a.shape[-1])
        newT = jnp.pad(jnp.transpose(r3(logf), (0, 2, 1)), ((0, 0), (0, 0), (0, LANES - T)))
        cpT, cnT = _cumsum_carry(jnp.transpose(cache_logf[l].astype(F32), (0, 2, 1)), newT)
        cnc = jnp.transpose(cnT[:, :, :T], (0, 2, 1))
        ya = _fox_sample(r3(q), cache_k[l], cache_v[l],
                         cpT, r3(kb), r3(vb), cnT, cnc, r3(ga), gattn, _tile(P, 2048))
        yl, h_T, hist_T = _rglru(r3(xl), r3(gl), state_conv[l], state_h[l][:, None, :], lw,
                                 _tile(T, 256), False)
        outs_s.append((k.reshape(DB, T, n_heads, HEAD_DIM), v.reshape(DB, T, n_heads, HEAD_DIM),
                       r3(logf), h_T.reshape(DB, d_lru), hist_T))
        xs = _out_proj(xs, ya.reshape(DB * T, d_attn), yl.reshape(DB * T, d_lru), wa, wl, lng, lnb,
                       alpha, _tile(DB * T, 256))

    stack = lambda outs, i: jnp.stack([o[i] for o in outs], 0)
    return (xp.reshape(B, S, d_model), xs.reshape(DB, T, d_model),
            stack(outs_p, 0), stack(outs_p, 1), stack(outs_p, 2), stack(outs_p, 3), stack(outs_p, 4),
            stack(outs_s, 0), stack(outs_s, 1), stack(outs_s, 2), stack(outs_s, 3), stack(outs_s, 4))
```

```python
import functools

import jax
import jax.numpy as jnp
from jax import lax
from jax.experimental import pallas as pl
from jax.experimental.pallas import tpu as pltpu

F32 = jnp.float32
BF16 = jnp.bfloat16

HEAD_DIM = 128
LRU_BLOCK = 128
CONV_W = 4
LRU_C = 8.0
LN_EPS = 1e-5
RMS_EPS = 1e-6
LOG2E = 1.4426950408889634
Q_SCALE = HEAD_DIM ** -0.5 * LOG2E

LANES = 128
SUBLANES = 8
V7X_VMEM_LIMIT = 56 * 2 ** 20
OUT_PROJ_ROWS = 128


def _dot(a, b):
    return jnp.dot(a, b, preferred_element_type=F32)


def _dot_nt(a, b):
    return lax.dot_general(a, b, (((1,), (1,)), ((), ())), preferred_element_type=F32)


def _softplus(y):
    return jnp.maximum(y, 0.0) + jnp.log1p(jnp.exp(-jnp.abs(y)))


def _log_sigmoid(y):
    return -_softplus(-y)


def _sigmoid(y):
    return 1.0 / (1.0 + jnp.exp2(y * -LOG2E))


def _silu(y):
    return y * _sigmoid(y)


def _group_rms(y):
    return y * lax.rsqrt(jnp.mean(y * y, axis=-1, keepdims=True) + RMS_EPS)


def _params(*semantics):
    return pltpu.CompilerParams(dimension_semantics=semantics, vmem_limit_bytes=V7X_VMEM_LIMIT)


def _resident(shape):
    return pl.BlockSpec(shape, lambda *_: (0,) * len(shape), pipeline_mode=pl.Buffered(1))


def _in_proj_kernel(x_ref, wq_ref, wk_ref, wv_ref, wf_ref, wga_ref, wxl_ref, wgl_ref, bf_ref,
                    q_ref, k_ref, v_ref, kb_ref, vb_ref, logf_ref, ga_ref, xl_ref, gl_ref):
    n_heads = logf_ref.shape[-1]
    xb = x_ref[...].astype(BF16)
    q_ref[...] = (_dot(xb, wq_ref[...]) * Q_SCALE).astype(BF16)
    k = _dot(xb, wk_ref[...])
    k_ref[...] = k
    kb_ref[...] = k.astype(BF16)
    v = _dot(xb, wv_ref[...])
    v_ref[...] = v
    vb_ref[...] = v.astype(BF16)
    zf = _dot(xb, wf_ref[...]) + bf_ref[...]
    logf_ref[...] = _log_sigmoid(zf)[:, :n_heads]
    ga_ref[...] = _silu(_dot(xb, wga_ref[...]))
    xl_ref[...] = _dot(xb, wxl_ref[...])
    gl_ref[...] = _silu(_dot(xb, wgl_ref[...]))


def _in_proj(x2d, wts, tm):
    M, D = x2d.shape
    wq, wk, wv, wf, wga, wxl, wgl, bf, n_heads = wts
    d_attn, d_lru = wq.shape[1], wxl.shape[1]
    row = lambda n: pl.BlockSpec((tm, n), lambda i: (i, 0))
    out_shape = (
        jax.ShapeDtypeStruct((M, d_attn), BF16),
        jax.ShapeDtypeStruct((M, d_attn), F32),
        jax.ShapeDtypeStruct((M, d_attn), F32),
        jax.ShapeDtypeStruct((M, d_attn), BF16),
        jax.ShapeDtypeStruct((M, d_attn), BF16),
        jax.ShapeDtypeStruct((M, n_heads), F32),
        jax.ShapeDtypeStruct((M, d_attn), F32),
        jax.ShapeDtypeStruct((M, d_lru), F32),
        jax.ShapeDtypeStruct((M, d_lru), F32),
    )
    return pl.pallas_call(
        _in_proj_kernel,
        grid=(M // tm,),
        in_specs=[row(D)] + [_resident(w.shape) for w in (wq, wk, wv, wf, wga, wxl, wgl, bf)],
        out_specs=tuple(row(s.shape[1]) for s in out_shape),
        out_shape=out_shape,
        compiler_params=_params("parallel"),
        name="in_proj",
    )(x2d, wq, wk, wv, wf, wga, wxl, wgl, bf)


def _cumsum_lanes(x):
    n = x.shape[-1]
    lane = lax.broadcasted_iota(jnp.int32, x.shape, x.ndim - 1)
    step = 1
    while step < n:
        x = x + jnp.where(lane >= step, pltpu.roll(x, step, x.ndim - 1), 0.0)
        step *= 2
    return x


def _rows(ref):
    return ref[...].reshape(ref.shape[0] * ref.shape[1], ref.shape[2])


def _cumsum_kernel(f_ref, c_ref):
    c_ref[...] = (_cumsum_lanes(_rows(f_ref)) * LOG2E).reshape(c_ref.shape)


def _cumsum_carry_kernel(past_ref, new_ref, cpast_ref, cnew_ref):
    cp = _cumsum_lanes(_rows(past_ref))
    cpast_ref[...] = (cp * LOG2E).reshape(cpast_ref.shape)
    cn = (_cumsum_lanes(_rows(new_ref)) + cp[:, cp.shape[1] - 1:]) * LOG2E
    cnew_ref[...] = cn.reshape(cnew_ref.shape)


CUMSUM_STREAMS = 4


def _cumsum(fT):
    B, H, S = fT.shape
    nb = _tile(B, CUMSUM_STREAMS)
    spec = pl.BlockSpec((nb, H, S), lambda b: (b, 0, 0))
    return pl.pallas_call(
        _cumsum_kernel, grid=(B // nb,), in_specs=[spec], out_specs=spec,
        out_shape=jax.ShapeDtypeStruct(fT.shape, F32),
        compiler_params=_params("parallel"), name="cumsum",
    )(fT)


def _cumsum_carry(pastT, newT):
    B, H, P = pastT.shape
    N = newT.shape[2]
    nb = _tile(B, CUMSUM_STREAMS)
    pspec = pl.BlockSpec((nb, H, P), lambda b: (b, 0, 0))
    nspec = pl.BlockSpec((nb, H, N), lambda b: (b, 0, 0))
    return pl.pallas_call(
        _cumsum_carry_kernel, grid=(B // nb,), in_specs=[pspec, nspec], out_specs=(pspec, nspec),
        out_shape=(jax.ShapeDtypeStruct(pastT.shape, F32), jax.ShapeDtypeStruct(newT.shape, F32)),
        compiler_params=_params("parallel"), name="cumsum_carry",
    )(pastT, newT)


def _head_epilogue(acc, l, gain, gate):
    normed = acc * lax.rsqrt(jnp.mean(acc * acc, axis=-1, keepdims=True) + RMS_EPS * (l * l))
    return (normed * gain * gate).astype(BF16)


def _fox_prompt_kernel(q_ref, kb_ref, vp_ref, cT_ref, ga_ref, gattn_ref, ya_ref,
                       m_s, acc_s, *, tq):
    n_heads = cT_ref.shape[1]
    qi = pl.program_id(1)
    q0 = pl.multiple_of(qi * tq, tq)

    def values(h, k0, rows):
        return vp_ref[0, pl.ds(k0, rows), 2 * h * HEAD_DIM:(2 * h + 2) * HEAD_DIM]

    def scores(h, k0):
        hs = slice(h * HEAD_DIM, (h + 1) * HEAD_DIM)
        s = _dot_nt(q_ref[0, :, hs], kb_ref[0, pl.ds(k0, tq), hs])
        return s - cT_ref[0, h:h + 1, pl.ds(k0, tq)]

    def row_max(s):
        m = s[:, :LANES]
        for c in range(1, s.shape[1] // LANES):
            m = jnp.maximum(m, s[:, c * LANES:(c + 1) * LANES])
        return jnp.max(m, axis=-1, keepdims=True)

    def weights(s, m):
        parts = [jnp.exp2(s[:, c * LANES:(c + 1) * LANES] - m) for c in range(s.shape[1] // LANES)]
        return jnp.concatenate(parts, axis=1).astype(BF16)

    half = tq // 2
    visible_a = (lax.broadcasted_iota(jnp.int32, (tq, half), 1)
                 <= lax.broadcasted_iota(jnp.int32, (tq, half), 0))
    visible_b = (lax.broadcasted_iota(jnp.int32, (half, half), 1)
                 <= lax.broadcasted_iota(jnp.int32, (half, half), 0))

    def diag_scores(h):
        hs = slice(h * HEAD_DIM, (h + 1) * HEAD_DIM)
        sa = _dot_nt(q_ref[0, :, hs], kb_ref[0, pl.ds(q0, half), hs])
        sb = _dot_nt(q_ref[0, half:, hs], kb_ref[0, pl.ds(q0 + half, half), hs])
        return (sa - cT_ref[0, h:h + 1, pl.ds(q0, half)],
                sb - cT_ref[0, h:h + 1, pl.ds(q0 + half, half)])

    s_next = diag_scores(0)
    for h in range(n_heads):
        sa = jnp.where(visible_a, s_next[0], -jnp.inf)
        sb = jnp.where(visible_b, s_next[1], -jnp.inf)
        if h + 1 < n_heads:
            s_next = diag_scores(h + 1)
        s_top = sa[:half]
        s_bot = jnp.concatenate([sa[half:], sb], axis=1)
        m_top = jnp.broadcast_to(row_max(s_top), (half, LANES))
        m_bot = jnp.broadcast_to(row_max(s_bot), (half, LANES))
        m_s[h, :half, :] = m_top
        m_s[h, half:, :] = m_bot
        acc_s[h, :half, :] = _dot(weights(s_top, m_top), values(h, q0, half))
        acc_s[h, half:, :] = _dot(weights(s_bot, m_bot), values(h, q0, tq))

    def body(j, carry):
        k0 = pl.multiple_of(j * tq, tq)
        s_next = scores(0, k0)
        for h in range(n_heads):
            s = s_next
            if h + 1 < n_heads:
                s_next = scores(h + 1, k0)
            m_old = m_s[h]
            m_new = jnp.maximum(m_old, row_max(s))
            alpha = jnp.exp2(m_old - m_new)
            m_s[h] = m_new
            pv = _dot(weights(s, m_new), values(h, k0, tq))
            acc_s[h, :, :HEAD_DIM] = alpha * acc_s[h, :, :HEAD_DIM] + pv[:, :HEAD_DIM]
            acc_s[h, :, HEAD_DIM:] = alpha * acc_s[h, :, HEAD_DIM:] + pv[:, HEAD_DIM:]
        return carry

    lax.fori_loop(0, qi, body, 0)

    for h in range(n_heads):
        hs = slice(h * HEAD_DIM, (h + 1) * HEAD_DIM)
        ya_ref[0, :, hs] = _head_epilogue(acc_s[h, :, :HEAD_DIM], acc_s[h, :, HEAD_DIM:],
                                          gattn_ref[:, hs], ga_ref[0, :, hs])


def _fox_prompt(q, kb, vp, cT, ga, gattn, tq):
    B, S, DA = q.shape
    H = cT.shape[1]
    blk = pl.BlockSpec((1, tq, DA), lambda b, i: (b, i, 0))
    seq = lambda n: pl.BlockSpec((1, S, n), lambda b, i: (b, 0, 0))
    return pl.pallas_call(
        functools.partial(_fox_prompt_kernel, tq=tq),
        grid=(B, S // tq),
        in_specs=[blk, seq(DA), seq(vp.shape[2]), pl.BlockSpec((1, H, S), lambda b, i: (b, 0, 0)),
                  blk, _resident(gattn.shape)],
        out_specs=blk,
        out_shape=jax.ShapeDtypeStruct((B, S, DA), BF16),
        scratch_shapes=[pltpu.VMEM((H, tq, LANES), F32),
                        pltpu.VMEM((H, tq, 2 * HEAD_DIM), F32)],
        compiler_params=_params("parallel", "arbitrary"),
        name="fox_prompt",
    )(q, kb, vp, cT, ga, gattn)


def _fox_sample_kernel(q_ref, ck_hbm, cv_hbm, cpT_ref, kn_ref, vn_ref, cnT_ref, cnc_ref, ga_ref,
                       gattn_ref, ya_ref, kbuf, vbuf, sems, m_s, l_s, acc_s):
    n_heads = cpT_ref.shape[1]
    T = q_ref.shape[1]
    tk = cpT_ref.shape[2]
    b, j = pl.program_id(0), pl.program_id(1)
    nj = pl.num_programs(1)
    step = b * nj + j
    slot = lax.rem(step, 2)

    def block_copies(stream, blk, sl):
        rows = pl.ds(blk * tk, tk)
        copies = []
        for h in range(n_heads):
            copies.append(pltpu.make_async_copy(ck_hbm.at[stream, rows, h, :], kbuf.at[sl, h],
                                                sems.at[0, sl, h]))
            copies.append(pltpu.make_async_copy(cv_hbm.at[stream, rows, h, :], vbuf.at[sl, h],
                                                sems.at[1, sl, h]))
        return copies

    @pl.when(step == 0)
    def _():
        for c in block_copies(b, j, slot):
            c.start()

    @pl.when(step + 1 < pl.num_programs(0) * nj)
    def _():
        last = j + 1 == nj
        for c in block_copies(jnp.where(last, b + 1, b), jnp.where(last, 0, j + 1), 1 - slot):
            c.start()

    for c in block_copies(b, j, slot):
        c.wait()

    @pl.when(j == 0)
    def _():
        m_s[...] = jnp.full(m_s.shape, -jnp.inf, F32)
        l_s[...] = jnp.zeros(l_s.shape, F32)
        acc_s[...] = jnp.zeros(acc_s.shape, F32)

    def update(h, s, vals):
        m_old = m_s[h]
        m_new = jnp.maximum(m_old, jnp.max(s, axis=-1, keepdims=True))
        alpha = jnp.exp2(m_old - m_new)
        p = jnp.exp2(s - m_new)
        m_s[h] = m_new
        l_s[h] = alpha * l_s[h] + jnp.sum(p, axis=-1, keepdims=True)
        acc_s[h] = alpha * acc_s[h] + _dot(p.astype(BF16), vals)

    def cached_scores(h):
        hs = slice(h * HEAD_DIM, (h + 1) * HEAD_DIM)
        kh = kbuf[slot, h].astype(BF16)
        return _dot_nt(q_ref[0, :, hs], kh) + (cnc_ref[0, :, h:h + 1] - cpT_ref[0, h:h + 1, :])

    s_next = cached_scores(0)
    for h in range(n_heads):
        s = s_next
        if h + 1 < n_heads:
            s_next = cached_scores(h + 1)
        update(h, s, vbuf[slot, h].astype(BF16))

    @pl.when(j == pl.num_programs(1) - 1)
    def _():
        row = lax.broadcasted_iota(jnp.int32, (T, T), 0)
        col = lax.broadcasted_iota(jnp.int32, (T, T), 1)
        heads = [slice(h * HEAD_DIM, (h + 1) * HEAD_DIM) for h in range(n_heads)]
        scores = [_dot_nt(q_ref[0, :, hs], kn_ref[0, :, hs])
                  + (cnc_ref[0, :, h:h + 1] - cnT_ref[0, h:h + 1, :T]) for h, hs in enumerate(heads)]
        for h, hs in enumerate(heads):
            update(h, jnp.where(col <= row, scores[h], -jnp.inf), vn_ref[0, :, hs])
        for h, hs in enumerate(heads):
            ya_ref[0, :, hs] = _head_epilogue(acc_s[h], l_s[h], gattn_ref[:, hs], ga_ref[0, :, hs])


def _fox_sample(q, ck, cv, cpT, kn, vn, cnT, cnc, ga, gattn, tk):
    B, T, DA = q.shape
    H = cpT.shape[1]
    P = ck.shape[1]
    new = pl.BlockSpec((1, T, DA), lambda b, j: (b, 0, 0))
    hbm = pl.BlockSpec(memory_space=pl.ANY)
    return pl.pallas_call(
        _fox_sample_kernel,
        grid=(B, P // tk),
        in_specs=[new, hbm, hbm,
                  pl.BlockSpec((1, H, tk), lambda b, j: (b, 0, j)),
                  new, new,
                  pl.BlockSpec((1, H, cnT.shape[2]), lambda b, j: (b, 0, 0)),
                  pl.BlockSpec((1, T, H), lambda b, j: (b, 0, 0)),
                  new, _resident(gattn.shape)],
        out_specs=new,
        out_shape=jax.ShapeDtypeStruct((B, T, DA), BF16),
        scratch_shapes=[pltpu.VMEM((2, H, tk, HEAD_DIM), ck.dtype),
                        pltpu.VMEM((2, H, tk, HEAD_DIM), cv.dtype),
                        pltpu.SemaphoreType.DMA((2, 2, H)),
                        pltpu.VMEM((H, T, 1), F32), pltpu.VMEM((H, T, 1), F32),
                        pltpu.VMEM((H, T, HEAD_DIM), F32)],
        compiler_params=_params("arbitrary", "arbitrary"),
        name="fox_sample",
    )(q, ck, cv, cpT, kn, vn, cnT, cnc, ga, gattn)


def _segment_pitch(seg_len):
    pitch = -(-seg_len // SUBLANES) * SUBLANES
    return pitch if (pitch // SUBLANES) % 2 else pitch + SUBLANES


def _rglru_init(t, hist_ref, h0_ref, tail_s, hcar_s):
    @pl.when(t == 0)
    def _():
        tail_s[...] = jnp.zeros(tail_s.shape, F32)
        tail_s[SUBLANES - (CONV_W - 1):, :] = hist_ref[...]
        hcar_s[...] = h0_ref[...]


def _rglru_block(t, xl_ref, gl_ref, cw_ref, cb_ref, wr_ref, br_ref, wi_ref, bi_ref,
                 lam_ref, glru_ref, yl_ref, hT_ref, hist_out_ref, xn_s, hn_s, hl_s, ac_s, tail_s, hcar_s,
                 *, tt, reset_first):
    n_blocks = wr_ref.shape[0]
    seg = tt // SUBLANES
    pitch = _segment_pitch(seg)
    sub = lax.broadcasted_iota(jnp.int32, (SUBLANES, LRU_BLOCK), 0)
    first = (lax.broadcasted_iota(jnp.int32, (tt, LRU_BLOCK), 0) == 0) & (t == 0)
    decay = -LRU_C * _softplus(-lam_ref[...])

    def shift_segments(v, head):
        return jnp.where(sub == 0, head, pltpu.roll(v, 1, 0))

    gated = []
    for n in range(n_blocks):
        ns = slice(n * LRU_BLOCK, (n + 1) * LRU_BLOCK)
        for s in range(SUBLANES):
            xn_s[n, s * pitch:s * pitch + seg, :] = xl_ref[s * seg:(s + 1) * seg, ns]
        x = [xn_s[n, pl.ds(i, SUBLANES, stride=pitch), :] for i in range(seg)]
        before = [shift_segments(x[seg - k], tail_s[SUBLANES - k:SUBLANES - k + 1, ns])
                  for k in range(CONV_W - 1, 0, -1)]
        xs = jnp.concatenate(before + x, axis=0)
        xc = cb_ref[:, ns] + xs[0:tt] * cw_ref[0:1, ns]
        for j in range(1, CONV_W):
            xc = xc + xs[j * SUBLANES:j * SUBLANES + tt] * cw_ref[j:j + 1, ns]

        xcb = xc.astype(BF16)
        gated.append((xc, _dot(xcb, wr_ref[n]), _dot(xcb, wi_ref[n])))

    tail_s[...] = xl_ref[tt - SUBLANES:, :]
    hist_out_ref[...] = xl_ref[tt - (CONV_W - 1):, :]
    _rglru_recurrence(gated, sub, first, decay, shift_segments, br_ref, bi_ref, glru_ref, gl_ref,
                      yl_ref, hT_ref, hn_s, hl_s, ac_s, hcar_s, seg=seg, pitch=pitch,
                      reset_first=reset_first)


def _rglru_recurrence(gated, sub, first, decay, shift_segments, br_ref, bi_ref, glru_ref, gl_ref,
                      yl_ref, hT_ref, hn_s, hl_s, ac_s, hcar_s, *, seg, pitch, reset_first):
    for n, (xc, zr, zi) in enumerate(gated):
        ns = slice(n * LRU_BLOCK, (n + 1) * LRU_BLOCK)
        r = _sigmoid(zr + br_ref[:, ns])
        i_gate = _sigmoid(zi + bi_ref[:, ns])
        log_a = r * decay[:, ns]
        a = jnp.exp(log_a)
        one_minus_a2 = -jnp.tanh(log_a) * (1.0 + a * a)
        mult = jnp.where(one_minus_a2 > 0.0, one_minus_a2 * lax.rsqrt(one_minus_a2), 0.0)
        if reset_first:
            mult = jnp.where(first, 1.0, mult)
        u = mult * i_gate * xc

        rows = lambda v, i: v[i * SUBLANES:(i + 1) * SUBLANES]
        e, p = rows(u, 0), rows(a, 0)
        hl_s[n, 0:SUBLANES, :] = e
        ac_s[n, 0:SUBLANES, :] = p
        for i in range(1, seg):
            e = rows(a, i) * e + rows(u, i)
            p = rows(a, i) * p
            hl_s[n, i * SUBLANES:(i + 1) * SUBLANES, :] = e
            ac_s[n, i * SUBLANES:(i + 1) * SUBLANES, :] = p
        for step in (1, 2, 4):
            keep = sub >= step
            e = e + p * jnp.where(keep, pltpu.roll(e, step, 0), 0.0)
            p = p * jnp.where(keep, pltpu.roll(p, step, 0), 1.0)
        h_end = e + p * hcar_s[:, ns]
        carry_in = shift_segments(h_end, hcar_s[:, ns])
        hcar_s[:, ns] = h_end[SUBLANES - 1:, :]
        hT_ref[:, ns] = h_end[SUBLANES - 1:, :]

        for i in range(seg):
            hn_s[n, pl.ds(i, SUBLANES, stride=pitch), :] = (
                hl_s[n, i * SUBLANES:(i + 1) * SUBLANES, :]
                + ac_s[n, i * SUBLANES:(i + 1) * SUBLANES, :] * carry_in)
        h = jnp.concatenate([hn_s[n, s * pitch:s * pitch + seg, :] for s in range(SUBLANES)], axis=0)
        y = _group_rms(h) * glru_ref[:, ns] * gl_ref[:, ns]
        yl_ref[:, ns] = y.astype(BF16)


def _rglru_kernel(xl_ref, gl_ref, hist_ref, h0_ref, *rest, tt, reset_first):
    weights, (yl_ref, hT_ref, hist_out_ref), scratch = rest[:8], rest[8:11], rest[11:]
    xn_s, hn_s, hl_s, ac_s, tail_s, hcar_s = scratch
    t = pl.program_id(1)
    for g in range(xl_ref.shape[0]):
        _rglru_init(t, hist_ref.at[g], h0_ref.at[g], tail_s.at[g], hcar_s.at[g])
        _rglru_block(t, xl_ref.at[g], gl_ref.at[g], *weights, yl_ref.at[g], hT_ref.at[g],
                     hist_out_ref.at[g], xn_s, hn_s, hl_s, ac_s, tail_s.at[g], hcar_s.at[g],
                     tt=tt, reset_first=reset_first)


def _in_proj_rglru_kernel(x_ref, wq_ref, wk_ref, wv_ref, wf_ref, wga_ref, wxl_ref, wgl_ref, bf_ref,
                          hist_ref, h0_ref, *rest, tt, reset_first):
    lru_w = rest[:8]
    (q_ref, k_ref, v_ref, kb_ref, vp_ref, ga_ref, yl_ref, hT_ref, hist_out_ref,
     logf_t_ref) = rest[8:18]
    xl_s, gl_s, *lru_scratch = rest[18:]
    n_heads = logf_t_ref.shape[1]
    t = pl.program_id(1)
    _rglru_init(t, hist_ref.at[0], h0_ref.at[0], *lru_scratch[4:])
    xb = x_ref[0].astype(BF16)
    xl_s[...] = _dot(xb, wxl_ref[...])
    gl_s[...] = _silu(_dot(xb, wgl_ref[...]))
    _rglru_block(t, xl_s, gl_s, *lru_w, yl_ref.at[0], hT_ref.at[0], hist_out_ref.at[0],
                 *lru_scratch, tt=tt, reset_first=reset_first)
    q_ref[0] = (_dot(xb, wq_ref[...]) * Q_SCALE).astype(BF16)
    k = _dot(xb, wk_ref[...])
    k_ref[0] = k
    kb_ref[0] = k.astype(BF16)
    v = _dot(xb, wv_ref[...])
    v_ref[0] = v
    ones = jnp.ones((tt, HEAD_DIM), BF16)
    for h in range(n_heads):
        vp_ref[0, :, 2 * h * HEAD_DIM:(2 * h + 1) * HEAD_DIM] = (
            v[:, h * HEAD_DIM:(h + 1) * HEAD_DIM].astype(BF16))
        vp_ref[0, :, (2 * h + 1) * HEAD_DIM:(2 * h + 2) * HEAD_DIM] = ones
    logf = _log_sigmoid(_dot(xb, wf_ref[...]) + bf_ref[...])
    logf_t_ref[0] = logf.T[:n_heads, :]
    ga_ref[0] = _silu(_dot(xb, wga_ref[...]))


def _in_proj_rglru(x, wts, hist, h0, lw, tt, reset_first):
    B, S, D = x.shape
    wq, wk, wv, wf, wga, wxl, wgl, bf, n_heads = wts
    d_attn, d_lru = wq.shape[1], wxl.shape[1]
    blk = lambda n: pl.BlockSpec((1, tt, n), lambda b, t: (b, t, 0))
    per_b = lambda n: pl.BlockSpec((1, n, d_lru), lambda b, t: (b, 0, 0))
    slab_rows = SUBLANES * _segment_pitch(tt // SUBLANES)
    seq = lambda n, dt: jax.ShapeDtypeStruct((B, S, n), dt)
    out_shape = (seq(d_attn, BF16), seq(d_attn, F32), seq(d_attn, F32), seq(d_attn, BF16),
                 seq(2 * d_attn, BF16), seq(d_attn, F32), seq(d_lru, BF16),
                 jax.ShapeDtypeStruct((B, 1, d_lru), F32),
                 jax.ShapeDtypeStruct((B, CONV_W - 1, d_lru), F32),
                 jax.ShapeDtypeStruct((B, n_heads, S), F32))
    return pl.pallas_call(
        functools.partial(_in_proj_rglru_kernel, tt=tt, reset_first=reset_first),
        grid=(B, S // tt),
        in_specs=[blk(D)] + [_resident(w.shape) for w in (wq, wk, wv, wf, wga, wxl, wgl, bf)]
                 + [per_b(CONV_W - 1), per_b(1)] + [_resident(w.shape) for w in lw],
        out_specs=tuple(blk(s.shape[2]) for s in out_shape[:7]) + (per_b(1), per_b(CONV_W - 1))
                  + (pl.BlockSpec((1, n_heads, tt), lambda b, t: (b, 0, t)),),
        out_shape=out_shape,
        scratch_shapes=[pltpu.VMEM((tt, d_lru), F32), pltpu.VMEM((tt, d_lru), F32),
                        pltpu.VMEM((lw[2].shape[0], slab_rows, LRU_BLOCK), F32),
                        pltpu.VMEM((lw[2].shape[0], slab_rows, LRU_BLOCK), F32),
                        pltpu.VMEM((lw[2].shape[0], tt, LRU_BLOCK), F32),
                        pltpu.VMEM((lw[2].shape[0], tt, LRU_BLOCK), F32),
                        pltpu.VMEM((SUBLANES, d_lru), F32), pltpu.VMEM((1, d_lru), F32)],
        compiler_params=_params("parallel", "arbitrary"),
        name="in_proj_rglru",
    )(x, wq, wk, wv, wf, wga, wxl, wgl, bf, hist, h0, *lw)


RGLRU_STREAMS = 4


def _rglru(xl, gl, hist, h0, lw, tt, reset_first):
    B, T, DL = xl.shape
    cw, cb, wr, br, wi, bi, lam, glru = lw
    G = _tile(B, RGLRU_STREAMS)
    blk = pl.BlockSpec((G, tt, DL), lambda b, t: (b, t, 0))
    per_b = lambda n: pl.BlockSpec((G, n, DL), lambda b, t: (b, 0, 0))
    slab_rows = SUBLANES * _segment_pitch(tt // SUBLANES)
    return pl.pallas_call(
        functools.partial(_rglru_kernel, tt=tt, reset_first=reset_first),
        grid=(B // G, T // tt),
        in_specs=[blk, blk, per_b(CONV_W - 1), per_b(1)]
                 + [_resident(w.shape) for w in (cw, cb, wr, br, wi, bi, lam, glru)],
        out_specs=(blk, per_b(1), per_b(CONV_W - 1)),
        out_shape=(jax.ShapeDtypeStruct((B, T, DL), BF16),
                   jax.ShapeDtypeStruct((B, 1, DL), F32),
                   jax.ShapeDtypeStruct((B, CONV_W - 1, DL), F32)),
        scratch_shapes=[pltpu.VMEM((wr.shape[0], slab_rows, LRU_BLOCK), F32),
                        pltpu.VMEM((wr.shape[0], slab_rows, LRU_BLOCK), F32),
                        pltpu.VMEM((wr.shape[0], tt, LRU_BLOCK), F32),
                        pltpu.VMEM((wr.shape[0], tt, LRU_BLOCK), F32),
                        pltpu.VMEM((G, SUBLANES, DL), F32), pltpu.VMEM((G, 1, DL), F32)],
        compiler_params=_params("parallel", "arbitrary"),
        name="rglru",
    )(xl, gl, hist, h0, cw, cb, wr, br, wi, bi, lam, glru)


def _out_proj_kernel(x_ref, ya_ref, yl_ref, wa_ref, wl_ref, g_ref, b_ref, o_ref, *, alpha):
    tm = x_ref.shape[0]
    rows = [slice(r, r + OUT_PROJ_ROWS) for r in range(0, tm, OUT_PROJ_ROWS)]
    outs = [_dot(ya_ref[r, :], wa_ref[...]) + _dot(yl_ref[r, :], wl_ref[...]) for r in rows]
    for r, out in zip(rows, outs):
        h = alpha * x_ref[r, :] + out
        mu = jnp.mean(h, axis=-1, keepdims=True)
        d = h - mu
        var = jnp.mean(d * d, axis=-1, keepdims=True)
        o_ref[r, :] = d * lax.rsqrt(var + LN_EPS) * g_ref[...] + b_ref[...]


def _out_proj(x2d, ya, yl, wa, wl, g, b, alpha, tm):
    M, D = x2d.shape
    row = lambda n: pl.BlockSpec((tm, n), lambda i: (i, 0))
    return pl.pallas_call(
        functools.partial(_out_proj_kernel, alpha=alpha),
        grid=(M // tm,),
        in_specs=[row(D), row(ya.shape[1]), row(yl.shape[1])]
                 + [_resident(w.shape) for w in (wa, wl, g, b)],
        out_specs=row(D),
        out_shape=jax.ShapeDtypeStruct((M, D), F32),
        compiler_params=_params("parallel"),
        name="out_proj",
    )(x2d, ya, yl, wa, wl, g, b)


def _split_weights_kernel(wt_hbm, wo_hbm, wq_ref, wk_ref, wv_ref, wf_ref, wga_ref, wxl_ref, wgl_ref,
                          wa_ref, wl_ref, buf, fbuf, sems, *, splits):
    n_heads = splits[3] - splits[2]
    half = wa_ref.shape[0]
    pieces = ((wt_hbm, 0, wq_ref, True), (wt_hbm, splits[0], wk_ref, True),
              (wt_hbm, splits[1], wv_ref, True), (wt_hbm, splits[3], wga_ref, True),
              (wt_hbm, splits[4], wxl_ref, True), (wt_hbm, splits[5], wgl_ref, True),
              (wo_hbm, 0, wa_ref, False), (wo_hbm, half, wl_ref, False))

    def piece_rows(i):
        _, _, out, transposed = pieces[i]
        return out.shape[1] if transposed else out.shape[0]

    def piece_copy(i):
        src, start, _, _ = pieces[i]
        rows = piece_rows(i)
        return pltpu.make_async_copy(src.at[pl.ds(start, rows), :], buf.at[i % 2, :rows, :],
                                     sems.at[i % 2])

    forget_copy = pltpu.make_async_copy(wt_hbm.at[pl.ds(splits[2], n_heads), :],
                                        fbuf.at[:n_heads, :], sems.at[2])
    fbuf[...] = jnp.zeros(fbuf.shape, F32)
    forget_copy.start()
    piece_copy(0).start()
    for i, (_, _, out, transposed) in enumerate(pieces):
        if i + 1 < len(pieces):
            piece_copy(i + 1).start()
        piece_copy(i).wait()
        rows = buf[i % 2, :piece_rows(i), :]
        out[...] = (rows.T if transposed else rows).astype(BF16)
    forget_copy.wait()
    wf_ref[...] = fbuf[...].T.astype(BF16)


def _split_weights(wt, wo, splits, d_attn):
    DIN, D = wt.shape
    widths = (splits[0], splits[1] - splits[0], splits[2] - splits[1], LANES,
              splits[4] - splits[3], splits[5] - splits[4], DIN - splits[5])
    assert wo.shape == (2 * d_attn, D), "w_out is split into two equal row halves"
    hbm = pl.BlockSpec(memory_space=pl.ANY)
    return pl.pallas_call(
        functools.partial(_split_weights_kernel, splits=splits),
        in_specs=[hbm, hbm],
        out_shape=tuple(jax.ShapeDtypeStruct((D, n), BF16) for n in widths)
                  + (jax.ShapeDtypeStruct((d_attn, D), BF16),) * 2,
        scratch_shapes=[pltpu.VMEM((2, max(max(widths), d_attn), D), F32),
                        pltpu.VMEM((LANES, D), F32), pltpu.SemaphoreType.DMA((3,))],
        compiler_params=pltpu.CompilerParams(vmem_limit_bytes=V7X_VMEM_LIMIT),
        name="split_weights",
    )(wt, wo)


def _tile(n, pref):
    return pref if n % pref == 0 else n


def kernel(x_prompt, x_sample, cache_k, cache_v, cache_logf, state_h, state_conv, w_in, b_f, conv_w,
           conv_b, w_r, b_r, w_i, b_i, lru_lambda, g_attn, g_lru, w_out, ln_g, ln_b):
    depth, d_model, _ = w_in.shape
    n_heads = b_f.shape[1]
    d_attn = n_heads * HEAD_DIM
    d_lru = lru_lambda.shape[1]
    alpha = (2.0 * depth) ** 0.25
    B, S, _ = x_prompt.shape
    DB, T, _ = x_sample.shape
    P = cache_k.shape[2]

    xp = x_prompt.reshape(B * S, d_model)
    xs = x_sample.reshape(DB * T, d_model)
    outs_p, outs_s = [], []
    for l in range(depth):
        o = (d_attn, 2 * d_attn, 3 * d_attn, 3 * d_attn + n_heads, 4 * d_attn + n_heads,
             4 * d_attn + n_heads + d_lru)
        bf = jnp.pad(b_f[l][None, :], ((0, 0), (0, LANES - n_heads)))
        *w_proj, wa, wl = _split_weights(jnp.swapaxes(w_in, 1, 2)[l], w_out[l], o, d_attn)
        wts = tuple(w_proj) + (bf, n_heads)
        lw = (conv_w[l], conv_b[l][None], w_r[l].astype(BF16), b_r[l][None], w_i[l].astype(BF16),
              b_i[l][None], lru_lambda[l][None], g_lru[l][None])
        gattn = g_attn[l][None]
        assert d_lru == d_attn, "w_out is split into two equal row halves"
        lng, lnb = ln_g[l][None], ln_b[l][None]

        q, k, v, kb, vp, ga, yl, h_T, hist_T, logf_t = _in_proj_rglru(
            xp.reshape(B, S, d_model), wts, jnp.zeros((B, CONV_W - 1, d_lru), F32),
            jnp.zeros((B, 1, d_lru), F32), lw, _tile(S, 256), True)
        cT = _cumsum(logf_t)
        ya = _fox_prompt(q, kb, vp, cT, ga, gattn, _tile(S, 512))
        outs_p.append((k.reshape(B, S, n_heads, HEAD_DIM), v.reshape(B, S, n_heads, HEAD_DIM),
                       jnp.transpose(logf_t, (0, 2, 1)), h_T.reshape(B, d_lru), hist_T))
        xp = _out_proj(xp, ya.reshape(B * S, d_attn), yl.reshape(B * S, d_lru), wa, wl, lng, lnb,
                       alpha, _tile(B * S, 512))

        q, k, v, kb, vb, logf, ga, xl, gl = _in_proj(xs, wts, _tile(DB * T, 256))
        r3 = lambda a: a.reshape(DB, T, a.shape[-1])
        newT = jnp.pad(jnp.transpose(r3(logf), (0, 2, 1)), ((0, 0), (0, 0), (0, LANES - T)))
        cpT, cnT = _cumsum_carry(jnp.transpose(cache_logf[l].astype(F32), (0, 2, 1)), newT)
        cnc = jnp.transpose(cnT[:, :, :T], (0, 2, 1))
        ya = _fox_sample(r3(q), cache_k[l], cache_v[l],
                         cpT, r3(kb), r3(vb), cnT, cnc, r3(ga), gattn, _tile(P, 2048))
        yl, h_T, hist_T = _rglru(r3(xl), r3(gl), state_conv[l], state_h[l][:, None, :], lw,
                                 _tile(T, 256), False)
        outs_s.append((k.reshape(DB, T, n_heads, HEAD_DIM), v.reshape(DB, T, n_heads, HEAD_DIM),
                       r3(logf), h_T.reshape(DB, d_lru), hist_T))
        xs = _out_proj(xs, ya.reshape(DB * T, d_attn), yl.reshape(DB * T, d_lru), wa, wl, lng, lnb,
                       alpha, _tile(DB * T, 256))

    stack = lambda outs, i: jnp.stack([o[i] for o in outs], 0)
    return (xp.reshape(B, S, d_model), xs.reshape(DB, T, d_model),
            stack(outs_p, 0), stack(outs_p, 1), stack(outs_p, 2), stack(outs_p, 3), stack(outs_p, 4),
            stack(outs_s, 0), stack(outs_s, 1), stack(outs_s, 2), stack(outs_s, 3), stack(outs_s, 4))
```

```python
import functools

import jax
import jax.numpy as jnp
from jax import lax
from jax.experimental import pallas as pl
from jax.experimental.pallas import tpu as pltpu

F32 = jnp.float32
BF16 = jnp.bfloat16

HEAD_DIM = 128
LRU_BLOCK = 128
CONV_W = 4
LRU_C = 8.0
LN_EPS = 1e-5
RMS_EPS = 1e-6
LOG2E = 1.4426950408889634
Q_SCALE = HEAD_DIM ** -0.5 * LOG2E

LANES = 128
SUBLANES = 8
V7X_VMEM_LIMIT = 56 * 2 ** 20
OUT_PROJ_ROWS = 128


def _dot(a, b):
    return jnp.dot(a, b, preferred_element_type=F32)


def _dot_nt(a, b):
    return lax.dot_general(a, b, (((1,), (1,)), ((), ())), preferred_element_type=F32)


def _softplus(y):
    return jnp.maximum(y, 0.0) + jnp.log1p(jnp.exp(-jnp.abs(y)))


def _log_sigmoid(y):
    return -_softplus(-y)


def _sigmoid(y):
    return 1.0 / (1.0 + jnp.exp2(y * -LOG2E))


def _silu(y):
    return y * _sigmoid(y)


def _group_rms(y):
    return y * lax.rsqrt(jnp.mean(y * y, axis=-1, keepdims=True) + RMS_EPS)


def _params(*semantics):
    return pltpu.CompilerParams(dimension_semantics=semantics, vmem_limit_bytes=V7X_VMEM_LIMIT)


def _resident(shape):
    return pl.BlockSpec(shape, lambda *_: (0,) * len(shape), pipeline_mode=pl.Buffered(1))


def _in_proj_kernel(x_ref, wq_ref, wk_ref, wv_ref, wf_ref, wga_ref, wxl_ref, wgl_ref, bf_ref,
                    q_ref, k_ref, v_ref, kb_ref, vb_ref, logf_ref, ga_ref, xl_ref, gl_ref):
    n_heads = logf_ref.shape[-1]
    xb = x_ref[...].astype(BF16)
    q_ref[...] = (_dot(xb, wq_ref[...]) * Q_SCALE).astype(BF16)
    k = _dot(xb, wk_ref[...])
    k_ref[...] = k
    kb_ref[...] = k.astype(BF16)
    v = _dot(xb, wv_ref[...])
    v_ref[...] = v
    vb_ref[...] = v.astype(BF16)
    zf = _dot(xb, wf_ref[...]) + bf_ref[...]
    logf_ref[...] = _log_sigmoid(zf)[:, :n_heads]
    ga_ref[...] = _silu(_dot(xb, wga_ref[...]))
    xl_ref[...] = _dot(xb, wxl_ref[...])
    gl_ref[...] = _silu(_dot(xb, wgl_ref[...]))


def _in_proj(x2d, wts, tm):
    M, D = x2d.shape
    wq, wk, wv, wf, wga, wxl, wgl, bf, n_heads = wts
    d_attn, d_lru = wq.shape[1], wxl.shape[1]
    row = lambda n: pl.BlockSpec((tm, n), lambda i: (i, 0))
    out_shape = (
        jax.ShapeDtypeStruct((M, d_attn), BF16),
        jax.ShapeDtypeStruct((M, d_attn), F32),
        jax.ShapeDtypeStruct((M, d_attn), F32),
        jax.ShapeDtypeStruct((M, d_attn), BF16),
        jax.ShapeDtypeStruct((M, d_attn), BF16),
        jax.ShapeDtypeStruct((M, n_heads), F32),
        jax.ShapeDtypeStruct((M, d_attn), F32),
        jax.ShapeDtypeStruct((M, d_lru), F32),
        jax.ShapeDtypeStruct((M, d_lru), F32),
    )
    return pl.pallas_call(
        _in_proj_kernel,
        grid=(M // tm,),
        in_specs=[row(D)] + [_resident(w.shape) for w in (wq, wk, wv, wf, wga, wxl, wgl, bf)],
        out_specs=tuple(row(s.shape[1]) for s in out_shape),
        out_shape=out_shape,
        compiler_params=_params("parallel"),
        name="in_proj",
    )(x2d, wq, wk, wv, wf, wga, wxl, wgl, bf)


def _cumsum_lanes(x):
    n = x.shape[-1]
    lane = lax.broadcasted_iota(jnp.int32, x.shape, x.ndim - 1)
    step = 1
    while step < n:
        x = x + jnp.where(lane >= step, pltpu.roll(x, step, x.ndim - 1), 0.0)
        step *= 2
    return x


def _rows(ref):
    return ref[...].reshape(ref.shape[0] * ref.shape[1], ref.shape[2])


def _cumsum_kernel(f_ref, c_ref):
    c_ref[...] = (_cumsum_lanes(_rows(f_ref)) * LOG2E).reshape(c_ref.shape)


def _cumsum_carry_kernel(past_ref, new_ref, cpast_ref, cnew_ref):
    cp = _cumsum_lanes(_rows(past_ref))
    cpast_ref[...] = (cp * LOG2E).reshape(cpast_ref.shape)
    cn = (_cumsum_lanes(_rows(new_ref)) + cp[:, cp.shape[1] - 1:]) * LOG2E
    cnew_ref[...] = cn.reshape(cnew_ref.shape)


CUMSUM_STREAMS = 8


def _cumsum(fT):
    B, H, S = fT.shape
    nb = _tile(B, CUMSUM_STREAMS)
    spec = pl.BlockSpec((nb, H, S), lambda b: (b, 0, 0))
    return pl.pallas_call(
        _cumsum_kernel, grid=(B // nb,), in_specs=[spec], out_specs=spec,
        out_shape=jax.ShapeDtypeStruct(fT.shape, F32),
        compiler_params=_params("parallel"), name="cumsum",
    )(fT)


def _cumsum_carry(pastT, newT):
    B, H, P = pastT.shape
    N = newT.shape[2]
    nb = _tile(B, CUMSUM_STREAMS)
    pspec = pl.BlockSpec((nb, H, P), lambda b: (b, 0, 0))
    nspec = pl.BlockSpec((nb, H, N), lambda b: (b, 0, 0))
    return pl.pallas_call(
        _cumsum_carry_kernel, grid=(B // nb,), in_specs=[pspec, nspec], out_specs=(pspec, nspec),
        out_shape=(jax.ShapeDtypeStruct(pastT.shape, F32), jax.ShapeDtypeStruct(newT.shape, F32)),
        compiler_params=_params("parallel"), name="cumsum_carry",
    )(pastT, newT)


def _head_epilogue(acc, l, gain, gate):
    normed = acc * lax.rsqrt(jnp.mean(acc * acc, axis=-1, keepdims=True) + RMS_EPS * (l * l))
    return (normed * gain * gate).astype(BF16)


def _fox_prompt_kernel(q_ref, kb_ref, vp_ref, cT_ref, ga_ref, gattn_ref, ya_ref,
                       m_s, acc_s, *, tq):
    n_heads = cT_ref.shape[1]
    qi = pl.program_id(1)
    q0 = pl.multiple_of(qi * tq, tq)

    def values(h, k0, rows):
        return vp_ref[0, pl.ds(k0, rows), 2 * h * HEAD_DIM:(2 * h + 2) * HEAD_DIM]

    def scores(h, k0):
        hs = slice(h * HEAD_DIM, (h + 1) * HEAD_DIM)
        s = _dot_nt(q_ref[0, :, hs], kb_ref[0, pl.ds(k0, tq), hs])
        return s - cT_ref[0, h:h + 1, pl.ds(k0, tq)]

    def row_max(s):
        m = s[:, :LANES]
        for c in range(1, s.shape[1] // LANES):
            m = jnp.maximum(m, s[:, c * LANES:(c + 1) * LANES])
        return jnp.max(m, axis=-1, keepdims=True)

    def weights(s, m):
        parts = [jnp.exp2(s[:, c * LANES:(c + 1) * LANES] - m) for c in range(s.shape[1] // LANES)]
        return jnp.concatenate(parts, axis=1).astype(BF16)

    half = tq // 2
    visible_a = (lax.broadcasted_iota(jnp.int32, (tq, half), 1)
                 <= lax.broadcasted_iota(jnp.int32, (tq, half), 0))
    visible_b = (lax.broadcasted_iota(jnp.int32, (half, half), 1)
                 <= lax.broadcasted_iota(jnp.int32, (half, half), 0))

    def diag_scores(h):
        hs = slice(h * HEAD_DIM, (h + 1) * HEAD_DIM)
        sa = _dot_nt(q_ref[0, :, hs], kb_ref[0, pl.ds(q0, half), hs])
        sb = _dot_nt(q_ref[0, half:, hs], kb_ref[0, pl.ds(q0 + half, half), hs])
        return (sa - cT_ref[0, h:h + 1, pl.ds(q0, half)],
                sb - cT_ref[0, h:h + 1, pl.ds(q0 + half, half)])

    s_next = diag_scores(0)
    for h in range(n_heads):
        sa = jnp.where(visible_a, s_next[0], -jnp.inf)
        sb = jnp.where(visible_b, s_next[1], -jnp.inf)
        if h + 1 < n_heads:
            s_next = diag_scores(h + 1)
        s_top = sa[:half]
        s_bot = jnp.concatenate([sa[half:], sb], axis=1)
        m_top = jnp.broadcast_to(row_max(s_top), (half, LANES))
        m_bot = jnp.broadcast_to(row_max(s_bot), (half, LANES))
        m_s[h, :half, :] = m_top
        m_s[h, half:, :] = m_bot
        acc_s[h, :half, :] = _dot(weights(s_top, m_top), values(h, q0, half))
        acc_s[h, half:, :] = _dot(weights(s_bot, m_bot), values(h, q0, tq))

    def body(j, carry):
        k0 = pl.multiple_of(j * tq, tq)
        s_next = scores(0, k0)
        for h in range(n_heads):
            s = s_next
            if h + 1 < n_heads:
                s_next = scores(h + 1, k0)
            m_old = m_s[h]
            m_new = jnp.maximum(m_old, row_max(s))
            alpha = jnp.exp2(m_old - m_new)
            m_s[h] = m_new
            pv = _dot(weights(s, m_new), values(h, k0, tq))
            acc_s[h, :, :HEAD_DIM] = alpha * acc_s[h, :, :HEAD_DIM] + pv[:, :HEAD_DIM]
            acc_s[h, :, HEAD_DIM:] = alpha * acc_s[h, :, HEAD_DIM:] + pv[:, HEAD_DIM:]
        return carry

    lax.fori_loop(0, qi, body, 0)

    for h in range(n_heads):
        hs = slice(h * HEAD_DIM, (h + 1) * HEAD_DIM)
        ya_ref[0, :, hs] = _head_epilogue(acc_s[h, :, :HEAD_DIM], acc_s[h, :, HEAD_DIM:],
                                          gattn_ref[:, hs], ga_ref[0, :, hs])


def _fox_prompt(q, kb, vp, cT, ga, gattn, tq):
    B, S, DA = q.shape
    H = cT.shape[1]
    blk = pl.BlockSpec((1, tq, DA), lambda b, i: (b, i, 0))
    seq = lambda n: pl.BlockSpec((1, S, n), lambda b, i: (b, 0, 0))
    return pl.pallas_call(
        functools.partial(_fox_prompt_kernel, tq=tq),
        grid=(B, S // tq),
        in_specs=[blk, seq(DA), seq(vp.shape[2]), pl.BlockSpec((1, H, S), lambda b, i: (b, 0, 0)),
                  blk, _resident(gattn.shape)],
        out_specs=blk,
        out_shape=jax.ShapeDtypeStruct((B, S, DA), BF16),
        scratch_shapes=[pltpu.VMEM((H, tq, LANES), F32),
                        pltpu.VMEM((H, tq, 2 * HEAD_DIM), F32)],
        compiler_params=_params("parallel", "arbitrary"),
        name="fox_prompt",
    )(q, kb, vp, cT, ga, gattn)


def _fox_sample_kernel(q_ref, ck_hbm, cv_hbm, cpT_ref, kn_ref, vn_ref, cnT_ref, cnc_ref, ga_ref,
                       gattn_ref, ya_ref, kbuf, vbuf, sems, m_s, l_s, acc_s):
    n_heads = cpT_ref.shape[1]
    T = q_ref.shape[1]
    tk = cpT_ref.shape[2]
    b, j = pl.program_id(0), pl.program_id(1)
    nj = pl.num_programs(1)
    step = b * nj + j
    slot = lax.rem(step, 2)

    def block_copies(stream, blk, sl):
        rows = pl.ds(blk * tk, tk)
        copies = []
        for h in range(n_heads):
            copies.append(pltpu.make_async_copy(ck_hbm.at[stream, rows, h, :], kbuf.at[sl, h],
                                                sems.at[0, sl, h]))
            copies.append(pltpu.make_async_copy(cv_hbm.at[stream, rows, h, :], vbuf.at[sl, h],
                                                sems.at[1, sl, h]))
        return copies

    @pl.when(step == 0)
    def _():
        for c in block_copies(b, j, slot):
            c.start()

    @pl.when(step + 1 < pl.num_programs(0) * nj)
    def _():
        last = j + 1 == nj
        for c in block_copies(jnp.where(last, b + 1, b), jnp.where(last, 0, j + 1), 1 - slot):
            c.start()

    for c in block_copies(b, j, slot):
        c.wait()

    @pl.when(j == 0)
    def _():
        m_s[...] = jnp.full(m_s.shape, -jnp.inf, F32)
        l_s[...] = jnp.zeros(l_s.shape, F32)
        acc_s[...] = jnp.zeros(acc_s.shape, F32)

    def update(h, s, vals):
        m_old = m_s[h]
        m_new = jnp.maximum(m_old, jnp.max(s, axis=-1, keepdims=True))
        alpha = jnp.exp2(m_old - m_new)
        p = jnp.exp2(s - m_new)
        m_s[h] = m_new
        l_s[h] = alpha * l_s[h] + jnp.sum(p, axis=-1, keepdims=True)
        acc_s[h] = alpha * acc_s[h] + _dot(p.astype(BF16), vals)

    def cached_scores(h):
        hs = slice(h * HEAD_DIM, (h + 1) * HEAD_DIM)
        kh = kbuf[slot, h].astype(BF16)
        return _dot_nt(q_ref[0, :, hs], kh) + (cnc_ref[0, :, h:h + 1] - cpT_ref[0, h:h + 1, :])

    s_next = cached_scores(0)
    for h in range(n_heads):
        s = s_next
        if h + 1 < n_heads:
            s_next = cached_scores(h + 1)
        update(h, s, vbuf[slot, h].astype(BF16))

    @pl.when(j == pl.num_programs(1) - 1)
    def _():
        row = lax.broadcasted_iota(jnp.int32, (T, T), 0)
        col = lax.broadcasted_iota(jnp.int32, (T, T), 1)
        heads = [slice(h * HEAD_DIM, (h + 1) * HEAD_DIM) for h in range(n_heads)]
        scores = [_dot_nt(q_ref[0, :, hs], kn_ref[0, :, hs])
                  + (cnc_ref[0, :, h:h + 1] - cnT_ref[0, h:h + 1, :T]) for h, hs in enumerate(heads)]
        for h, hs in enumerate(heads):
            update(h, jnp.where(col <= row, scores[h], -jnp.inf), vn_ref[0, :, hs])
        for h, hs in enumerate(heads):
            ya_ref[0, :, hs] = _head_epilogue(acc_s[h], l_s[h], gattn_ref[:, hs], ga_ref[0, :, hs])


def _fox_sample(q, ck, cv, cpT, kn, vn, cnT, cnc, ga, gattn, tk):
    B, T, DA = q.shape
    H = cpT.shape[1]
    P = ck.shape[1]
    new = pl.BlockSpec((1, T, DA), lambda b, j: (b, 0, 0))
    hbm = pl.BlockSpec(memory_space=pl.ANY)
    return pl.pallas_call(
        _fox_sample_kernel,
        grid=(B, P // tk),
        in_specs=[new, hbm, hbm,
                  pl.BlockSpec((1, H, tk), lambda b, j: (b, 0, j)),
                  new, new,
                  pl.BlockSpec((1, H, cnT.shape[2]), lambda b, j: (b, 0, 0)),
                  pl.BlockSpec((1, T, H), lambda b, j: (b, 0, 0)),
                  new, _resident(gattn.shape)],
        out_specs=new,
        out_shape=jax.ShapeDtypeStruct((B, T, DA), BF16),
        scratch_shapes=[pltpu.VMEM((2, H, tk, HEAD_DIM), ck.dtype),
                        pltpu.VMEM((2, H, tk, HEAD_DIM), cv.dtype),
                        pltpu.SemaphoreType.DMA((2, 2, H)),
                        pltpu.VMEM((H, T, 1), F32), pltpu.VMEM((H, T, 1), F32),
                        pltpu.VMEM((H, T, HEAD_DIM), F32)],
        compiler_params=_params("arbitrary", "arbitrary"),
        name="fox_sample",
    )(q, ck, cv, cpT, kn, vn, cnT, cnc, ga, gattn)


def _segment_pitch(seg_len):
    pitch = -(-seg_len // SUBLANES) * SUBLANES
    return pitch if (pitch // SUBLANES) % 2 else pitch + SUBLANES


def _rglru_init(t, hist_ref, h0_ref, tail_s, hcar_s):
    @pl.when(t == 0)
    def _():
        tail_s[...] = jnp.zeros(tail_s.shape, F32)
        tail_s[SUBLANES - (CONV_W - 1):, :] = hist_ref[...]
        hcar_s[...] = h0_ref[...]


def _rglru_block(t, xl_ref, gl_ref, cw_ref, cb_ref, wr_ref, br_ref, wi_ref, bi_ref,
                 lam_ref, glru_ref, yl_ref, hT_ref, hist_out_ref, xn_s, hn_s, hl_s, ac_s, tail_s, hcar_s,
                 *, tt, reset_first):
    n_blocks = wr_ref.shape[0]
    seg = tt // SUBLANES
    pitch = _segment_pitch(seg)
    sub = lax.broadcasted_iota(jnp.int32, (SUBLANES, LRU_BLOCK), 0)
    first = (lax.broadcasted_iota(jnp.int32, (tt, LRU_BLOCK), 0) == 0) & (t == 0)
    decay = -LRU_C * _softplus(-lam_ref[...])

    def shift_segments(v, head):
        return jnp.where(sub == 0, head, pltpu.roll(v, 1, 0))

    gated = []
    for n in range(n_blocks):
        ns = slice(n * LRU_BLOCK, (n + 1) * LRU_BLOCK)
        for s in range(SUBLANES):
            xn_s[n, s * pitch:s * pitch + seg, :] = xl_ref[s * seg:(s + 1) * seg, ns]
        x = [xn_s[n, pl.ds(i, SUBLANES, stride=pitch), :] for i in range(seg)]
        before = [shift_segments(x[seg - k], tail_s[SUBLANES - k:SUBLANES - k + 1, ns])
                  for k in range(CONV_W - 1, 0, -1)]
        xs = jnp.concatenate(before + x, axis=0)
        xc = cb_ref[:, ns] + xs[0:tt] * cw_ref[0:1, ns]
        for j in range(1, CONV_W):
            xc = xc + xs[j * SUBLANES:j * SUBLANES + tt] * cw_ref[j:j + 1, ns]

        xcb = xc.astype(BF16)
        gated.append((xc, _dot(xcb, wr_ref[n]), _dot(xcb, wi_ref[n])))

    tail_s[...] = xl_ref[tt - SUBLANES:, :]
    hist_out_ref[...] = xl_ref[tt - (CONV_W - 1):, :]
    _rglru_recurrence(gated, sub, first, decay, shift_segments, br_ref, bi_ref, glru_ref, gl_ref,
                      yl_ref, hT_ref, hn_s, hl_s, ac_s, hcar_s, seg=seg, pitch=pitch,
                      reset_first=reset_first)


def _rglru_recurrence(gated, sub, first, decay, shift_segments, br_ref, bi_ref, glru_ref, gl_ref,
                      yl_ref, hT_ref, hn_s, hl_s, ac_s, hcar_s, *, seg, pitch, reset_first):
    for n, (xc, zr, zi) in enumerate(gated):
        ns = slice(n * LRU_BLOCK, (n + 1) * LRU_BLOCK)
        r = _sigmoid(zr + br_ref[:, ns])
        i_gate = _sigmoid(zi + bi_ref[:, ns])
        log_a = r * decay[:, ns]
        a = jnp.exp(log_a)
        one_minus_a2 = -jnp.tanh(log_a) * (1.0 + a * a)
        mult = jnp.where(one_minus_a2 > 0.0, one_minus_a2 * lax.rsqrt(one_minus_a2), 0.0)
        if reset_first:
            mult = jnp.where(first, 1.0, mult)
        u = mult * i_gate * xc

        rows = lambda v, i: v[i * SUBLANES:(i + 1) * SUBLANES]
        e, p = rows(u, 0), rows(a, 0)
        hl_s[n, 0:SUBLANES, :] = e
        ac_s[n, 0:SUBLANES, :] = p
        for i in range(1, seg):
            e = rows(a, i) * e + rows(u, i)
            p = rows(a, i) * p
            hl_s[n, i * SUBLANES:(i + 1) * SUBLANES, :] = e
            ac_s[n, i * SUBLANES:(i + 1) * SUBLANES, :] = p
        for step in (1, 2, 4):
            keep = sub >= step
            e = e + p * jnp.where(keep, pltpu.roll(e, step, 0), 0.0)
            p = p * jnp.where(keep, pltpu.roll(p, step, 0), 1.0)
        h_end = e + p * hcar_s[:, ns]
        carry_in = shift_segments(h_end, hcar_s[:, ns])
        hcar_s[:, ns] = h_end[SUBLANES - 1:, :]
        hT_ref[:, ns] = h_end[SUBLANES - 1:, :]

        for i in range(seg):
            hn_s[n, pl.ds(i, SUBLANES, stride=pitch), :] = (
                hl_s[n, i * SUBLANES:(i + 1) * SUBLANES, :]
                + ac_s[n, i * SUBLANES:(i + 1) * SUBLANES, :] * carry_in)
        h = jnp.concatenate([hn_s[n, s * pitch:s * pitch + seg, :] for s in range(SUBLANES)], axis=0)
        y = _group_rms(h) * glru_ref[:, ns] * gl_ref[:, ns]
        yl_ref[:, ns] = y.astype(BF16)


def _rglru_kernel(xl_ref, gl_ref, hist_ref, h0_ref, *rest, tt, reset_first):
    weights, (yl_ref, hT_ref, hist_out_ref), scratch = rest[:8], rest[8:11], rest[11:]
    xn_s, hn_s, hl_s, ac_s, tail_s, hcar_s = scratch
    t = pl.program_id(1)
    for g in range(xl_ref.shape[0]):
        _rglru_init(t, hist_ref.at[g], h0_ref.at[g], tail_s.at[g], hcar_s.at[g])
        _rglru_block(t, xl_ref.at[g], gl_ref.at[g], *weights, yl_ref.at[g], hT_ref.at[g],
                     hist_out_ref.at[g], xn_s, hn_s, hl_s, ac_s, tail_s.at[g], hcar_s.at[g],
                     tt=tt, reset_first=reset_first)


def _in_proj_rglru_kernel(x_ref, wq_ref, wk_ref, wv_ref, wf_ref, wga_ref, wxl_ref, wgl_ref, bf_ref,
                          hist_ref, h0_ref, *rest, tt, reset_first):
    lru_w = rest[:8]
    (q_ref, k_ref, v_ref, kb_ref, vp_ref, ga_ref, yl_ref, hT_ref, hist_out_ref,
     logf_t_ref) = rest[8:18]
    xl_s, gl_s, *lru_scratch = rest[18:]
    n_heads = logf_t_ref.shape[1]
    t = pl.program_id(1)
    _rglru_init(t, hist_ref.at[0], h0_ref.at[0], *lru_scratch[4:])
    xb = x_ref[0].astype(BF16)
    xl_s[...] = _dot(xb, wxl_ref[...])
    gl_s[...] = _silu(_dot(xb, wgl_ref[...]))
    _rglru_block(t, xl_s, gl_s, *lru_w, yl_ref.at[0], hT_ref.at[0], hist_out_ref.at[0],
                 *lru_scratch, tt=tt, reset_first=reset_first)
    q_ref[0] = (_dot(xb, wq_ref[...]) * Q_SCALE).astype(BF16)
    k = _dot(xb, wk_ref[...])
    k_ref[0] = k
    kb_ref[0] = k.astype(BF16)
    v = _dot(xb, wv_ref[...])
    v_ref[0] = v
    ones = jnp.ones((tt, HEAD_DIM), BF16)
    for h in range(n_heads):
        vp_ref[0, :, 2 * h * HEAD_DIM:(2 * h + 1) * HEAD_DIM] = (
            v[:, h * HEAD_DIM:(h + 1) * HEAD_DIM].astype(BF16))
        vp_ref[0, :, (2 * h + 1) * HEAD_DIM:(2 * h + 2) * HEAD_DIM] = ones
    logf = _log_sigmoid(_dot(xb, wf_ref[...]) + bf_ref[...])
    logf_t_ref[0] = logf.T[:n_heads, :]
    ga_ref[0] = _silu(_dot(xb, wga_ref[...]))


def _in_proj_rglru(x, wts, hist, h0, lw, tt, reset_first):
    B, S, D = x.shape
    wq, wk, wv, wf, wga, wxl, wgl, bf, n_heads = wts
    d_attn, d_lru = wq.shape[1], wxl.shape[1]
    blk = lambda n: pl.BlockSpec((1, tt, n), lambda b, t: (b, t, 0))
    per_b = lambda n: pl.BlockSpec((1, n, d_lru), lambda b, t: (b, 0, 0))
    slab_rows = SUBLANES * _segment_pitch(tt // SUBLANES)
    seq = lambda n, dt: jax.ShapeDtypeStruct((B, S, n), dt)
    out_shape = (seq(d_attn, BF16), seq(d_attn, F32), seq(d_attn, F32), seq(d_attn, BF16),
                 seq(2 * d_attn, BF16), seq(d_attn, F32), seq(d_lru, BF16),
                 jax.ShapeDtypeStruct((B, 1, d_lru), F32),
                 jax.ShapeDtypeStruct((B, CONV_W - 1, d_lru), F32),
                 jax.ShapeDtypeStruct((B, n_heads, S), F32))
    return pl.pallas_call(
        functools.partial(_in_proj_rglru_kernel, tt=tt, reset_first=reset_first),
        grid=(B, S // tt),
        in_specs=[blk(D)] + [_resident(w.shape) for w in (wq, wk, wv, wf, wga, wxl, wgl, bf)]
                 + [per_b(CONV_W - 1), per_b(1)] + [_resident(w.shape) for w in lw],
        out_specs=tuple(blk(s.shape[2]) for s in out_shape[:7]) + (per_b(1), per_b(CONV_W - 1))
                  + (pl.BlockSpec((1, n_heads, tt), lambda b, t: (b, 0, t)),),
        out_shape=out_shape,
        scratch_shapes=[pltpu.VMEM((tt, d_lru), F32), pltpu.VMEM((tt, d_lru), F32),
                        pltpu.VMEM((lw[2].shape[0], slab_rows, LRU_BLOCK), F32),
                        pltpu.VMEM((lw[2].shape[0], slab_rows, LRU_BLOCK), F32),
                        pltpu.VMEM((lw[2].shape[0], tt, LRU_BLOCK), F32),
                        pltpu.VMEM((lw[2].shape[0], tt, LRU_BLOCK), F32),
                        pltpu.VMEM((SUBLANES, d_lru), F32), pltpu.VMEM((1, d_lru), F32)],
        compiler_params=_params("parallel", "arbitrary"),
        name="in_proj_rglru",
    )(x, wq, wk, wv, wf, wga, wxl, wgl, bf, hist, h0, *lw)


RGLRU_STREAMS = 8


def _rglru(xl, gl, hist, h0, lw, tt, reset_first):
    B, T, DL = xl.shape
    cw, cb, wr, br, wi, bi, lam, glru = lw
    G = _tile(B, RGLRU_STREAMS)
    blk = pl.BlockSpec((G, tt, DL), lambda b, t: (b, t, 0))
    per_b = lambda n: pl.BlockSpec((G, n, DL), lambda b, t: (b, 0, 0))
    slab_rows = SUBLANES * _segment_pitch(tt // SUBLANES)
    return pl.pallas_call(
        functools.partial(_rglru_kernel, tt=tt, reset_first=reset_first),
        grid=(B // G, T // tt),
        in_specs=[blk, blk, per_b(CONV_W - 1), per_b(1)]
                 + [_resident(w.shape) for w in (cw, cb, wr, br, wi, bi, lam, glru)],
        out_specs=(blk, per_b(1), per_b(CONV_W - 1)),
        out_shape=(jax.ShapeDtypeStruct((B, T, DL), BF16),
                   jax.ShapeDtypeStruct((B, 1, DL), F32),
                   jax.ShapeDtypeStruct((B, CONV_W - 1, DL), F32)),
        scratch_shapes=[pltpu.VMEM((wr.shape[0], slab_rows, LRU_BLOCK), F32),
                        pltpu.VMEM((wr.shape[0], slab_rows, LRU_BLOCK), F32),
                        pltpu.VMEM((wr.shape[0], tt, LRU_BLOCK), F32),
                        pltpu.VMEM((wr.shape[0], tt, LRU_BLOCK), F32),
                        pltpu.VMEM((G, SUBLANES, DL), F32), pltpu.VMEM((G, 1, DL), F32)],
        compiler_params=_params("parallel", "arbitrary"),
        name="rglru",
    )(xl, gl, hist, h0, cw, cb, wr, br, wi, bi, lam, glru)


def _out_proj_kernel(x_ref, ya_ref, yl_ref, wa_ref, wl_ref, g_ref, b_ref, o_ref, *, alpha):
    tm = x_ref.shape[0]
    rows = [slice(r, r + OUT_PROJ_ROWS) for r in range(0, tm, OUT_PROJ_ROWS)]
    outs = [_dot(ya_ref[r, :], wa_ref[...]) + _dot(yl_ref[r, :], wl_ref[...]) for r in rows]
    for r, out in zip(rows, outs):
        h = alpha * x_ref[r, :] + out
        mu = jnp.mean(h, axis=-1, keepdims=True)
        d = h - mu
        var = jnp.mean(d * d, axis=-1, keepdims=True)
        o_ref[r, :] = d * lax.rsqrt(var + LN_EPS) * g_ref[...] + b_ref[...]


def _out_proj(x2d, ya, yl, wa, wl, g, b, alpha, tm):
    M, D = x2d.shape
    row = lambda n: pl.BlockSpec((tm, n), lambda i: (i, 0))
    return pl.pallas_call(
        functools.partial(_out_proj_kernel, alpha=alpha),
        grid=(M // tm,),
        in_specs=[row(D), row(ya.shape[1]), row(yl.shape[1])]
                 + [_resident(w.shape) for w in (wa, wl, g, b)],
        out_specs=row(D),
        out_shape=jax.ShapeDtypeStruct((M, D), F32),
        compiler_params=_params("parallel"),
        name="out_proj",
    )(x2d, ya, yl, wa, wl, g, b)


def _split_weights_kernel(wt_hbm, wo_hbm, wq_ref, wk_ref, wv_ref, wf_ref, wga_ref, wxl_ref, wgl_ref,
                          wa_ref, wl_ref, buf, fbuf, sems, *, splits):
    n_heads = splits[3] - splits[2]
    half = wa_ref.shape[0]
    pieces = ((wt_hbm, 0, wq_ref, True), (wt_hbm, splits[0], wk_ref, True),
              (wt_hbm, splits[1], wv_ref, True), (wt_hbm, splits[3], wga_ref, True),
              (wt_hbm, splits[4], wxl_ref, True), (wt_hbm, splits[5], wgl_ref, True),
              (wo_hbm, 0, wa_ref, False), (wo_hbm, half, wl_ref, False))

    def piece_rows(i):
        _, _, out, transposed = pieces[i]
        return out.shape[1] if transposed else out.shape[0]

    def piece_copy(i):
        src, start, _, _ = pieces[i]
        rows = piece_rows(i)
        return pltpu.make_async_copy(src.at[pl.ds(start, rows), :], buf.at[i % 2, :rows, :],
                                     sems.at[i % 2])

    forget_copy = pltpu.make_async_copy(wt_hbm.at[pl.ds(splits[2], n_heads), :],
                                        fbuf.at[:n_heads, :], sems.at[2])
    fbuf[...] = jnp.zeros(fbuf.shape, F32)
    forget_copy.start()
    piece_copy(0).start()
    for i, (_, _, out, transposed) in enumerate(pieces):
        if i + 1 < len(pieces):
            piece_copy(i + 1).start()
        piece_copy(i).wait()
        rows = buf[i % 2, :piece_rows(i), :]
        out[...] = (rows.T if transposed else rows).astype(BF16)
    forget_copy.wait()
    wf_ref[...] = fbuf[...].T.astype(BF16)


def _split_weights(wt, wo, splits, d_attn):
    DIN, D = wt.shape
    widths = (splits[0], splits[1] - splits[0], splits[2] - splits[1], LANES,
              splits[4] - splits[3], splits[5] - splits[4], DIN - splits[5])
    assert wo.shape == (2 * d_attn, D), "w_out is split into two equal row halves"
    hbm = pl.BlockSpec(memory_space=pl.ANY)
    return pl.pallas_call(
        functools.partial(_split_weights_kernel, splits=splits),
        in_specs=[hbm, hbm],
        out_shape=tuple(jax.ShapeDtypeStruct((D, n), BF16) for n in widths)
                  + (jax.ShapeDtypeStruct((d_attn, D), BF16),) * 2,
        scratch_shapes=[pltpu.VMEM((2, max(max(widths), d_attn), D), F32),
                        pltpu.VMEM((LANES, D), F32), pltpu.SemaphoreType.DMA((3,))],
        compiler_params=pltpu.CompilerParams(vmem_limit_bytes=V7X_VMEM_LIMIT),
        name="split_weights",
    )(wt, wo)


def _tile(n, pref):
    return pref if n % pref == 0 else n


def kernel(x_prompt, x_sample, cache_k, cache_v, cache_logf, state_h, state_conv, w_in, b_f, conv_w,
           conv_b, w_r, b_r, w_i, b_i, lru_lambda, g_attn, g_lru, w_out, ln_g, ln_b):
    depth, d_model, _ = w_in.shape
    n_heads = b_f.shape[1]
    d_attn = n_heads * HEAD_DIM
    d_lru = lru_lambda.shape[1]
    alpha = (2.0 * depth) ** 0.25
    B, S, _ = x_prompt.shape
    DB, T, _ = x_sample.shape
    P = cache_k.shape[2]

    xp = x_prompt.reshape(B * S, d_model)
    xs = x_sample.reshape(DB * T, d_model)
    outs_p, outs_s = [], []
    for l in range(depth):
        o = (d_attn, 2 * d_attn, 3 * d_attn, 3 * d_attn + n_heads, 4 * d_attn + n_heads,
             4 * d_attn + n_heads + d_lru)
        bf = jnp.pad(b_f[l][None, :], ((0, 0), (0, LANES - n_heads)))
        *w_proj, wa, wl = _split_weights(jnp.swapaxes(w_in, 1, 2)[l], w_out[l], o, d_attn)
        wts = tuple(w_proj) + (bf, n_heads)
        lw = (conv_w[l], conv_b[l][None], w_r[l].astype(BF16), b_r[l][None], w_i[l].astype(BF16),
              b_i[l][None], lru_lambda[l][None], g_lru[l][None])
        gattn = g_attn[l][None]
        assert d_lru == d_attn, "w_out is split into two equal row halves"
        lng, lnb = ln_g[l][None], ln_b[l][None]

        q, k, v, kb, vp, ga, yl, h_T, hist_T, logf_t = _in_proj_rglru(
            xp.reshape(B, S, d_model), wts, jnp.zeros((B, CONV_W - 1, d_lru), F32),
            jnp.zeros((B, 1, d_lru), F32), lw, _tile(S, 256), True)
        cT = _cumsum(logf_t)
        ya = _fox_prompt(q, kb, vp, cT, ga, gattn, _tile(S, 512))
        outs_p.append((k.reshape(B, S, n_heads, HEAD_DIM), v.reshape(B, S, n_heads, HEAD_DIM),
                       jnp.transpose(logf_t, (0, 2, 1)), h_T.reshape(B, d_lru), hist_T))
        xp = _out_proj(xp, ya.reshape(B * S, d_attn), yl.reshape(B * S, d_lru), wa, wl, lng, lnb,
                       alpha, _tile(B * S, 512))

        q, k, v, kb, vb, logf, ga, xl, gl = _in_proj(xs, wts, _tile(DB * T, 256))
        r3 = lambda a: a.reshape(DB, T, a.shape[-1])
        newT = jnp.pad(jnp.transpose(r3(logf), (0, 2, 1)), ((0, 0), (0, 0), (0, LANES - T)))
        cpT, cnT = _cumsum_carry(jnp.transpose(cache_logf[l].astype(F32), (0, 2, 1)), newT)
        cnc = jnp.transpose(cnT[:, :, :T], (0, 2, 1))
        ya = _fox_sample(r3(q), cache_k[l], cache_v[l],
                         cpT, r3(kb), r3(vb), cnT, cnc, r3(ga), gattn, _tile(P, 2048))
        yl, h_T, hist_T = _rglru(r3(xl), r3(gl), state_conv[l], state_h[l][:, None, :], lw,
                                 _tile(T, 256), False)
        outs_s.append((k.reshape(DB, T, n_heads, HEAD_DIM), v.reshape(DB, T, n_heads, HEAD_DIM),
                       r3(logf), h_T.reshape(DB, d_lru), hist_T))
        xs = _out_proj(xs, ya.reshape(DB * T, d_attn), yl.reshape(DB * T, d_lru), wa, wl, lng, lnb,
                       alpha, _tile(DB * T, 256))

    stack = lambda outs, i: jnp.stack([o[i] for o in outs], 0)
    return (xp.reshape(B, S, d_model), xs.reshape(DB, T, d_model),
            stack(outs_p, 0), stack(outs_p, 1), stack(outs_p, 2), stack(outs_p, 3), stack(outs_p, 4),
            stack(outs_s, 0), stack(outs_s, 1), stack(outs_s, 2), stack(outs_s, 3), stack(outs_s, 4))
```

```python
import functools

import jax
import jax.numpy as jnp
from jax import lax
from jax.experimental import pallas as pl
from jax.experimental.pallas import tpu as pltpu

F32 = jnp.float32
BF16 = jnp.bfloat16

HEAD_DIM = 128
LRU_BLOCK = 128
CONV_W = 4
LRU_C = 8.0
LN_EPS = 1e-5
RMS_EPS = 1e-6
LOG2E = 1.4426950408889634
Q_SCALE = HEAD_DIM ** -0.5 * LOG2E

LANES = 128
SUBLANES = 8
V7X_VMEM_LIMIT = 56 * 2 ** 20
OUT_PROJ_ROWS = 128
CACHE_RING = 3


def _dot(a, b):
    return jnp.dot(a, b, preferred_element_type=F32)


def _dot_nt(a, b):
    return lax.dot_general(a, b, (((1,), (1,)), ((), ())), preferred_element_type=F32)


def _softplus(y):
    return jnp.maximum(y, 0.0) + jnp.log1p(jnp.exp(-jnp.abs(y)))


def _log_sigmoid(y):
    return -_softplus(-y)


def _sigmoid(y):
    return 1.0 / (1.0 + jnp.exp2(y * -LOG2E))


def _silu(y):
    return y * _sigmoid(y)


def _group_rms(y):
    return y * lax.rsqrt(jnp.mean(y * y, axis=-1, keepdims=True) + RMS_EPS)


def _params(*semantics):
    return pltpu.CompilerParams(dimension_semantics=semantics, vmem_limit_bytes=V7X_VMEM_LIMIT)


def _resident(shape):
    return pl.BlockSpec(shape, lambda *_: (0,) * len(shape), pipeline_mode=pl.Buffered(1))


def _in_proj_kernel(x_ref, wq_ref, wk_ref, wv_ref, wf_ref, wga_ref, wxl_ref, wgl_ref, bf_ref,
                    q_ref, k_ref, v_ref, kb_ref, vb_ref, logf_ref, ga_ref, xl_ref, gl_ref):
    n_heads = logf_ref.shape[-1]
    xb = x_ref[...].astype(BF16)
    q_ref[...] = (_dot(xb, wq_ref[...]) * Q_SCALE).astype(BF16)
    k = _dot(xb, wk_ref[...])
    k_ref[...] = k
    kb_ref[...] = k.astype(BF16)
    v = _dot(xb, wv_ref[...])
    v_ref[...] = v
    vb_ref[...] = v.astype(BF16)
    zf = _dot(xb, wf_ref[...]) + bf_ref[...]
    logf_ref[...] = _log_sigmoid(zf)[:, :n_heads]
    ga_ref[...] = _silu(_dot(xb, wga_ref[...]))
    xl_ref[...] = _dot(xb, wxl_ref[...])
    gl_ref[...] = _silu(_dot(xb, wgl_ref[...]))


def _in_proj(x2d, wts, tm):
    M, D = x2d.shape
    wq, wk, wv, wf, wga, wxl, wgl, bf, n_heads = wts
    d_attn, d_lru = wq.shape[1], wxl.shape[1]
    row = lambda n: pl.BlockSpec((tm, n), lambda i: (i, 0))
    out_shape = (
        jax.ShapeDtypeStruct((M, d_attn), BF16),
        jax.ShapeDtypeStruct((M, d_attn), F32),
        jax.ShapeDtypeStruct((M, d_attn), F32),
        jax.ShapeDtypeStruct((M, d_attn), BF16),
        jax.ShapeDtypeStruct((M, d_attn), BF16),
        jax.ShapeDtypeStruct((M, n_heads), F32),
        jax.ShapeDtypeStruct((M, d_attn), F32),
        jax.ShapeDtypeStruct((M, d_lru), F32),
        jax.ShapeDtypeStruct((M, d_lru), F32),
    )
    return pl.pallas_call(
        _in_proj_kernel,
        grid=(M // tm,),
        in_specs=[row(D)] + [_resident(w.shape) for w in (wq, wk, wv, wf, wga, wxl, wgl, bf)],
        out_specs=tuple(row(s.shape[1]) for s in out_shape),
        out_shape=out_shape,
        compiler_params=_params("parallel"),
        name="in_proj",
    )(x2d, wq, wk, wv, wf, wga, wxl, wgl, bf)


def _cumsum_lanes(x):
    n = x.shape[-1]
    lane = lax.broadcasted_iota(jnp.int32, x.shape, x.ndim - 1)
    step = 1
    while step < n:
        x = x + jnp.where(lane >= step, pltpu.roll(x, step, x.ndim - 1), 0.0)
        step *= 2
    return x


def _rows(ref):
    return ref[...].reshape(ref.shape[0] * ref.shape[1], ref.shape[2])


def _cumsum_kernel(f_ref, c_ref):
    c_ref[...] = (_cumsum_lanes(_rows(f_ref)) * LOG2E).reshape(c_ref.shape)


def _cumsum_carry_kernel(past_ref, new_ref, cpast_ref, cnew_ref):
    cp = _cumsum_lanes(_rows(past_ref))
    cpast_ref[...] = (cp * LOG2E).reshape(cpast_ref.shape)
    cn = (_cumsum_lanes(_rows(new_ref)) + cp[:, cp.shape[1] - 1:]) * LOG2E
    cnew_ref[...] = cn.reshape(cnew_ref.shape)


CUMSUM_STREAMS = 8


def _cumsum(fT):
    B, H, S = fT.shape
    nb = _tile(B, CUMSUM_STREAMS)
    spec = pl.BlockSpec((nb, H, S), lambda b: (b, 0, 0))
    return pl.pallas_call(
        _cumsum_kernel, grid=(B // nb,), in_specs=[spec], out_specs=spec,
        out_shape=jax.ShapeDtypeStruct(fT.shape, F32),
        compiler_params=_params("parallel"), name="cumsum",
    )(fT)


def _cumsum_carry(pastT, newT):
    B, H, P = pastT.shape
    N = newT.shape[2]
    nb = _tile(B, CUMSUM_STREAMS)
    pspec = pl.BlockSpec((nb, H, P), lambda b: (b, 0, 0))
    nspec = pl.BlockSpec((nb, H, N), lambda b: (b, 0, 0))
    return pl.pallas_call(
        _cumsum_carry_kernel, grid=(B // nb,), in_specs=[pspec, nspec], out_specs=(pspec, nspec),
        out_shape=(jax.ShapeDtypeStruct(pastT.shape, F32), jax.ShapeDtypeStruct(newT.shape, F32)),
        compiler_params=_params("parallel"), name="cumsum_carry",
    )(pastT, newT)


def _head_epilogue(acc, l, gain, gate):
    normed = acc * lax.rsqrt(jnp.mean(acc * acc, axis=-1, keepdims=True) + RMS_EPS * (l * l))
    return (normed * gain * gate).astype(BF16)


def _fox_prompt_kernel(q_ref, kb_ref, vp_ref, cT_ref, ga_ref, gattn_ref, ya_ref,
                       m_s, acc_s, *, tq):
    n_heads = cT_ref.shape[1]
    qi = pl.program_id(1)
    q0 = pl.multiple_of(qi * tq, tq)

    def values(h, k0, rows):
        return vp_ref[0, pl.ds(k0, rows), 2 * h * HEAD_DIM:(2 * h + 2) * HEAD_DIM]

    def scores(h, k0):
        hs = slice(h * HEAD_DIM, (h + 1) * HEAD_DIM)
        s = _dot_nt(q_ref[0, :, hs], kb_ref[0, pl.ds(k0, tq), hs])
        return s - cT_ref[0, h:h + 1, pl.ds(k0, tq)]

    def row_max(s):
        m = s[:, :LANES]
        for c in range(1, s.shape[1] // LANES):
            m = jnp.maximum(m, s[:, c * LANES:(c + 1) * LANES])
        return jnp.max(m, axis=-1, keepdims=True)

    def weights(s, m):
        parts = [jnp.exp2(s[:, c * LANES:(c + 1) * LANES] - m) for c in range(s.shape[1] // LANES)]
        return jnp.concatenate(parts, axis=1).astype(BF16)

    half = tq // 2
    visible_a = (lax.broadcasted_iota(jnp.int32, (tq, half), 1)
                 <= lax.broadcasted_iota(jnp.int32, (tq, half), 0))
    visible_b = (lax.broadcasted_iota(jnp.int32, (half, half), 1)
                 <= lax.broadcasted_iota(jnp.int32, (half, half), 0))

    def diag_scores(h):
        hs = slice(h * HEAD_DIM, (h + 1) * HEAD_DIM)
        sa = _dot_nt(q_ref[0, :, hs], kb_ref[0, pl.ds(q0, half), hs])
        sb = _dot_nt(q_ref[0, half:, hs], kb_ref[0, pl.ds(q0 + half, half), hs])
        return (sa - cT_ref[0, h:h + 1, pl.ds(q0, half)],
                sb - cT_ref[0, h:h + 1, pl.ds(q0 + half, half)])

    s_next = diag_scores(0)
    for h in range(n_heads):
        sa = jnp.where(visible_a, s_next[0], -jnp.inf)
        sb = jnp.where(visible_b, s_next[1], -jnp.inf)
        if h + 1 < n_heads:
            s_next = diag_scores(h + 1)
        s_top = sa[:half]
        s_bot = jnp.concatenate([sa[half:], sb], axis=1)
        m_top = jnp.broadcast_to(row_max(s_top), (half, LANES))
        m_bot = jnp.broadcast_to(row_max(s_bot), (half, LANES))
        m_s[h, :half, :] = m_top
        m_s[h, half:, :] = m_bot
        acc_s[h, :half, :] = _dot(weights(s_top, m_top), values(h, q0, half))
        acc_s[h, half:, :] = _dot(weights(s_bot, m_bot), values(h, q0, tq))

    def body(j, carry):
        k0 = pl.multiple_of(j * tq, tq)
        s_next = scores(0, k0)
        for h in range(n_heads):
            s = s_next
            if h + 1 < n_heads:
                s_next = scores(h + 1, k0)
            m_old = m_s[h]
            m_new = jnp.maximum(m_old, row_max(s))
            alpha = jnp.exp2(m_old - m_new)
            m_s[h] = m_new
            pv = _dot(weights(s, m_new), values(h, k0, tq))
            acc_s[h, :, :HEAD_DIM] = alpha * acc_s[h, :, :HEAD_DIM] + pv[:, :HEAD_DIM]
            acc_s[h, :, HEAD_DIM:] = alpha * acc_s[h, :, HEAD_DIM:] + pv[:, HEAD_DIM:]
        return carry

    lax.fori_loop(0, qi, body, 0)

    for h in range(n_heads):
        hs = slice(h * HEAD_DIM, (h + 1) * HEAD_DIM)
        ya_ref[0, :, hs] = _head_epilogue(acc_s[h, :, :HEAD_DIM], acc_s[h, :, HEAD_DIM:],
                                          gattn_ref[:, hs], ga_ref[0, :, hs])


def _fox_prompt(q, kb, vp, cT, ga, gattn, tq):
    B, S, DA = q.shape
    H = cT.shape[1]
    blk = pl.BlockSpec((1, tq, DA), lambda b, i: (b, i, 0))
    seq = lambda n: pl.BlockSpec((1, S, n), lambda b, i: (b, 0, 0))
    return pl.pallas_call(
        functools.partial(_fox_prompt_kernel, tq=tq),
        grid=(B, S // tq),
        in_specs=[blk, seq(DA), seq(vp.shape[2]), pl.BlockSpec((1, H, S), lambda b, i: (b, 0, 0)),
                  blk, _resident(gattn.shape)],
        out_specs=blk,
        out_shape=jax.ShapeDtypeStruct((B, S, DA), BF16),
        scratch_shapes=[pltpu.VMEM((H, tq, LANES), F32),
                        pltpu.VMEM((H, tq, 2 * HEAD_DIM), F32)],
        compiler_params=_params("parallel", "arbitrary"),
        name="fox_prompt",
    )(q, kb, vp, cT, ga, gattn)


def _fox_sample_kernel(q_ref, ck_hbm, cv_hbm, cpT_ref, kn_ref, vn_ref, cnT_ref, cnc_ref, ga_ref,
                       gattn_ref, ya_ref, kbuf, vbuf, sems, m_s, l_s, acc_s):
    n_heads = cpT_ref.shape[1]
    T = q_ref.shape[1]
    tk = cpT_ref.shape[2]
    ring = kbuf.shape[0]
    b, j = pl.program_id(0), pl.program_id(1)
    nj = pl.num_programs(1)
    total = pl.num_programs(0) * nj
    step = b * nj + j
    slot = lax.rem(step, ring)

    def block_copies(s):
        stream, blk, sl = s // nj, lax.rem(s, nj), lax.rem(s, ring)
        rows = pl.ds(blk * tk, tk)
        copies = []
        for h in range(n_heads):
            copies.append(pltpu.make_async_copy(ck_hbm.at[stream, rows, h, :], kbuf.at[sl, h],
                                                sems.at[0, sl, h]))
            copies.append(pltpu.make_async_copy(cv_hbm.at[stream, rows, h, :], vbuf.at[sl, h],
                                                sems.at[1, sl, h]))
        return copies

    for ahead in range(ring - 1):
        @pl.when((step == 0) & (ahead < total))
        def _(ahead=ahead):
            for c in block_copies(step + ahead):
                c.start()

    @pl.when(step + ring - 1 < total)
    def _():
        for c in block_copies(step + ring - 1):
            c.start()

    for c in block_copies(step):
        c.wait()

    @pl.when(j == 0)
    def _():
        m_s[...] = jnp.full(m_s.shape, -jnp.inf, F32)
        l_s[...] = jnp.zeros(l_s.shape, F32)
        acc_s[...] = jnp.zeros(acc_s.shape, F32)

    def update(h, s, vals):
        m_old = m_s[h]
        m_new = jnp.maximum(m_old, jnp.max(s, axis=-1, keepdims=True))
        alpha = jnp.exp2(m_old - m_new)
        p = jnp.exp2(s - m_new)
        m_s[h] = m_new
        l_s[h] = alpha * l_s[h] + jnp.sum(p, axis=-1, keepdims=True)
        acc_s[h] = alpha * acc_s[h] + _dot(p.astype(BF16), vals)

    def cached_scores(h):
        hs = slice(h * HEAD_DIM, (h + 1) * HEAD_DIM)
        kh = kbuf[slot, h].astype(BF16)
        return _dot_nt(q_ref[0, :, hs], kh) + (cnc_ref[0, :, h:h + 1] - cpT_ref[0, h:h + 1, :])

    s_next = cached_scores(0)
    for h in range(n_heads):
        s = s_next
        if h + 1 < n_heads:
            s_next = cached_scores(h + 1)
        update(h, s, vbuf[slot, h].astype(BF16))

    @pl.when(j == pl.num_programs(1) - 1)
    def _():
        row = lax.broadcasted_iota(jnp.int32, (T, T), 0)
        col = lax.broadcasted_iota(jnp.int32, (T, T), 1)
        heads = [slice(h * HEAD_DIM, (h + 1) * HEAD_DIM) for h in range(n_heads)]
        scores = [_dot_nt(q_ref[0, :, hs], kn_ref[0, :, hs])
                  + (cnc_ref[0, :, h:h + 1] - cnT_ref[0, h:h + 1, :T]) for h, hs in enumerate(heads)]
        for h, hs in enumerate(heads):
            update(h, jnp.where(col <= row, scores[h], -jnp.inf), vn_ref[0, :, hs])
        for h, hs in enumerate(heads):
            ya_ref[0, :, hs] = _head_epilogue(acc_s[h], l_s[h], gattn_ref[:, hs], ga_ref[0, :, hs])


def _fox_sample(q, ck, cv, cpT, kn, vn, cnT, cnc, ga, gattn, tk):
    B, T, DA = q.shape
    H = cpT.shape[1]
    P = ck.shape[1]
    new = pl.BlockSpec((1, T, DA), lambda b, j: (b, 0, 0))
    hbm = pl.BlockSpec(memory_space=pl.ANY)
    return pl.pallas_call(
        _fox_sample_kernel,
        grid=(B, P // tk),
        in_specs=[new, hbm, hbm,
                  pl.BlockSpec((1, H, tk), lambda b, j: (b, 0, j)),
                  new, new,
                  pl.BlockSpec((1, H, cnT.shape[2]), lambda b, j: (b, 0, 0)),
                  pl.BlockSpec((1, T, H), lambda b, j: (b, 0, 0)),
                  new, _resident(gattn.shape)],
        out_specs=new,
        out_shape=jax.ShapeDtypeStruct((B, T, DA), BF16),
        scratch_shapes=[pltpu.VMEM((CACHE_RING, H, tk, HEAD_DIM), ck.dtype),
                        pltpu.VMEM((CACHE_RING, H, tk, HEAD_DIM), cv.dtype),
                        pltpu.SemaphoreType.DMA((2, CACHE_RING, H)),
                        pltpu.VMEM((H, T, 1), F32), pltpu.VMEM((H, T, 1), F32),
                        pltpu.VMEM((H, T, HEAD_DIM), F32)],
        compiler_params=_params("arbitrary", "arbitrary"),
        name="fox_sample",
    )(q, ck, cv, cpT, kn, vn, cnT, cnc, ga, gattn)


def _segment_pitch(seg_len):
    pitch = -(-seg_len // SUBLANES) * SUBLANES
    return pitch if (pitch // SUBLANES) % 2 else pitch + SUBLANES


def _rglru_init(t, hist_ref, h0_ref, tail_s, hcar_s):
    @pl.when(t == 0)
    def _():
        tail_s[...] = jnp.zeros(tail_s.shape, F32)
        tail_s[SUBLANES - (CONV_W - 1):, :] = hist_ref[...]
        hcar_s[...] = h0_ref[...]


def _rglru_block(t, xl_ref, gl_ref, cw_ref, cb_ref, wr_ref, br_ref, wi_ref, bi_ref,
                 lam_ref, glru_ref, yl_ref, hT_ref, hist_out_ref, xn_s, hn_s, hl_s, ac_s, tail_s, hcar_s,
                 *, tt, reset_first):
    n_blocks = wr_ref.shape[0]
    seg = tt // SUBLANES
    pitch = _segment_pitch(seg)
    sub = lax.broadcasted_iota(jnp.int32, (SUBLANES, LRU_BLOCK), 0)
    first = (lax.broadcasted_iota(jnp.int32, (tt, LRU_BLOCK), 0) == 0) & (t == 0)
    decay = -LRU_C * _softplus(-lam_ref[...])

    def shift_segments(v, head):
        return jnp.where(sub == 0, head, pltpu.roll(v, 1, 0))

    gated = []
    for n in range(n_blocks):
        ns = slice(n * LRU_BLOCK, (n + 1) * LRU_BLOCK)
        for s in range(SUBLANES):
            xn_s[n, s * pitch:s * pitch + seg, :] = xl_ref[s * seg:(s + 1) * seg, ns]
        x = [xn_s[n, pl.ds(i, SUBLANES, stride=pitch), :] for i in range(seg)]
        before = [shift_segments(x[seg - k], tail_s[SUBLANES - k:SUBLANES - k + 1, ns])
                  for k in range(CONV_W - 1, 0, -1)]
        xs = jnp.concatenate(before + x, axis=0)
        xc = cb_ref[:, ns] + xs[0:tt] * cw_ref[0:1, ns]
        for j in range(1, CONV_W):
            xc = xc + xs[j * SUBLANES:j * SUBLANES + tt] * cw_ref[j:j + 1, ns]

        xcb = xc.astype(BF16)
        gated.append((xc, _dot(xcb, wr_ref[n]), _dot(xcb, wi_ref[n])))

    tail_s[...] = xl_ref[tt - SUBLANES:, :]
    hist_out_ref[...] = xl_ref[tt - (CONV_W - 1):, :]
    _rglru_recurrence(gated, sub, first, decay, shift_segments, br_ref, bi_ref, glru_ref, gl_ref,
                      yl_ref, hT_ref, hn_s, hl_s, ac_s, hcar_s, seg=seg, pitch=pitch,
                      reset_first=reset_first)


def _rglru_recurrence(gated, sub, first, decay, shift_segments, br_ref, bi_ref, glru_ref, gl_ref,
                      yl_ref, hT_ref, hn_s, hl_s, ac_s, hcar_s, *, seg, pitch, reset_first):
    for n, (xc, zr, zi) in enumerate(gated):
        ns = slice(n * LRU_BLOCK, (n + 1) * LRU_BLOCK)
        r = _sigmoid(zr + br_ref[:, ns])
        i_gate = _sigmoid(zi + bi_ref[:, ns])
        log_a = r * decay[:, ns]
        a = jnp.exp(log_a)
        one_minus_a2 = -jnp.tanh(log_a) * (1.0 + a * a)
        mult = jnp.where(one_minus_a2 > 0.0, one_minus_a2 * lax.rsqrt(one_minus_a2), 0.0)
        if reset_first:
            mult = jnp.where(first, 1.0, mult)
        u = mult * i_gate * xc

        rows = lambda v, i: v[i * SUBLANES:(i + 1) * SUBLANES]
        e, p = rows(u, 0), rows(a, 0)
        hl_s[n, 0:SUBLANES, :] = e
        ac_s[n, 0:SUBLANES, :] = p
        for i in range(1, seg):
            e = rows(a, i) * e + rows(u, i)
            p = rows(a, i) * p
            hl_s[n, i * SUBLANES:(i + 1) * SUBLANES, :] = e
            ac_s[n, i * SUBLANES:(i + 1) * SUBLANES, :] = p
        for step in (1, 2, 4):
            keep = sub >= step
            e = e + p * jnp.where(keep, pltpu.roll(e, step, 0), 0.0)
            p = p * jnp.where(keep, pltpu.roll(p, step, 0), 1.0)
        h_end = e + p * hcar_s[:, ns]
        carry_in = shift_segments(h_end, hcar_s[:, ns])
        hcar_s[:, ns] = h_end[SUBLANES - 1:, :]
        hT_ref[:, ns] = h_end[SUBLANES - 1:, :]

        for i in range(seg):
            hn_s[n, pl.ds(i, SUBLANES, stride=pitch), :] = (
                hl_s[n, i * SUBLANES:(i + 1) * SUBLANES, :]
                + ac_s[n, i * SUBLANES:(i + 1) * SUBLANES, :] * carry_in)
        h = jnp.concatenate([hn_s[n, s * pitch:s * pitch + seg, :] for s in range(SUBLANES)], axis=0)
        y = _group_rms(h) * glru_ref[:, ns] * gl_ref[:, ns]
        yl_ref[:, ns] = y.astype(BF16)


def _rglru_kernel(xl_ref, gl_ref, hist_ref, h0_ref, *rest, tt, reset_first):
    weights, (yl_ref, hT_ref, hist_out_ref), scratch = rest[:8], rest[8:11], rest[11:]
    xn_s, hn_s, hl_s, ac_s, tail_s, hcar_s = scratch
    t = pl.program_id(1)
    for g in range(xl_ref.shape[0]):
        _rglru_init(t, hist_ref.at[g], h0_ref.at[g], tail_s.at[g], hcar_s.at[g])
        _rglru_block(t, xl_ref.at[g], gl_ref.at[g], *weights, yl_ref.at[g], hT_ref.at[g],
                     hist_out_ref.at[g], xn_s, hn_s, hl_s, ac_s, tail_s.at[g], hcar_s.at[g],
                     tt=tt, reset_first=reset_first)


def _in_proj_rglru_kernel(x_ref, wq_ref, wk_ref, wv_ref, wf_ref, wga_ref, wxl_ref, wgl_ref, bf_ref,
                          hist_ref, h0_ref, *rest, tt, reset_first):
    lru_w = rest[:8]
    (q_ref, k_ref, v_ref, kb_ref, vp_ref, ga_ref, yl_ref, hT_ref, hist_out_ref,
     logf_t_ref) = rest[8:18]
    xl_s, gl_s, *lru_scratch = rest[18:]
    n_heads = logf_t_ref.shape[1]
    t = pl.program_id(1)
    _rglru_init(t, hist_ref.at[0], h0_ref.at[0], *lru_scratch[4:])
    xb = x_ref[0].astype(BF16)
    xl_s[...] = _dot(xb, wxl_ref[...])
    gl_s[...] = _silu(_dot(xb, wgl_ref[...]))
    _rglru_block(t, xl_s, gl_s, *lru_w, yl_ref.at[0], hT_ref.at[0], hist_out_ref.at[0],
                 *lru_scratch, tt=tt, reset_first=reset_first)
    q_ref[0] = (_dot(xb, wq_ref[...]) * Q_SCALE).astype(BF16)
    k = _dot(xb, wk_ref[...])
    k_ref[0] = k
    kb_ref[0] = k.astype(BF16)
    v = _dot(xb, wv_ref[...])
    v_ref[0] = v
    ones = jnp.ones((tt, HEAD_DIM), BF16)
    for h in range(n_heads):
        vp_ref[0, :, 2 * h * HEAD_DIM:(2 * h + 1) * HEAD_DIM] = (
            v[:, h * HEAD_DIM:(h + 1) * HEAD_DIM].astype(BF16))
        vp_ref[0, :, (2 * h + 1) * HEAD_DIM:(2 * h + 2) * HEAD_DIM] = ones
    logf = _log_sigmoid(_dot(xb, wf_ref[...]) + bf_ref[...])
    logf_t_ref[0] = logf.T[:n_heads, :]
    ga_ref[0] = _silu(_dot(xb, wga_ref[...]))


def _in_proj_rglru(x, wts, hist, h0, lw, tt, reset_first):
    B, S, D = x.shape
    wq, wk, wv, wf, wga, wxl, wgl, bf, n_heads = wts
    d_attn, d_lru = wq.shape[1], wxl.shape[1]
    blk = lambda n: pl.BlockSpec((1, tt, n), lambda b, t: (b, t, 0))
    per_b = lambda n: pl.BlockSpec((1, n, d_lru), lambda b, t: (b, 0, 0))
    slab_rows = SUBLANES * _segment_pitch(tt // SUBLANES)
    seq = lambda n, dt: jax.ShapeDtypeStruct((B, S, n), dt)
    out_shape = (seq(d_attn, BF16), seq(d_attn, F32), seq(d_attn, F32), seq(d_attn, BF16),
                 seq(2 * d_attn, BF16), seq(d_attn, F32), seq(d_lru, BF16),
                 jax.ShapeDtypeStruct((B, 1, d_lru), F32),
                 jax.ShapeDtypeStruct((B, CONV_W - 1, d_lru), F32),
                 jax.ShapeDtypeStruct((B, n_heads, S), F32))
    return pl.pallas_call(
        functools.partial(_in_proj_rglru_kernel, tt=tt, reset_first=reset_first),
        grid=(B, S // tt),
        in_specs=[blk(D)] + [_resident(w.shape) for w in (wq, wk, wv, wf, wga, wxl, wgl, bf)]
                 + [per_b(CONV_W - 1), per_b(1)] + [_resident(w.shape) for w in lw],
        out_specs=tuple(blk(s.shape[2]) for s in out_shape[:7]) + (per_b(1), per_b(CONV_W - 1))
                  + (pl.BlockSpec((1, n_heads, tt), lambda b, t: (b, 0, t)),),
        out_shape=out_shape,
        scratch_shapes=[pltpu.VMEM((tt, d_lru), F32), pltpu.VMEM((tt, d_lru), F32),
                        pltpu.VMEM((lw[2].shape[0], slab_rows, LRU_BLOCK), F32),
                        pltpu.VMEM((lw[2].shape[0], slab_rows, LRU_BLOCK), F32),
                        pltpu.VMEM((lw[2].shape[0], tt, LRU_BLOCK), F32),
                        pltpu.VMEM((lw[2].shape[0], tt, LRU_BLOCK), F32),
                        pltpu.VMEM((SUBLANES, d_lru), F32), pltpu.VMEM((1, d_lru), F32)],
        compiler_params=_params("parallel", "arbitrary"),
        name="in_proj_rglru",
    )(x, wq, wk, wv, wf, wga, wxl, wgl, bf, hist, h0, *lw)


RGLRU_STREAMS = 8


def _rglru(xl, gl, hist, h0, lw, tt, reset_first):
    B, T, DL = xl.shape
    cw, cb, wr, br, wi, bi, lam, glru = lw
    G = _tile(B, RGLRU_STREAMS)
    blk = pl.BlockSpec((G, tt, DL), lambda b, t: (b, t, 0))
    per_b = lambda n: pl.BlockSpec((G, n, DL), lambda b, t: (b, 0, 0))
    slab_rows = SUBLANES * _segment_pitch(tt // SUBLANES)
    return pl.pallas_call(
        functools.partial(_rglru_kernel, tt=tt, reset_first=reset_first),
        grid=(B // G, T // tt),
        in_specs=[blk, blk, per_b(CONV_W - 1), per_b(1)]
                 + [_resident(w.shape) for w in (cw, cb, wr, br, wi, bi, lam, glru)],
        out_specs=(blk, per_b(1), per_b(CONV_W - 1)),
        out_shape=(jax.ShapeDtypeStruct((B, T, DL), BF16),
                   jax.ShapeDtypeStruct((B, 1, DL), F32),
                   jax.ShapeDtypeStruct((B, CONV_W - 1, DL), F32)),
        scratch_shapes=[pltpu.VMEM((wr.shape[0], slab_rows, LRU_BLOCK), F32),
                        pltpu.VMEM((wr.shape[0], slab_rows, LRU_BLOCK), F32),
                        pltpu.VMEM((wr.shape[0], tt, LRU_BLOCK), F32),
                        pltpu.VMEM((wr.shape[0], tt, LRU_BLOCK), F32),
                        pltpu.VMEM((G, SUBLANES, DL), F32), pltpu.VMEM((G, 1, DL), F32)],
        compiler_params=_params("parallel", "arbitrary"),
        name="rglru",
    )(xl, gl, hist, h0, cw, cb, wr, br, wi, bi, lam, glru)


def _out_proj_kernel(x_ref, ya_ref, yl_ref, wa_ref, wl_ref, g_ref, b_ref, o_ref, *, alpha):
    tm = x_ref.shape[0]
    rows = [slice(r, r + OUT_PROJ_ROWS) for r in range(0, tm, OUT_PROJ_ROWS)]
    outs = [_dot(ya_ref[r, :], wa_ref[...]) + _dot(yl_ref[r, :], wl_ref[...]) for r in rows]
    for r, out in zip(rows, outs):
        h = alpha * x_ref[r, :] + out
        mu = jnp.mean(h, axis=-1, keepdims=True)
        d = h - mu
        var = jnp.mean(d * d, axis=-1, keepdims=True)
        o_ref[r, :] = d * lax.rsqrt(var + LN_EPS) * g_ref[...] + b_ref[...]


def _out_proj(x2d, ya, yl, wa, wl, g, b, alpha, tm):
    M, D = x2d.shape
    row = lambda n: pl.BlockSpec((tm, n), lambda i: (i, 0))
    return pl.pallas_call(
        functools.partial(_out_proj_kernel, alpha=alpha),
        grid=(M // tm,),
        in_specs=[row(D), row(ya.shape[1]), row(yl.shape[1])]
                 + [_resident(w.shape) for w in (wa, wl, g, b)],
        out_specs=row(D),
        out_shape=jax.ShapeDtypeStruct((M, D), F32),
        compiler_params=_params("parallel"),
        name="out_proj",
    )(x2d, ya, yl, wa, wl, g, b)


def _split_weights_kernel(wt_hbm, wo_hbm, wq_ref, wk_ref, wv_ref, wf_ref, wga_ref, wxl_ref, wgl_ref,
                          wa_ref, wl_ref, buf, fbuf, sems, *, splits):
    n_heads = splits[3] - splits[2]
    half = wa_ref.shape[0]
    pieces = ((wt_hbm, 0, wq_ref, True), (wt_hbm, splits[0], wk_ref, True),
              (wt_hbm, splits[1], wv_ref, True), (wt_hbm, splits[3], wga_ref, True),
              (wt_hbm, splits[4], wxl_ref, True), (wt_hbm, splits[5], wgl_ref, True),
              (wo_hbm, 0, wa_ref, False), (wo_hbm, half, wl_ref, False))

    def piece_rows(i):
        _, _, out, transposed = pieces[i]
        return out.shape[1] if transposed else out.shape[0]

    def piece_copy(i):
        src, start, _, _ = pieces[i]
        rows = piece_rows(i)
        return pltpu.make_async_copy(src.at[pl.ds(start, rows), :], buf.at[i % 2, :rows, :],
                                     sems.at[i % 2])

    forget_copy = pltpu.make_async_copy(wt_hbm.at[pl.ds(splits[2], n_heads), :],
                                        fbuf.at[:n_heads, :], sems.at[2])
    fbuf[...] = jnp.zeros(fbuf.shape, F32)
    forget_copy.start()
    piece_copy(0).start()
    for i, (_, _, out, transposed) in enumerate(pieces):
        if i + 1 < len(pieces):
            piece_copy(i + 1).start()
        piece_copy(i).wait()
        rows = buf[i % 2, :piece_rows(i), :]
        out[...] = (rows.T if transposed else rows).astype(BF16)
    forget_copy.wait()
    wf_ref[...] = fbuf[...].T.astype(BF16)


def _split_weights(wt, wo, splits, d_attn):
    DIN, D = wt.shape
    widths = (splits[0], splits[1] - splits[0], splits[2] - splits[1], LANES,
              splits[4] - splits[3], splits[5] - splits[4], DIN - splits[5])
    assert wo.shape == (2 * d_attn, D), "w_out is split into two equal row halves"
    hbm = pl.BlockSpec(memory_space=pl.ANY)
    return pl.pallas_call(
        functools.partial(_split_weights_kernel, splits=splits),
        in_specs=[hbm, hbm],
        out_shape=tuple(jax.ShapeDtypeStruct((D, n), BF16) for n in widths)
                  + (jax.ShapeDtypeStruct((d_attn, D), BF16),) * 2,
        scratch_shapes=[pltpu.VMEM((2, max(max(widths), d_attn), D), F32),
                        pltpu.VMEM((LANES, D), F32), pltpu.SemaphoreType.DMA((3,))],
        compiler_params=pltpu.CompilerParams(vmem_limit_bytes=V7X_VMEM_LIMIT),
        name="split_weights",
    )(wt, wo)


def _tile(n, pref):
    return pref if n % pref == 0 else n


def kernel(x_prompt, x_sample, cache_k, cache_v, cache_logf, state_h, state_conv, w_in, b_f, conv_w,
           conv_b, w_r, b_r, w_i, b_i, lru_lambda, g_attn, g_lru, w_out, ln_g, ln_b):
    depth, d_model, _ = w_in.shape
    n_heads = b_f.shape[1]
    d_attn = n_heads * HEAD_DIM
    d_lru = lru_lambda.shape[1]
    alpha = (2.0 * depth) ** 0.25
    B, S, _ = x_prompt.shape
    DB, T, _ = x_sample.shape
    P = cache_k.shape[2]

    xp = x_prompt.reshape(B * S, d_model)
    xs = x_sample.reshape(DB * T, d_model)
    outs_p, outs_s = [], []
    for l in range(depth):
        o = (d_attn, 2 * d_attn, 3 * d_attn, 3 * d_attn + n_heads, 4 * d_attn + n_heads,
             4 * d_attn + n_heads + d_lru)
        bf = jnp.pad(b_f[l][None, :], ((0, 0), (0, LANES - n_heads)))
        *w_proj, wa, wl = _split_weights(jnp.swapaxes(w_in, 1, 2)[l], w_out[l], o, d_attn)
        wts = tuple(w_proj) + (bf, n_heads)
        lw = (conv_w[l], conv_b[l][None], w_r[l].astype(BF16), b_r[l][None], w_i[l].astype(BF16),
              b_i[l][None], lru_lambda[l][None], g_lru[l][None])
        gattn = g_attn[l][None]
        assert d_lru == d_attn, "w_out is split into two equal row halves"
        lng, lnb = ln_g[l][None], ln_b[l][None]

        q, k, v, kb, vp, ga, yl, h_T, hist_T, logf_t = _in_proj_rglru(
            xp.reshape(B, S, d_model), wts, jnp.zeros((B, CONV_W - 1, d_lru), F32),
            jnp.zeros((B, 1, d_lru), F32), lw, _tile(S, 256), True)
        cT = _cumsum(logf_t)
        ya = _fox_prompt(q, kb, vp, cT, ga, gattn, _tile(S, 512))
        outs_p.append((k.reshape(B, S, n_heads, HEAD_DIM), v.reshape(B, S, n_heads, HEAD_DIM),
                       jnp.transpose(logf_t, (0, 2, 1)), h_T.reshape(B, d_lru), hist_T))
        xp = _out_proj(xp, ya.reshape(B * S, d_attn), yl.reshape(B * S, d_lru), wa, wl, lng, lnb,
                       alpha, _tile(B * S, 512))

        q, k, v, kb, vb, logf, ga, xl, gl = _in_proj(xs, wts, _tile(DB * T, 256))
        r3 = lambda a: a.reshape(DB, T, a.shape[-1])
        newT = jnp.pad(jnp.transpose(r3(logf), (0, 2, 1)), ((0, 0), (0, 0), (0, LANES - T)))
        cpT, cnT = _cumsum_carry(jnp.transpose(cache_logf[l].astype(F32), (0, 2, 1)), newT)
        cnc = jnp.transpose(cnT[:, :, :T], (0, 2, 1))
        ya = _fox_sample(r3(q), cache_k[l], cache_v[l],
                         cpT, r3(kb), r3(vb), cnT, cnc, r3(ga), gattn, _tile(P, 2048))
        yl, h_T, hist_T = _rglru(r3(xl), r3(gl), state_conv[l], state_h[l][:, None, :], lw,
                                 _tile(T, 256), False)
        outs_s.append((k.reshape(DB, T, n_heads, HEAD_DIM), v.reshape(DB, T, n_heads, HEAD_DIM),
                       r3(logf), h_T.reshape(DB, d_lru), hist_T))
        xs = _out_proj(xs, ya.reshape(DB * T, d_attn), yl.reshape(DB * T, d_lru), wa, wl, lng, lnb,
                       alpha, _tile(DB * T, 256))

    stack = lambda outs, i: jnp.stack([o[i] for o in outs], 0)
    return (xp.reshape(B, S, d_model), xs.reshape(DB, T, d_model),
            stack(outs_p, 0), stack(outs_p, 1), stack(outs_p, 2), stack(outs_p, 3), stack(outs_p, 4),
            stack(outs_s, 0), stack(outs_s, 1), stack(outs_s, 2), stack(outs_s, 3), stack(outs_s, 4))
```

```python
import functools

import jax
import jax.numpy as jnp
from jax import lax
from jax.experimental import pallas as pl
from jax.experimental.pallas import tpu as pltpu

F32 = jnp.float32
BF16 = jnp.bfloat16

HEAD_DIM = 128
LRU_BLOCK = 128
CONV_W = 4
LRU_C = 8.0
LN_EPS = 1e-5
RMS_EPS = 1e-6
LOG2E = 1.4426950408889634
Q_SCALE = HEAD_DIM ** -0.5 * LOG2E

LANES = 128
SUBLANES = 8
V7X_VMEM_LIMIT = 56 * 2 ** 20
OUT_PROJ_ROWS = 128
CACHE_RING = 3


def _dot(a, b):
    return jnp.dot(a, b, preferred_element_type=F32)


def _dot_nt(a, b):
    return lax.dot_general(a, b, (((1,), (1,)), ((), ())), preferred_element_type=F32)


def _softplus(y):
    return jnp.maximum(y, 0.0) + jnp.log1p(jnp.exp(-jnp.abs(y)))


def _log_sigmoid(y):
    return -_softplus(-y)


def _sigmoid(y):
    return 1.0 / (1.0 + jnp.exp2(y * -LOG2E))


def _silu(y):
    return y * _sigmoid(y)


def _group_rms(y):
    return y * lax.rsqrt(jnp.mean(y * y, axis=-1, keepdims=True) + RMS_EPS)


def _params(*semantics):
    return pltpu.CompilerParams(dimension_semantics=semantics, vmem_limit_bytes=V7X_VMEM_LIMIT)


def _resident(shape):
    return pl.BlockSpec(shape, lambda *_: (0,) * len(shape), pipeline_mode=pl.Buffered(1))


def _in_proj_kernel(x_ref, wq_ref, wk_ref, wv_ref, wf_ref, wga_ref, wxl_ref, wgl_ref, bf_ref,
                    q_ref, k_ref, v_ref, kb_ref, vb_ref, logf_ref, ga_ref, xl_ref, gl_ref):
    n_heads = logf_ref.shape[-1]
    xb = x_ref[...].astype(BF16)
    q_ref[...] = (_dot(xb, wq_ref[...]) * Q_SCALE).astype(BF16)
    k = _dot(xb, wk_ref[...])
    k_ref[...] = k
    kb_ref[...] = k.astype(BF16)
    v = _dot(xb, wv_ref[...])
    v_ref[...] = v
    vb_ref[...] = v.astype(BF16)
    zf = _dot(xb, wf_ref[...]) + bf_ref[...]
    logf_ref[...] = _log_sigmoid(zf)[:, :n_heads]
    ga_ref[...] = _silu(_dot(xb, wga_ref[...]))
    xl_ref[...] = _dot(xb, wxl_ref[...])
    gl_ref[...] = _silu(_dot(xb, wgl_ref[...]))


def _in_proj(x2d, wts, tm):
    M, D = x2d.shape
    wq, wk, wv, wf, wga, wxl, wgl, bf, n_heads = wts
    d_attn, d_lru = wq.shape[1], wxl.shape[1]
    row = lambda n: pl.BlockSpec((tm, n), lambda i: (i, 0))
    out_shape = (
        jax.ShapeDtypeStruct((M, d_attn), BF16),
        jax.ShapeDtypeStruct((M, d_attn), F32),
        jax.ShapeDtypeStruct((M, d_attn), F32),
        jax.ShapeDtypeStruct((M, d_attn), BF16),
        jax.ShapeDtypeStruct((M, d_attn), BF16),
        jax.ShapeDtypeStruct((M, n_heads), F32),
        jax.ShapeDtypeStruct((M, d_attn), F32),
        jax.ShapeDtypeStruct((M, d_lru), F32),
        jax.ShapeDtypeStruct((M, d_lru), F32),
    )
    return pl.pallas_call(
        _in_proj_kernel,
        grid=(M // tm,),
        in_specs=[row(D)] + [_resident(w.shape) for w in (wq, wk, wv, wf, wga, wxl, wgl, bf)],
        out_specs=tuple(row(s.shape[1]) for s in out_shape),
        out_shape=out_shape,
        compiler_params=_params("parallel"),
        name="in_proj",
    )(x2d, wq, wk, wv, wf, wga, wxl, wgl, bf)


def _cumsum_lanes(x):
    n = x.shape[-1]
    lane = lax.broadcasted_iota(jnp.int32, x.shape, x.ndim - 1)
    step = 1
    while step < n:
        x = x + jnp.where(lane >= step, pltpu.roll(x, step, x.ndim - 1), 0.0)
        step *= 2
    return x


def _rows(ref):
    return ref[...].reshape(ref.shape[0] * ref.shape[1], ref.shape[2])


def _cumsum_kernel(f_ref, c_ref):
    c_ref[...] = (_cumsum_lanes(_rows(f_ref)) * LOG2E).reshape(c_ref.shape)


def _cumsum_carry_kernel(past_ref, new_ref, cpast_ref, cnew_ref):
    cp = _cumsum_lanes(_rows(past_ref))
    cpast_ref[...] = (cp * LOG2E).reshape(cpast_ref.shape)
    cn = (_cumsum_lanes(_rows(new_ref)) + cp[:, cp.shape[1] - 1:]) * LOG2E
    cnew_ref[...] = cn.reshape(cnew_ref.shape)


CUMSUM_STREAMS = 8


def _cumsum(fT):
    B, H, S = fT.shape
    nb = _tile(B, CUMSUM_STREAMS)
    spec = pl.BlockSpec((nb, H, S), lambda b: (b, 0, 0))
    return pl.pallas_call(
        _cumsum_kernel, grid=(B // nb,), in_specs=[spec], out_specs=spec,
        out_shape=jax.ShapeDtypeStruct(fT.shape, F32),
        compiler_params=_params("parallel"), name="cumsum",
    )(fT)


def _cumsum_carry(pastT, newT):
    B, H, P = pastT.shape
    N = newT.shape[2]
    nb = _tile(B, CUMSUM_STREAMS)
    pspec = pl.BlockSpec((nb, H, P), lambda b: (b, 0, 0))
    nspec = pl.BlockSpec((nb, H, N), lambda b: (b, 0, 0))
    return pl.pallas_call(
        _cumsum_carry_kernel, grid=(B // nb,), in_specs=[pspec, nspec], out_specs=(pspec, nspec),
        out_shape=(jax.ShapeDtypeStruct(pastT.shape, F32), jax.ShapeDtypeStruct(newT.shape, F32)),
        compiler_params=_params("parallel"), name="cumsum_carry",
    )(pastT, newT)


def _head_epilogue(acc, l, gain, gate):
    normed = acc * lax.rsqrt(jnp.mean(acc * acc, axis=-1, keepdims=True) + RMS_EPS * (l * l))
    return (normed * gain * gate).astype(BF16)


def _fox_prompt_kernel(q_ref, kb_ref, vp_ref, cT_ref, ga_ref, gattn_ref, ya_ref,
                       m_s, acc_s, *, tq):
    n_heads = cT_ref.shape[1]
    qi = pl.program_id(1)
    q0 = pl.multiple_of(qi * tq, tq)

    def values(h, k0, rows):
        return vp_ref[0, pl.ds(k0, rows), 2 * h * HEAD_DIM:(2 * h + 2) * HEAD_DIM]

    def scores(h, k0):
        hs = slice(h * HEAD_DIM, (h + 1) * HEAD_DIM)
        s = _dot_nt(q_ref[0, :, hs], kb_ref[0, pl.ds(k0, tq), hs])
        return s - cT_ref[0, h:h + 1, pl.ds(k0, tq)]

    def row_max(s):
        m = s[:, :LANES]
        for c in range(1, s.shape[1] // LANES):
            m = jnp.maximum(m, s[:, c * LANES:(c + 1) * LANES])
        return jnp.max(m, axis=-1, keepdims=True)

    def weights(s, m):
        parts = [jnp.exp2(s[:, c * LANES:(c + 1) * LANES] - m) for c in range(s.shape[1] // LANES)]
        return jnp.concatenate(parts, axis=1).astype(BF16)

    half = tq // 2
    visible_a = (lax.broadcasted_iota(jnp.int32, (tq, half), 1)
                 <= lax.broadcasted_iota(jnp.int32, (tq, half), 0))
    visible_b = (lax.broadcasted_iota(jnp.int32, (half, half), 1)
                 <= lax.broadcasted_iota(jnp.int32, (half, half), 0))

    def diag_scores(h):
        hs = slice(h * HEAD_DIM, (h + 1) * HEAD_DIM)
        sa = _dot_nt(q_ref[0, :, hs], kb_ref[0, pl.ds(q0, half), hs])
        sb = _dot_nt(q_ref[0, half:, hs], kb_ref[0, pl.ds(q0 + half, half), hs])
        return (sa - cT_ref[0, h:h + 1, pl.ds(q0, half)],
                sb - cT_ref[0, h:h + 1, pl.ds(q0 + half, half)])

    s_next = diag_scores(0)
    for h in range(n_heads):
        sa = jnp.where(visible_a, s_next[0], -jnp.inf)
        sb = jnp.where(visible_b, s_next[1], -jnp.inf)
        if h + 1 < n_heads:
            s_next = diag_scores(h + 1)
        s_top = sa[:half]
        s_bot = jnp.concatenate([sa[half:], sb], axis=1)
        m_top = jnp.broadcast_to(row_max(s_top), (half, LANES))
        m_bot = jnp.broadcast_to(row_max(s_bot), (half, LANES))
        m_s[h, :half, :] = m_top
        m_s[h, half:, :] = m_bot
        acc_s[h, :half, :] = _dot(weights(s_top, m_top), values(h, q0, half))
        acc_s[h, half:, :] = _dot(weights(s_bot, m_bot), values(h, q0, tq))

    def body(j, carry):
        k0 = pl.multiple_of(j * tq, tq)
        s_next = scores(0, k0)
        for h in range(n_heads):
            s = s_next
            if h + 1 < n_heads:
                s_next = scores(h + 1, k0)
            m_old = m_s[h]
            m_new = jnp.maximum(m_old, row_max(s))
            alpha = jnp.exp2(m_old - m_new)
            m_s[h] = m_new
            pv = _dot(weights(s, m_new), values(h, k0, tq))
            acc_s[h, :, :HEAD_DIM] = alpha * acc_s[h, :, :HEAD_DIM] + pv[:, :HEAD_DIM]
            acc_s[h, :, HEAD_DIM:] = alpha * acc_s[h, :, HEAD_DIM:] + pv[:, HEAD_DIM:]
        return carry

    lax.fori_loop(0, qi, body, 0)

    for h in range(n_heads):
        hs = slice(h * HEAD_DIM, (h + 1) * HEAD_DIM)
        ya_ref[0, :, hs] = _head_epilogue(acc_s[h, :, :HEAD_DIM], acc_s[h, :, HEAD_DIM:],
                                          gattn_ref[:, hs], ga_ref[0, :, hs])


def _fox_prompt(q, kb, vp, cT, ga, gattn, tq):
    B, S, DA = q.shape
    H = cT.shape[1]
    blk = pl.BlockSpec((1, tq, DA), lambda b, i: (b, i, 0))
    seq = lambda n: pl.BlockSpec((1, S, n), lambda b, i: (b, 0, 0))
    return pl.pallas_call(
        functools.partial(_fox_prompt_kernel, tq=tq),
        grid=(B, S // tq),
        in_specs=[blk, seq(DA), seq(vp.shape[2]), pl.BlockSpec((1, H, S), lambda b, i: (b, 0, 0)),
                  blk, _resident(gattn.shape)],
        out_specs=blk,
        out_shape=jax.ShapeDtypeStruct((B, S, DA), BF16),
        scratch_shapes=[pltpu.VMEM((H, tq, LANES), F32),
                        pltpu.VMEM((H, tq, 2 * HEAD_DIM), F32)],
        compiler_params=_params("parallel", "arbitrary"),
        name="fox_prompt",
    )(q, kb, vp, cT, ga, gattn)


def _fox_sample_kernel(q_ref, ck_hbm, cv_hbm, cpT_ref, kn_ref, vn_ref, cnT_ref, cnc_ref, ga_ref,
                       gattn_ref, ya_ref, kbuf, vbuf, sems, m_s, l_s, acc_s):
    n_heads = cpT_ref.shape[1]
    T = q_ref.shape[1]
    tk = cpT_ref.shape[2]
    ring = kbuf.shape[0]
    b, j = pl.program_id(0), pl.program_id(1)
    nj = pl.num_programs(1)
    total = pl.num_programs(0) * nj
    step = b * nj + j
    slot = lax.rem(step, ring)

    def block_copies(s):
        stream, blk, sl = s // nj, lax.rem(s, nj), lax.rem(s, ring)
        rows = pl.ds(blk * tk, tk)
        copies = []
        for h in range(n_heads):
            copies.append(pltpu.make_async_copy(ck_hbm.at[stream, rows, h, :], kbuf.at[sl, h],
                                                sems.at[0, sl, h]))
            copies.append(pltpu.make_async_copy(cv_hbm.at[stream, rows, h, :], vbuf.at[sl, h],
                                                sems.at[1, sl, h]))
        return copies

    for ahead in range(ring - 1):
        @pl.when((step == 0) & (ahead < total))
        def _(ahead=ahead):
            for i, c in enumerate(block_copies(step + ahead)):
                c.start(priority=i % 2)

    @pl.when(step + ring - 1 < total)
    def _():
        for i, c in enumerate(block_copies(step + ring - 1)):
            c.start(priority=i % 2)

    for c in block_copies(step):
        c.wait()

    @pl.when(j == 0)
    def _():
        m_s[...] = jnp.full(m_s.shape, -jnp.inf, F32)
        l_s[...] = jnp.zeros(l_s.shape, F32)
        acc_s[...] = jnp.zeros(acc_s.shape, F32)

    def update(h, s, vals):
        m_old = m_s[h]
        m_new = jnp.maximum(m_old, jnp.max(s, axis=-1, keepdims=True))
        alpha = jnp.exp2(m_old - m_new)
        p = jnp.exp2(s - m_new)
        m_s[h] = m_new
        l_s[h] = alpha * l_s[h] + jnp.sum(p, axis=-1, keepdims=True)
        acc_s[h] = alpha * acc_s[h] + _dot(p.astype(BF16), vals)

    def cached_scores(h):
        hs = slice(h * HEAD_DIM, (h + 1) * HEAD_DIM)
        kh = kbuf[slot, h].astype(BF16)
        return _dot_nt(q_ref[0, :, hs], kh) + (cnc_ref[0, :, h:h + 1] - cpT_ref[0, h:h + 1, :])

    s_next = cached_scores(0)
    for h in range(n_heads):
        s = s_next
        if h + 1 < n_heads:
            s_next = cached_scores(h + 1)
        update(h, s, vbuf[slot, h].astype(BF16))

    @pl.when(j == pl.num_programs(1) - 1)
    def _():
        row = lax.broadcasted_iota(jnp.int32, (T, T), 0)
        col = lax.broadcasted_iota(jnp.int32, (T, T), 1)
        heads = [slice(h * HEAD_DIM, (h + 1) * HEAD_DIM) for h in range(n_heads)]
        scores = [_dot_nt(q_ref[0, :, hs], kn_ref[0, :, hs])
                  + (cnc_ref[0, :, h:h + 1] - cnT_ref[0, h:h + 1, :T]) for h, hs in enumerate(heads)]
        for h, hs in enumerate(heads):
            update(h, jnp.where(col <= row, scores[h], -jnp.inf), vn_ref[0, :, hs])
        for h, hs in enumerate(heads):
            ya_ref[0, :, hs] = _head_epilogue(acc_s[h], l_s[h], gattn_ref[:, hs], ga_ref[0, :, hs])


def _fox_sample(q, ck, cv, cpT, kn, vn, cnT, cnc, ga, gattn, tk):
    B, T, DA = q.shape
    H = cpT.shape[1]
    P = ck.shape[1]
    new = pl.BlockSpec((1, T, DA), lambda b, j: (b, 0, 0))
    hbm = pl.BlockSpec(memory_space=pl.ANY)
    return pl.pallas_call(
        _fox_sample_kernel,
        grid=(B, P // tk),
        in_specs=[new, hbm, hbm,
                  pl.BlockSpec((1, H, tk), lambda b, j: (b, 0, j)),
                  new, new,
                  pl.BlockSpec((1, H, cnT.shape[2]), lambda b, j: (b, 0, 0)),
                  pl.BlockSpec((1, T, H), lambda b, j: (b, 0, 0)),
                  new, _resident(gattn.shape)],
        out_specs=new,
        out_shape=jax.ShapeDtypeStruct((B, T, DA), BF16),
        scratch_shapes=[pltpu.VMEM((CACHE_RING, H, tk, HEAD_DIM), ck.dtype),
                        pltpu.VMEM((CACHE_RING, H, tk, HEAD_DIM), cv.dtype),
                        pltpu.SemaphoreType.DMA((2, CACHE_RING, H)),
                        pltpu.VMEM((H, T, 1), F32), pltpu.VMEM((H, T, 1), F32),
                        pltpu.VMEM((H, T, HEAD_DIM), F32)],
        compiler_params=_params("arbitrary", "arbitrary"),
        name="fox_sample",
    )(q, ck, cv, cpT, kn, vn, cnT, cnc, ga, gattn)


def _segment_pitch(seg_len):
    pitch = -(-seg_len // SUBLANES) * SUBLANES
    return pitch if (pitch // SUBLANES) % 2 else pitch + SUBLANES


def _rglru_init(t, hist_ref, h0_ref, tail_s, hcar_s):
    @pl.when(t == 0)
    def _():
        tail_s[...] = jnp.zeros(tail_s.shape, F32)
        tail_s[SUBLANES - (CONV_W - 1):, :] = hist_ref[...]
        hcar_s[...] = h0_ref[...]


def _rglru_block(t, xl_ref, gl_ref, cw_ref, cb_ref, wr_ref, br_ref, wi_ref, bi_ref,
                 lam_ref, glru_ref, yl_ref, hT_ref, hist_out_ref, xn_s, hn_s, hl_s, ac_s, tail_s, hcar_s,
                 *, tt, reset_first):
    n_blocks = wr_ref.shape[0]
    seg = tt // SUBLANES
    pitch = _segment_pitch(seg)
    sub = lax.broadcasted_iota(jnp.int32, (SUBLANES, LRU_BLOCK), 0)
    first = (lax.broadcasted_iota(jnp.int32, (tt, LRU_BLOCK), 0) == 0) & (t == 0)
    decay = -LRU_C * _softplus(-lam_ref[...])

    def shift_segments(v, head):
        return jnp.where(sub == 0, head, pltpu.roll(v, 1, 0))

    gated = []
    for n in range(n_blocks):
        ns = slice(n * LRU_BLOCK, (n + 1) * LRU_BLOCK)
        for s in range(SUBLANES):
            xn_s[n, s * pitch:s * pitch + seg, :] = xl_ref[s * seg:(s + 1) * seg, ns]
        x = [xn_s[n, pl.ds(i, SUBLANES, stride=pitch), :] for i in range(seg)]
        before = [shift_segments(x[seg - k], tail_s[SUBLANES - k:SUBLANES - k + 1, ns])
                  for k in range(CONV_W - 1, 0, -1)]
        xs = jnp.concatenate(before + x, axis=0)
        xc = cb_ref[:, ns] + xs[0:tt] * cw_ref[0:1, ns]
        for j in range(1, CONV_W):
            xc = xc + xs[j * SUBLANES:j * SUBLANES + tt] * cw_ref[j:j + 1, ns]

        xcb = xc.astype(BF16)
        gated.append((xc, _dot(xcb, wr_ref[n]), _dot(xcb, wi_ref[n])))

    tail_s[...] = xl_ref[tt - SUBLANES:, :]
    hist_out_ref[...] = xl_ref[tt - (CONV_W - 1):, :]
    _rglru_recurrence(gated, sub, first, decay, shift_segments, br_ref, bi_ref, glru_ref, gl_ref,
                      yl_ref, hT_ref, hn_s, hl_s, ac_s, hcar_s, seg=seg, pitch=pitch,
                      reset_first=reset_first)


def _rglru_recurrence(gated, sub, first, decay, shift_segments, br_ref, bi_ref, glru_ref, gl_ref,
                      yl_ref, hT_ref, hn_s, hl_s, ac_s, hcar_s, *, seg, pitch, reset_first):
    for n, (xc, zr, zi) in enumerate(gated):
        ns = slice(n * LRU_BLOCK, (n + 1) * LRU_BLOCK)
        r = _sigmoid(zr + br_ref[:, ns])
        i_gate = _sigmoid(zi + bi_ref[:, ns])
        log_a = r * decay[:, ns]
        a = jnp.exp(log_a)
        one_minus_a2 = -jnp.tanh(log_a) * (1.0 + a * a)
        mult = jnp.where(one_minus_a2 > 0.0, one_minus_a2 * lax.rsqrt(one_minus_a2), 0.0)
        if reset_first:
            mult = jnp.where(first, 1.0, mult)
        u = mult * i_gate * xc

        rows = lambda v, i: v[i * SUBLANES:(i + 1) * SUBLANES]
        e, p = rows(u, 0), rows(a, 0)
        hl_s[n, 0:SUBLANES, :] = e
        ac_s[n, 0:SUBLANES, :] = p
        for i in range(1, seg):
            e = rows(a, i) * e + rows(u, i)
            p = rows(a, i) * p
            hl_s[n, i * SUBLANES:(i + 1) * SUBLANES, :] = e
            ac_s[n, i * SUBLANES:(i + 1) * SUBLANES, :] = p
        for step in (1, 2, 4):
            keep = sub >= step
            e = e + p * jnp.where(keep, pltpu.roll(e, step, 0), 0.0)
            p = p * jnp.where(keep, pltpu.roll(p, step, 0), 1.0)
        h_end = e + p * hcar_s[:, ns]
        carry_in = shift_segments(h_end, hcar_s[:, ns])
        hcar_s[:, ns] = h_end[SUBLANES - 1:, :]
        hT_ref[:, ns] = h_end[SUBLANES - 1:, :]

        for i in range(seg):
            hn_s[n, pl.ds(i, SUBLANES, stride=pitch), :] = (
                hl_s[n, i * SUBLANES:(i + 1) * SUBLANES, :]
                + ac_s[n, i * SUBLANES:(i + 1) * SUBLANES, :] * carry_in)
        h = jnp.concatenate([hn_s[n, s * pitch:s * pitch + seg, :] for s in range(SUBLANES)], axis=0)
        y = _group_rms(h) * glru_ref[:, ns] * gl_ref[:, ns]
        yl_ref[:, ns] = y.astype(BF16)


def _rglru_kernel(xl_ref, gl_ref, hist_ref, h0_ref, *rest, tt, reset_first):
    weights, (yl_ref, hT_ref, hist_out_ref), scratch = rest[:8], rest[8:11], rest[11:]
    xn_s, hn_s, hl_s, ac_s, tail_s, hcar_s = scratch
    t = pl.program_id(1)
    for g in range(xl_ref.shape[0]):
        _rglru_init(t, hist_ref.at[g], h0_ref.at[g], tail_s.at[g], hcar_s.at[g])
        _rglru_block(t, xl_ref.at[g], gl_ref.at[g], *weights, yl_ref.at[g], hT_ref.at[g],
                     hist_out_ref.at[g], xn_s, hn_s, hl_s, ac_s, tail_s.at[g], hcar_s.at[g],
                     tt=tt, reset_first=reset_first)


def _in_proj_rglru_kernel(x_ref, wq_ref, wk_ref, wv_ref, wf_ref, wga_ref, wxl_ref, wgl_ref, bf_ref,
                          hist_ref, h0_ref, *rest, tt, reset_first):
    lru_w = rest[:8]
    (q_ref, k_ref, v_ref, kb_ref, vp_ref, ga_ref, yl_ref, hT_ref, hist_out_ref,
     logf_t_ref) = rest[8:18]
    xl_s, gl_s, *lru_scratch = rest[18:]
    n_heads = logf_t_ref.shape[1]
    t = pl.program_id(1)
    _rglru_init(t, hist_ref.at[0], h0_ref.at[0], *lru_scratch[4:])
    xb = x_ref[0].astype(BF16)
    xl_s[...] = _dot(xb, wxl_ref[...])
    gl_s[...] = _silu(_dot(xb, wgl_ref[...]))
    _rglru_block(t, xl_s, gl_s, *lru_w, yl_ref.at[0], hT_ref.at[0], hist_out_ref.at[0],
                 *lru_scratch, tt=tt, reset_first=reset_first)
    q_ref[0] = (_dot(xb, wq_ref[...]) * Q_SCALE).astype(BF16)
    k = _dot(xb, wk_ref[...])
    k_ref[0] = k
    kb_ref[0] = k.astype(BF16)
    v = _dot(xb, wv_ref[...])
    v_ref[0] = v
    ones = jnp.ones((tt, HEAD_DIM), BF16)
    for h in range(n_heads):
        vp_ref[0, :, 2 * h * HEAD_DIM:(2 * h + 1) * HEAD_DIM] = (
            v[:, h * HEAD_DIM:(h + 1) * HEAD_DIM].astype(BF16))
        vp_ref[0, :, (2 * h + 1) * HEAD_DIM:(2 * h + 2) * HEAD_DIM] = ones
    logf = _log_sigmoid(_dot(xb, wf_ref[...]) + bf_ref[...])
    logf_t_ref[0] = logf.T[:n_heads, :]
    ga_ref[0] = _silu(_dot(xb, wga_ref[...]))


def _in_proj_rglru(x, wts, hist, h0, lw, tt, reset_first):
    B, S, D = x.shape
    wq, wk, wv, wf, wga, wxl, wgl, bf, n_heads = wts
    d_attn, d_lru = wq.shape[1], wxl.shape[1]
    blk = lambda n: pl.BlockSpec((1, tt, n), lambda b, t: (b, t, 0))
    per_b = lambda n: pl.BlockSpec((1, n, d_lru), lambda b, t: (b, 0, 0))
    slab_rows = SUBLANES * _segment_pitch(tt // SUBLANES)
    seq = lambda n, dt: jax.ShapeDtypeStruct((B, S, n), dt)
    out_shape = (seq(d_attn, BF16), seq(d_attn, F32), seq(d_attn, F32), seq(d_attn, BF16),
                 seq(2 * d_attn, BF16), seq(d_attn, F32), seq(d_lru, BF16),
                 jax.ShapeDtypeStruct((B, 1, d_lru), F32),
                 jax.ShapeDtypeStruct((B, CONV_W - 1, d_lru), F32),
                 jax.ShapeDtypeStruct((B, n_heads, S), F32))
    return pl.pallas_call(
        functools.partial(_in_proj_rglru_kernel, tt=tt, reset_first=reset_first),
        grid=(B, S // tt),
        in_specs=[blk(D)] + [_resident(w.shape) for w in (wq, wk, wv, wf, wga, wxl, wgl, bf)]
                 + [per_b(CONV_W - 1), per_b(1)] + [_resident(w.shape) for w in lw],
        out_specs=tuple(blk(s.shape[2]) for s in out_shape[:7]) + (per_b(1), per_b(CONV_W - 1))
                  + (pl.BlockSpec((1, n_heads, tt), lambda b, t: (b, 0, t)),),
        out_shape=out_shape,
        scratch_shapes=[pltpu.VMEM((tt, d_lru), F32), pltpu.VMEM((tt, d_lru), F32),
                        pltpu.VMEM((lw[2].shape[0], slab_rows, LRU_BLOCK), F32),
                        pltpu.VMEM((lw[2].shape[0], slab_rows, LRU_BLOCK), F32),
                        pltpu.VMEM((lw[2].shape[0], tt, LRU_BLOCK), F32),
                        pltpu.VMEM((lw[2].shape[0], tt, LRU_BLOCK), F32),
                        pltpu.VMEM((SUBLANES, d_lru), F32), pltpu.VMEM((1, d_lru), F32)],
        compiler_params=_params("parallel", "arbitrary"),
        name="in_proj_rglru",
    )(x, wq, wk, wv, wf, wga, wxl, wgl, bf, hist, h0, *lw)


RGLRU_STREAMS = 8


def _rglru(xl, gl, hist, h0, lw, tt, reset_first):
    B, T, DL = xl.shape
    cw, cb, wr, br, wi, bi, lam, glru = lw
    G = _tile(B, RGLRU_STREAMS)
    blk = pl.BlockSpec((G, tt, DL), lambda b, t: (b, t, 0))
    per_b = lambda n: pl.BlockSpec((G, n, DL), lambda b, t: (b, 0, 0))
    slab_rows = SUBLANES * _segment_pitch(tt // SUBLANES)
    return pl.pallas_call(
        functools.partial(_rglru_kernel, tt=tt, reset_first=reset_first),
        grid=(B // G, T // tt),
        in_specs=[blk, blk, per_b(CONV_W - 1), per_b(1)]
                 + [_resident(w.shape) for w in (cw, cb, wr, br, wi, bi, lam, glru)],
        out_specs=(blk, per_b(1), per_b(CONV_W - 1)),
        out_shape=(jax.ShapeDtypeStruct((B, T, DL), BF16),
                   jax.ShapeDtypeStruct((B, 1, DL), F32),
                   jax.ShapeDtypeStruct((B, CONV_W - 1, DL), F32)),
        scratch_shapes=[pltpu.VMEM((wr.shape[0], slab_rows, LRU_BLOCK), F32),
                        pltpu.VMEM((wr.shape[0], slab_rows, LRU_BLOCK), F32),
                        pltpu.VMEM((wr.shape[0], tt, LRU_BLOCK), F32),
                        pltpu.VMEM((wr.shape[0], tt, LRU_BLOCK), F32),
                        pltpu.VMEM((G, SUBLANES, DL), F32), pltpu.VMEM((G, 1, DL), F32)],
        compiler_params=_params("parallel", "arbitrary"),
        name="rglru",
    )(xl, gl, hist, h0, cw, cb, wr, br, wi, bi, lam, glru)


def _out_proj_kernel(x_ref, ya_ref, yl_ref, wa_ref, wl_ref, g_ref, b_ref, o_ref, *, alpha):
    tm = x_ref.shape[0]
    rows = [slice(r, r + OUT_PROJ_ROWS) for r in range(0, tm, OUT_PROJ_ROWS)]
    outs = [_dot(ya_ref[r, :], wa_ref[...]) + _dot(yl_ref[r, :], wl_ref[...]) for r in rows]
    for r, out in zip(rows, outs):
        h = alpha * x_ref[r, :] + out
        mu = jnp.mean(h, axis=-1, keepdims=True)
        d = h - mu
        var = jnp.mean(d * d, axis=-1, keepdims=True)
        o_ref[r, :] = d * lax.rsqrt(var + LN_EPS) * g_ref[...] + b_ref[...]


def _out_proj(x2d, ya, yl, wa, wl, g, b, alpha, tm):
    M, D = x2d.shape
    row = lambda n: pl.BlockSpec((tm, n), lambda i: (i, 0))
    return pl.pallas_call(
        functools.partial(_out_proj_kernel, alpha=alpha),
        grid=(M // tm,),
        in_specs=[row(D), row(ya.shape[1]), row(yl.shape[1])]
                 + [_resident(w.shape) for w in (wa, wl, g, b)],
        out_specs=row(D),
        out_shape=jax.ShapeDtypeStruct((M, D), F32),
        compiler_params=_params("parallel"),
        name="out_proj",
    )(x2d, ya, yl, wa, wl, g, b)


def _split_weights_kernel(wt_hbm, wo_hbm, wq_ref, wk_ref, wv_ref, wf_ref, wga_ref, wxl_ref, wgl_ref,
                          wa_ref, wl_ref, buf, fbuf, sems, *, splits):
    n_heads = splits[3] - splits[2]
    half = wa_ref.shape[0]
    pieces = ((wt_hbm, 0, wq_ref, True), (wt_hbm, splits[0], wk_ref, True),
              (wt_hbm, splits[1], wv_ref, True), (wt_hbm, splits[3], wga_ref, True),
              (wt_hbm, splits[4], wxl_ref, True), (wt_hbm, splits[5], wgl_ref, True),
              (wo_hbm, 0, wa_ref, False), (wo_hbm, half, wl_ref, False))

    def piece_rows(i):
        _, _, out, transposed = pieces[i]
        return out.shape[1] if transposed else out.shape[0]

    def piece_copy(i):
        src, start, _, _ = pieces[i]
        rows = piece_rows(i)
        return pltpu.make_async_copy(src.at[pl.ds(start, rows), :], buf.at[i % 2, :rows, :],
                                     sems.at[i % 2])

    forget_copy = pltpu.make_async_copy(wt_hbm.at[pl.ds(splits[2], n_heads), :],
                                        fbuf.at[:n_heads, :], sems.at[2])
    fbuf[...] = jnp.zeros(fbuf.shape, F32)
    forget_copy.start()
    piece_copy(0).start()
    for i, (_, _, out, transposed) in enumerate(pieces):
        if i + 1 < len(pieces):
            piece_copy(i + 1).start()
        piece_copy(i).wait()
        rows = buf[i % 2, :piece_rows(i), :]
        out[...] = (rows.T if transposed else rows).astype(BF16)
    forget_copy.wait()
    wf_ref[...] = fbuf[...].T.astype(BF16)


def _split_weights(wt, wo, splits, d_attn):
    DIN, D = wt.shape
    widths = (splits[0], splits[1] - splits[0], splits[2] - splits[1], LANES,
              splits[4] - splits[3], splits[5] - splits[4], DIN - splits[5])
    assert wo.shape == (2 * d_attn, D), "w_out is split into two equal row halves"
    hbm = pl.BlockSpec(memory_space=pl.ANY)
    return pl.pallas_call(
        functools.partial(_split_weights_kernel, splits=splits),
        in_specs=[hbm, hbm],
        out_shape=tuple(jax.ShapeDtypeStruct((D, n), BF16) for n in widths)
                  + (jax.ShapeDtypeStruct((d_attn, D), BF16),) * 2,
        scratch_shapes=[pltpu.VMEM((2, max(max(widths), d_attn), D), F32),
                        pltpu.VMEM((LANES, D), F32), pltpu.SemaphoreType.DMA((3,))],
        compiler_params=pltpu.CompilerParams(vmem_limit_bytes=V7X_VMEM_LIMIT),
        name="split_weights",
    )(wt, wo)


def _tile(n, pref):
    return pref if n % pref == 0 else n


def kernel(x_prompt, x_sample, cache_k, cache_v, cache_logf, state_h, state_conv, w_in, b_f, conv_w,
           conv_b, w_r, b_r, w_i, b_i, lru_lambda, g_attn, g_lru, w_out, ln_g, ln_b):
    depth, d_model, _ = w_in.shape
    n_heads = b_f.shape[1]
    d_attn = n_heads * HEAD_DIM
    d_lru = lru_lambda.shape[1]
    alpha = (2.0 * depth) ** 0.25
    B, S, _ = x_prompt.shape
    DB, T, _ = x_sample.shape
    P = cache_k.shape[2]

    xp = x_prompt.reshape(B * S, d_model)
    xs = x_sample.reshape(DB * T, d_model)
    outs_p, outs_s = [], []
    for l in range(depth):
        o = (d_attn, 2 * d_attn, 3 * d_attn, 3 * d_attn + n_heads, 4 * d_attn + n_heads,
             4 * d_attn + n_heads + d_lru)
        bf = jnp.pad(b_f[l][None, :], ((0, 0), (0, LANES - n_heads)))
        *w_proj, wa, wl = _split_weights(jnp.swapaxes(w_in, 1, 2)[l], w_out[l], o, d_attn)
        wts = tuple(w_proj) + (bf, n_heads)
        lw = (conv_w[l], conv_b[l][None], w_r[l].astype(BF16), b_r[l][None], w_i[l].astype(BF16),
              b_i[l][None], lru_lambda[l][None], g_lru[l][None])
        gattn = g_attn[l][None]
        assert d_lru == d_attn, "w_out is split into two equal row halves"
        lng, lnb = ln_g[l][None], ln_b[l][None]

        q, k, v, kb, vp, ga, yl, h_T, hist_T, logf_t = _in_proj_rglru(
            xp.reshape(B, S, d_model), wts, jnp.zeros((B, CONV_W - 1, d_lru), F32),
            jnp.zeros((B, 1, d_lru), F32), lw, _tile(S, 256), True)
        cT = _cumsum(logf_t)
        ya = _fox_prompt(q, kb, vp, cT, ga, gattn, _tile(S, 512))
        outs_p.append((k.reshape(B, S, n_heads, HEAD_DIM), v.reshape(B, S, n_heads, HEAD_DIM),
                       jnp.transpose(logf_t, (0, 2, 1)), h_T.reshape(B, d_lru), hist_T))
        xp = _out_proj(xp, ya.reshape(B * S, d_attn), yl.reshape(B * S, d_lru), wa, wl, lng, lnb,
                       alpha, _tile(B * S, 512))

        q, k, v, kb, vb, logf, ga, xl, gl = _in_proj(xs, wts, _tile(DB * T, 256))
        r3 = lambda a: a.reshape(DB, T, a.shape[-1])
        newT = jnp.pad(jnp.transpose(r3(logf), (0, 2, 1)), ((0, 0), (0, 0), (0, LANES - T)))
        cpT, cnT = _cumsum_carry(jnp.transpose(cache_logf[l].astype(F32), (0, 2, 1)), newT)
        cnc = jnp.transpose(cnT[:, :, :T], (0, 2, 1))
        ya = _fox_sample(r3(q), cache_k[l], cache_v[l],
                         cpT, r3(kb), r3(vb), cnT, cnc, r3(ga), gattn, _tile(P, 2048))
        yl, h_T, hist_T = _rglru(r3(xl), r3(gl), state_conv[l], state_h[l][:, None, :], lw,
                                 _tile(T, 256), False)
        outs_s.append((k.reshape(DB, T, n_heads, HEAD_DIM), v.reshape(DB, T, n_heads, HEAD_DIM),
                       r3(logf), h_T.reshape(DB, d_lru), hist_T))
        xs = _out_proj(xs, ya.reshape(DB * T, d_attn), yl.reshape(DB * T, d_lru), wa, wl, lng, lnb,
                       alpha, _tile(DB * T, 256))

    stack = lambda outs, i: jnp.stack([o[i] for o in outs], 0)
    return (xp.reshape(B, S, d_model), xs.reshape(DB, T, d_model),
            stack(outs_p, 0), stack(outs_p, 1), stack(outs_p, 2), stack(outs_p, 3), stack(outs_p, 4),
            stack(outs_s, 0), stack(outs_s, 1), stack(outs_s, 2), stack(outs_s, 3), stack(outs_s, 4))
```
